```python
import math
import jax
import jax.numpy as jnp
from jax import lax
import numpy as np

D_MODEL = 1024
BATCH = 2
SEQ = 8192
DEPTH = 2

N_EVEN = (DEPTH + 1) // 2
N_ODD = DEPTH // 2
RMS_EPS = 1e-6

GDN_HEADS = 8
GDN_DK = 128
GDN_DV = 128
GDN_CONV = 4
GDN_CHUNK = 64
GDN_QK = GDN_HEADS * GDN_DK
GDN_V = GDN_HEADS * GDN_DV

S5_WIDTH = D_MODEL
S5_GROUP = 16
S5_GROUPS = S5_WIDTH // S5_GROUP
S5_STATE = 64
S5_MIN_NEG = 1e-4

EVEN_SIZES = (2 * GDN_QK + GDN_V, GDN_V, GDN_HEADS, GDN_HEADS, S5_WIDTH, S5_WIDTH)
EVEN_PROJ = sum(EVEN_SIZES)
EVEN_MIX = GDN_V + S5_WIDTH

SC_WIDTH = 2 * D_MODEL
SC_CONV = 3

kernel_name = "hybrid_gdn_s5_shortconv_sandwich"


def rms_norm(x, w):
    xf = x.astype(jnp.float32)
    xf = xf * lax.rsqrt(jnp.mean(xf * xf, axis=-1, keepdims=True) + RMS_EPS)
    return (xf * w.astype(jnp.float32)).astype(x.dtype)


def l2_normalize(x):
    return x * lax.rsqrt(jnp.sum(x * x, axis=-1, keepdims=True) + RMS_EPS)


def causal_dwconv(x, w):
    k, c = w.shape
    return lax.conv_general_dilated(
        x, w[:, None, :], window_strides=(1,), padding=[(k - 1, 0)],
        dimension_numbers=("NWC", "WIO", "NWC"), feature_group_count=c)


def split_cols(t, sizes):
    idx = [int(i) for i in np.cumsum(sizes)[:-1]]
    return jnp.split(t, idx, axis=-1)


def gated_delta_rule_chunked(q, k, v, beta, g):
    bsz, s, h, dk = q.shape
    dv = v.shape[-1]
    c = GDN_CHUNK
    n = s // c

    def to_chunks(t):
        t = t.reshape((bsz, n, c, h) + t.shape[3:])
        return jnp.moveaxis(t, 3, 1)

    q, k, v, beta, g = (to_chunks(t) for t in (q, k, v, beta, g))
    gc = jnp.cumsum(g, axis=-1)
    causal = jnp.tril(jnp.ones((c, c), dtype=bool))
    strict = jnp.tril(jnp.ones((c, c), dtype=bool), -1)
    decay = jnp.exp(jnp.where(causal, gc[..., :, None] - gc[..., None, :], -jnp.inf))
    kb = k * beta[..., None]
    lower = jnp.where(strict, jnp.einsum("bhncd,bhnmd->bhncm", kb, k) * decay, 0.0)
    eye = jnp.eye(c, dtype=jnp.float32)
    rhs = jnp.concatenate([v * beta[..., None], kb * jnp.exp(gc)[..., None]], axis=-1)
    sol = lax.linalg.triangular_solve(lower + eye, rhs, left_side=True, lower=True,
                                      unit_diagonal=True)
    u_c, w_c = sol[..., :dv], sol[..., dv:]
    attn = jnp.einsum("bhncd,bhnmd->bhncm", q, k) * decay
    q_dec = q * jnp.exp(gc)[..., None]
    k_dec = k * jnp.exp(gc[..., -1:] - gc)[..., None]
    g_last = jnp.exp(gc[..., -1])

    def step(state, xs):
        q_n, a_n, u_n, w_n, k_n, gl_n = xs
        v_new = u_n - jnp.einsum("bhcd,bhde->bhce", w_n, state)
        o_n = (jnp.einsum("bhcd,bhde->bhce", q_n, state)
               + jnp.einsum("bhcm,bhme->bhce", a_n, v_new))
        state = state * gl_n[..., None, None] + jnp.einsum("bhcd,bhce->bhde", k_n, v_new)
        return state, o_n

    xs = tuple(jnp.moveaxis(t, 2, 0) for t in (q_dec, attn, u_c, w_c, k_dec, g_last))
    state0 = jnp.zeros((bsz, h, dk, dv), jnp.float32)
    _, o = lax.scan(step, state0, xs)
    return o.transpose(1, 0, 3, 2, 4).reshape(bsz, s, h, dv)


def _complex_affine_combine(e1, e2):
    a1r, a1i, b1r, b1i = e1
    a2r, a2i, b2r, b2i = e2
    return (a2r * a1r - a2i * a1i,
            a2r * a1i + a2i * a1r,
            a2r * b1r - a2i * b1i + b2r,
            a2r * b1i + a2i * b1r + b2i)


def s5_ssm(u, lam_re, lam_im, b_re, b_im, c_re, c_im, log_dt, d):
    bsz, s, _ = u.shape
    f32 = jnp.float32
    uf = u.astype(f32).reshape(bsz, s, S5_GROUPS, S5_GROUP)
    lr = jnp.minimum(lam_re.astype(f32), -S5_MIN_NEG)
    li = lam_im.astype(f32)
    dt = jnp.exp(log_dt.astype(f32))[:, None]
    mag = jnp.exp(lr * dt)
    ab_re = mag * jnp.cos(li * dt)
    ab_im = mag * jnp.sin(li * dt)
    den = lr * lr + li * li
    nr, ni = ab_re - 1.0, ab_im
    f_re = (nr * lr + ni * li) / den
    f_im = (ni * lr - nr * li) / den
    br, bi = b_re.astype(f32), b_im.astype(f32)
    bb_re = f_re[..., None] * br - f_im[..., None] * bi
    bb_im = f_re[..., None] * bi + f_im[..., None] * br
    bu_re = jnp.einsum("bsgh,gph->bsgp", uf, bb_re)
    bu_im = jnp.einsum("bsgh,gph->bsgp", uf, bb_im)
    a_re = jnp.broadcast_to(ab_re, (1, s, S5_GROUPS, S5_STATE))
    a_im = jnp.broadcast_to(ab_im, (1, s, S5_GROUPS, S5_STATE))
    _, _, x_re, x_im = lax.associative_scan(
        _complex_affine_combine, (a_re, a_im, bu_re, bu_im), axis=1)
    y = (jnp.einsum("bsgp,ghp->bsgh", x_re, c_re.astype(f32))
         - jnp.einsum("bsgp,ghp->bsgh", x_im, c_im.astype(f32)))
    y = y.reshape(bsz, s, S5_WIDTH) + d.astype(f32) * u.astype(f32)
    return y.astype(u.dtype)


def even_mixer(h, w_in, conv_w, a_log, dt_bias, gdn_norm_w, lam_re, lam_im,
               b_re, b_im, c_re, c_im, log_dt, s5_d, w_glu, w_out):
    bsz, s, _ = h.shape
    f32 = jnp.float32
    proj = h @ w_in
    qkv, z_a, b_raw, a_raw, u, z_b = split_cols(proj, EVEN_SIZES)
    qkv = jax.nn.silu(causal_dwconv(qkv, conv_w))
    q, k, v = split_cols(qkv, (GDN_QK, GDN_QK, GDN_V))
    q = l2_normalize(q.astype(f32).reshape(bsz, s, GDN_HEADS, GDN_DK)) * (GDN_DK ** -0.5)
    k = l2_normalize(k.astype(f32).reshape(bsz, s, GDN_HEADS, GDN_DK))
    v = v.astype(f32).reshape(bsz, s, GDN_HEADS, GDN_DV)
    beta = jax.nn.sigmoid(b_raw.astype(f32))
    g = -jnp.exp(a_log.astype(f32)) * jax.nn.softplus(a_raw.astype(f32) + dt_bias.astype(f32))
    o = gated_delta_rule_chunked(q, k, v, beta, g)
    o = rms_norm(o, gdn_norm_w).reshape(bsz, s, GDN_V).astype(h.dtype)
    y_a = o * jax.nn.silu(z_a)
    y = jax.nn.gelu(s5_ssm(u, lam_re, lam_im, b_re, b_im, c_re, c_im, log_dt, s5_d))
    y = y * jax.nn.sigmoid(y @ w_glu)
    y_b = y * jax.nn.silu(z_b)
    return jnp.concatenate([y_a, y_b], axis=-1) @ w_out


def odd_mixer(h, w_in, conv_w, w_out):
    gb, gc, hv, z = split_cols(h @ w_in, (SC_WIDTH, SC_WIDTH, SC_WIDTH, SC_WIDTH))
    y = gb * causal_dwconv(gc * hv, conv_w)
    return (y * jax.nn.silu(z)) @ w_out


def setup_inputs(seed: int = 0) -> dict:
    key = jax.random.key(seed)
    ks = jax.random.split(key, 24)
    f32 = jnp.float32
    nrm = lambda k, shp, sc: jax.random.normal(k, shp, f32) * sc
    log_lo, log_hi = math.log(1e-3), math.log(1e-1)
    dt0 = jnp.exp(jax.random.uniform(ks[5], (N_EVEN, GDN_HEADS), f32, log_lo, log_hi))
    lam_im0 = jnp.pi * jnp.arange(S5_STATE, dtype=f32)
    return {
        "x": nrm(ks[0], (BATCH, SEQ, D_MODEL), 1.0),
        "norm_pre": 1.0 + nrm(ks[1], (DEPTH, D_MODEL), 0.02),
        "norm_post": 1.0 + nrm(ks[2], (DEPTH, D_MODEL), 0.02),
        "w_in_even": nrm(ks[3], (N_EVEN, D_MODEL, EVEN_PROJ), D_MODEL ** -0.5),
        "conv_qkv": nrm(ks[4], (N_EVEN, GDN_CONV, 2 * GDN_QK + GDN_V), GDN_CONV ** -0.5),
        "a_log": jnp.log(jax.random.uniform(ks[6], (N_EVEN, GDN_HEADS), f32, 1.0, 16.0)),
        "dt_bias": dt0 + jnp.log(-jnp.expm1(-dt0)),
        "gdn_norm_w": 1.0 + nrm(ks[7], (N_EVEN, GDN_DV), 0.02),
        "s5_lam_re": -0.5 + nrm(ks[8], (N_EVEN, S5_GROUPS, S5_STATE), 1e-3),
        "s5_lam_im": lam_im0 + nrm(ks[9], (N_EVEN, S5_GROUPS, S5_STATE), 1e-3),
        "s5_b_re": nrm(ks[10], (N_EVEN, S5_GROUPS, S5_STATE, S5_GROUP), (2 * S5_GROUP) ** -0.5),
        "s5_b_im": nrm(ks[11], (N_EVEN, S5_GROUPS, S5_STATE, S5_GROUP), (2 * S5_GROUP) ** -0.5),
        "s5_c_re": nrm(ks[12], (N_EVEN, S5_GROUPS, S5_GROUP, S5_STATE), S5_STATE ** -0.5),
        "s5_c_im": nrm(ks[13], (N_EVEN, S5_GROUPS, S5_GROUP, S5_STATE), S5_STATE ** -0.5),
        "s5_log_dt": jax.random.uniform(ks[14], (N_EVEN, S5_GROUPS), f32, log_lo, log_hi),
        "s5_d": nrm(ks[15], (N_EVEN, S5_WIDTH), 1.0),
        "w_glu": nrm(ks[16], (N_EVEN, S5_WIDTH, S5_WIDTH), S5_WIDTH ** -0.5),
        "w_out_even": nrm(ks[17], (N_EVEN, EVEN_MIX, D_MODEL), EVEN_MIX ** -0.5),
        "w_in_odd": nrm(ks[18], (N_ODD, D_MODEL, 4 * SC_WIDTH), D_MODEL ** -0.5),
        "conv_short": nrm(ks[19], (N_ODD, SC_CONV, SC_WIDTH), SC_CONV ** -0.5),
        "w_out_odd": nrm(ks[20], (N_ODD, SC_WIDTH, D_MODEL), SC_WIDTH ** -0.5),
    }


def reference(x, norm_pre, norm_post, w_in_even, conv_qkv, a_log, dt_bias, gdn_norm_w,
              s5_lam_re, s5_lam_im, s5_b_re, s5_b_im, s5_c_re, s5_c_im, s5_log_dt, s5_d,
              w_glu, w_out_even, w_in_odd, conv_short, w_out_odd):
    for layer in range(DEPTH):
        i = layer // 2
        h = rms_norm(x, norm_pre[layer])
        if layer % 2 == 0:
            y = even_mixer(h, w_in_even[i], conv_qkv[i], a_log[i], dt_bias[i], gdn_norm_w[i],
                           s5_lam_re[i], s5_lam_im[i], s5_b_re[i], s5_b_im[i],
                           s5_c_re[i], s5_c_im[i], s5_log_dt[i], s5_d[i],
                           w_glu[i], w_out_even[i])
        else:
            y = odd_mixer(h, w_in_odd[i], conv_short[i], w_out_odd[i])
        x = x + rms_norm(y, norm_post[layer]).astype(x.dtype)
    return x
```

```python
import functools
import math

import jax
import jax.numpy as jnp
from jax import lax
from jax.experimental import pallas as pl
from jax.experimental.pallas import tpu as pltpu

F32 = jnp.float32
BF16 = jnp.bfloat16

D_MODEL = 1024
BATCH = 2
SEQ = 8192
TOKENS = BATCH * SEQ
RMS_EPS = 1e-6

GDN_HEADS = 8
GDN_DK = 128
GDN_CONV = 4
GDN_CHUNK = 64
GDN_QKV = 3 * GDN_HEADS * GDN_DK

S5_GROUP = 16
S5_GROUPS = D_MODEL // S5_GROUP
S5_STATE = 64
S5_MIN_NEG = 1e-4
S5_CHUNK = 16
S5_SEGS = 8
S5_CPS = SEQ // (S5_CHUNK * S5_SEGS)
S5_ROWS = BATCH * SEQ // S5_CHUNK
S5_W = S5_CHUNK * S5_GROUP

SC_WIDTH = 2 * D_MODEL
SC_CONV = 3

HALO = 8
VMEM_LIMIT = 56 * 1024 * 1024

TM = 256
GDN_TILE = 256


def _dot(a, b):
    return jnp.dot(a, b, preferred_element_type=F32)


def _dot_nt(a, b):
    return lax.dot_general(a, b, (((1,), (1,)), ((), ())), preferred_element_type=F32)


def _dot_tn(a, b):
    return lax.dot_general(a, b, (((0,), (0,)), ((), ())), preferred_element_type=F32)


def _split3(x):
    x1 = x.astype(BF16)
    r = x - x1.astype(F32)
    x2 = r.astype(BF16)
    x3 = (r - x2.astype(F32)).astype(BF16)
    return x1, x2, x3


def _dot3_nt(a, b):
    a1, a2, a3 = _split3(a)
    b1, b2, b3 = _split3(b)
    return (_dot_nt(a1, b1) + _dot_nt(a1, b2) + _dot_nt(a2, b1)
            + _dot_nt(a2, b2) + _dot_nt(a1, b3) + _dot_nt(a3, b1))


def _sigmoid(x):
    return 1.0 / (1.0 + jnp.exp(-x))


def _silu(x):
    return x * _sigmoid(x)


def _rms_scale(x):
    return x * lax.rsqrt(jnp.mean(x * x, axis=-1, keepdims=True) + RMS_EPS)


def _proj0_kernel(x_ref, npre_ref, wqkv_ref, wza_ref, wba_ref, wu_ref, wzb_ref, convw_ref,
                  alog_ref, dtb_ref,
                  q_ref, k_ref, v_ref, za_ref, bg_ref, u_ref, zb_ref, ext_ref):
    tiles_per_seq = SEQ // TM

    @pl.when(pl.program_id(0) % tiles_per_seq == 0)
    def _():
        ext_ref[0:HALO, :] = jnp.zeros((HALO, GDN_QKV), F32)

    h = (_rms_scale(x_ref[...]) * npre_ref[...]).astype(BF16)

    cw = 512
    heads_per_chunk = cw // GDN_DK
    outs = (q_ref, k_ref, v_ref)
    for c in range(GDN_QKV // cw):
        cols = slice(c * cw, (c + 1) * cw)
        raw = _dot(h, wqkv_ref[:, cols])
        ext_ref[HALO:HALO + TM, cols] = raw
        w = convw_ref[:, cols]
        acc = raw * w[3:4]
        for j in range(GDN_CONV - 1):
            shift = GDN_CONV - 1 - j
            acc = acc + ext_ref[HALO - shift:HALO - shift + TM, cols] * w[j:j + 1]
        s = _silu(acc)
        which = (c * cw) // (GDN_HEADS * GDN_DK)
        out_ref = outs[which]
        col0 = c * cw - which * GDN_HEADS * GDN_DK
        for hh in range(heads_per_chunk):
            sh = s[:, hh * GDN_DK:(hh + 1) * GDN_DK]
            if which < 2:
                sh = sh * lax.rsqrt(jnp.sum(sh * sh, axis=-1, keepdims=True) + RMS_EPS)
            if which == 0:
                sh = sh * (GDN_DK ** -0.5)
            out_ref[:, col0 + hh * GDN_DK:col0 + (hh + 1) * GDN_DK] = sh.astype(out_ref.dtype)
    ext_ref[0:HALO, :] = ext_ref[TM:TM + HALO, :]

    za_ref[...] = _silu(_dot(h, wza_ref[...])).astype(za_ref.dtype)
    zb_ref[...] = _silu(_dot(h, wzb_ref[...])).astype(zb_ref.dtype)
    u_ref[...] = _dot(h, wu_ref[...]).astype(u_ref.dtype)

    ba = _dot(h, wba_ref[...])
    beta = _sigmoid(ba)
    xs = ba + dtb_ref[...]
    softplus = jnp.maximum(xs, 0.0) + jnp.log(1.0 + jnp.exp(-jnp.abs(xs)))
    g = -jnp.exp(alog_ref[...]) * softplus
    lane = lax.broadcasted_iota(jnp.int32, ba.shape, 1)
    bg_ref[...] = jnp.where(lane < GDN_HEADS, beta, g)


def _const_spec(shape):
    nd = len(shape)
    return pl.BlockSpec(shape, lambda *_: (0,) * nd)


def _proj0(x2, npre, wqkv, wza, wba, wu, wzb, convw, alog, dtb):
    n_tiles = TOKENS // TM
    tok = lambda cols: pl.BlockSpec((TM, cols), lambda i: (i, 0))
    hd = GDN_HEADS * GDN_DK
    out_shape = (
        jax.ShapeDtypeStruct((TOKENS, hd), BF16),
        jax.ShapeDtypeStruct((TOKENS, hd), BF16),
        jax.ShapeDtypeStruct((TOKENS, hd), BF16),
        jax.ShapeDtypeStruct((TOKENS, hd), BF16),
        jax.ShapeDtypeStruct((TOKENS, 128), F32),
        jax.ShapeDtypeStruct((TOKENS, D_MODEL), BF16),
        jax.ShapeDtypeStruct((TOKENS, D_MODEL), BF16),
    )
    return pl.pallas_call(
        _proj0_kernel,
        grid=(n_tiles,),
        in_specs=[tok(D_MODEL), _const_spec((1, D_MODEL)), _const_spec(wqkv.shape),
                  _const_spec(wza.shape), _const_spec(wba.shape), _const_spec(wu.shape),
                  _const_spec(wzb.shape), _const_spec(convw.shape), _const_spec((1, 128)),
                  _const_spec((1, 128))],
        out_specs=(tok(hd), tok(hd), tok(hd), tok(hd), tok(128), tok(D_MODEL), tok(D_MODEL)),
        out_shape=out_shape,
        scratch_shapes=[pltpu.VMEM((TM + HALO, GDN_QKV), F32)],
        compiler_params=pltpu.CompilerParams(dimension_semantics=("arbitrary",),
                                             vmem_limit_bytes=VMEM_LIMIT),
        name="proj0",
    )(x2, npre, wqkv, wza, wba, wu, wzb, convw, alog, dtb)


def _gdn_kernel(q_ref, k_ref, v_ref, bg_ref, za_ref, gnw_ref, ya_ref, state_ref):
    c = GDN_CHUNK

    @pl.when(pl.program_id(1) == 0)
    def _():
        state_ref[...] = jnp.zeros(state_ref.shape, F32)

    row = lax.broadcasted_iota(jnp.int32, (c, c), 0)
    col = lax.broadcasted_iota(jnp.int32, (c, c), 1)
    causal = row >= col
    strict = row > col
    tri = jnp.where(causal, 1.0, 0.0).astype(BF16)
    eye = jnp.where(row == col, 1.0, 0.0).astype(F32)
    lane = lax.broadcasted_iota(jnp.int32, (c, 128), 1)
    gnw = gnw_ref[...]

    def chunk_body(ci, carry):
        r0 = pl.multiple_of(ci * c, c)
        rows = pl.ds(r0, c)
        bg = bg_ref[rows, :]
        g_pad = jnp.where(lane >= GDN_HEADS, bg, 0.0)
        g1, g2, g3 = _split3(g_pad)
        gc = _dot(tri, g1) + _dot(tri, g2) + _dot(tri, g3)
        gc_t = gc.T
        gc_last = gc[c - 1:c, :]
        e_gc = jnp.exp(gc)
        e_rev = jnp.exp(gc_last - gc)
        e_last = jnp.exp(gc_last)

        for hd in range(GDN_HEADS):
            cols = slice(hd * GDN_DK, (hd + 1) * GDN_DK)
            qh = q_ref[rows, cols].astype(F32)
            kh_b = k_ref[rows, cols]
            kh = kh_b.astype(F32)
            vh = v_ref[rows, cols].astype(F32)
            beta = bg[:, hd:hd + 1]
            gl = GDN_HEADS + hd
            egc = e_gc[:, gl:gl + 1]
            erev = e_rev[:, gl:gl + 1]
            elast = e_last[:, gl:gl + 1]
            diff = gc[:, gl:gl + 1] - gc_t[gl:gl + 1, :]
            decay = jnp.exp(jnp.minimum(diff, 0.0))

            kb = kh * beta
            kb_b = kb.astype(BF16)
            a = jnp.where(strict, _dot_nt(kb_b, kh_b) * decay, 0.0)
            attn = jnp.where(causal, _dot_nt(qh.astype(BF16), kh_b) * decay, 0.0)

            t = eye - a
            p = a
            for _ in range(5):
                pb = p.astype(BF16)
                p = _dot(pb, pb)
                t = t + _dot(t.astype(BF16), p.astype(BF16))
            rhs = jnp.concatenate([vh * beta, kb * egc], axis=1).astype(BF16)
            sol = _dot(t.astype(BF16), rhs)
            u_c = sol[:, :GDN_DK]
            w_c = sol[:, GDN_DK:]

            s_h = state_ref[hd]
            s_b = s_h.astype(BF16)
            wq = jnp.concatenate([w_c, qh * egc], axis=0).astype(BF16)
            wqs = _dot(wq, s_b)
            v_new = u_c - wqs[:c]
            v_new_b = v_new.astype(BF16)
            o = wqs[c:] + _dot(attn.astype(BF16), v_new_b)
            k_dec = (kh * erev).astype(BF16)
            state_ref[hd] = s_h * elast + _dot_tn(k_dec, v_new_b)

            y = _rms_scale(o) * gnw * za_ref[rows, cols].astype(F32)
            ya_ref[rows, cols] = y.astype(ya_ref.dtype)
        return carry

    lax.fori_loop(0, GDN_TILE // c, chunk_body, 0)


def _gdn(q, k, v, bg, za, gnw):
    nt = SEQ // GDN_TILE
    hd = GDN_HEADS * GDN_DK
    tok = lambda cols: pl.BlockSpec((GDN_TILE, cols), lambda b, j: (b * nt + j, 0))
    return pl.pallas_call(
        _gdn_kernel,
        grid=(BATCH, nt),
        in_specs=[tok(hd), tok(hd), tok(hd), tok(128), tok(hd), _const_spec((1, GDN_DK))],
        out_specs=tok(hd),
        out_shape=jax.ShapeDtypeStruct((TOKENS, hd), BF16),
        scratch_shapes=[pltpu.VMEM((GDN_HEADS, GDN_DK, GDN_DK), F32)],
        compiler_params=pltpu.CompilerParams(dimension_semantics=("arbitrary", "arbitrary"),
                                             vmem_limit_bytes=VMEM_LIMIT),
        name="gdn",
    )(q, k, v, bg, za, gnw)


S5_PREP_GROUPS = 8


def _s5_prep_kernel(lre_ref, lim_ref, ldt_ref, bre_ref, bim_ref, cre_ref, cim_ref, d_ref,
                    toep_ref, m1r_ref, m1i_ref, m3r_ref, m3i_ref, a16r_ref, a16i_ref):
    n = S5_CHUNK
    row = lax.broadcasted_iota(jnp.int32, (S5_W, S5_W), 0)
    col = lax.broadcasted_iota(jnp.int32, (S5_W, S5_W), 1)
    colblk = col // S5_GROUP
    for gi in range(S5_PREP_GROUPS):
        lr = jnp.minimum(lre_ref[gi], -S5_MIN_NEG)
        li = lim_ref[gi]
        dt = jnp.exp(ldt_ref[gi])
        mag = jnp.exp(lr * dt)
        ar = mag * jnp.cos(li * dt)
        ai = mag * jnp.sin(li * dt)
        den = lr * lr + li * li
        nr, ni = ar - 1.0, ai
        fr = (nr * lr + ni * li) / den
        fi = (ni * lr - nr * li) / den
        br, bi = bre_ref[gi], bim_ref[gi]
        bbr = fr * br - fi * bi
        bbi = fr * bi + fi * br
        pr = [jnp.ones_like(ar)]
        pi = [jnp.zeros_like(ar)]
        for _ in range(n):
            pr.append(pr[-1] * ar - pi[-1] * ai)
            pi.append(pr[-2] * ai + pi[-1] * ar)
        m1r = jnp.concatenate([pr[n - 1 - s] * bbr - pi[n - 1 - s] * bbi for s in range(n)], axis=0)
        m1i = jnp.concatenate([pr[n - 1 - s] * bbi + pi[n - 1 - s] * bbr for s in range(n)], axis=0)
        cr, ci = cre_ref[gi], cim_ref[gi]
        m3r = jnp.concatenate([cr * pr[t + 1] - ci * pi[t + 1] for t in range(n)], axis=0)
        m3i = jnp.concatenate([-(cr * pi[t + 1] + ci * pr[t + 1]) for t in range(n)], axis=0)
        cer = jnp.concatenate([cr] * n, axis=0)
        cei = jnp.concatenate([ci] * n, axis=0)
        rt = _dot3_nt(m1r, cer) - _dot3_nt(m1i, cei)
        toep = jnp.zeros((S5_W, S5_W), F32)
        for t in range(n):
            sh = (n - 1 - t) * S5_GROUP
            if sh:
                shifted = jnp.concatenate([rt[sh:], jnp.zeros((sh, S5_W), F32)], axis=0)
            else:
                shifted = rt
            toep = jnp.where(colblk == t, shifted, toep)
        toep = toep + jnp.where(row == col, d_ref[gi], 0.0)
        toep_ref[gi] = toep.astype(toep_ref.dtype)
        m1r_ref[gi] = m1r.astype(m1r_ref.dtype)
        m1i_ref[gi] = m1i.astype(m1i_ref.dtype)
        m3r_ref[gi] = m3r.astype(m3r_ref.dtype)
        m3i_ref[gi] = m3i.astype(m3i_ref.dtype)
        a16r_ref[gi] = pr[n]
        a16i_ref[gi] = pi[n]


def _s5_prep(lre, lim, ldt, bre_t, bim_t, cre, cim, drow):
    gp = S5_PREP_GROUPS
    blk = lambda *tail: pl.BlockSpec((gp,) + tail, lambda i: (i,) + (0,) * len(tail))
    p = S5_STATE
    out_shape = (
        jax.ShapeDtypeStruct((S5_GROUPS, S5_W, S5_W), BF16),
        jax.ShapeDtypeStruct((S5_GROUPS, S5_W, p), BF16),
        jax.ShapeDtypeStruct((S5_GROUPS, S5_W, p), BF16),
        jax.ShapeDtypeStruct((S5_GROUPS, S5_W, p), BF16),
        jax.ShapeDtypeStruct((S5_GROUPS, S5_W, p), BF16),
        jax.ShapeDtypeStruct((S5_GROUPS, 1, p), F32),
        jax.ShapeDtypeStruct((S5_GROUPS, 1, p), F32),
    )
    return pl.pallas_call(
        _s5_prep_kernel,
        grid=(S5_GROUPS // gp,),
        in_specs=[blk(1, p), blk(1, p), blk(1, 1), blk(S5_GROUP, p), blk(S5_GROUP, p),
                  blk(S5_GROUP, p), blk(S5_GROUP, p), blk(1, S5_W)],
        out_specs=(blk(S5_W, S5_W), blk(S5_W, p), blk(S5_W, p), blk(S5_W, p), blk(S5_W, p),
                   blk(1, p), blk(1, p)),
        out_shape=out_shape,
        compiler_params=pltpu.CompilerParams(dimension_semantics=("arbitrary",),
                                             vmem_limit_bytes=VMEM_LIMIT),
        name="s5_prep",
    )(lre, lim, ldt, bre_t, bim_t, cre, cim, drow)


S5_GB = 2


def _s5_kernel(u_ref, toep_ref, m1r_ref, m1i_ref, m3r_ref, m3i_ref, a16r_ref, a16i_ref,
               y_ref, zr_ref, zi_ref, xr_ref, xi_ref, pr_ref, pi_ref):
    p = S5_STATE
    segs = S5_SEGS
    rows_b = S5_ROWS // BATCH
    for gi in range(S5_GB):
        u = u_ref[gi]
        y_intra = _dot(u, toep_ref[gi])
        zr_ref[...] = _dot(u, m1r_ref[gi])
        zi_ref[...] = _dot(u, m1i_ref[gi])
        ar = jnp.broadcast_to(a16r_ref[gi], (segs, p))
        ai = jnp.broadcast_to(a16i_ref[gi], (segs, p))

        def scan_body(i, carry):
            pwr, pwi, st = carry
            r0 = pl.multiple_of(i * segs, segs)
            pr_ref[pl.ds(r0, segs), :] = pwr
            pi_ref[pl.ds(r0, segs), :] = pwi
            new_st = []
            for b in range(BATCH):
                sr, si = st[b]
                rb = pl.ds(pl.multiple_of(b * rows_b + i * segs, segs), segs)
                xr_ref[rb, :] = sr
                xi_ref[rb, :] = si
                new_st.append((sr * ar - si * ai + zr_ref[rb, :], sr * ai + si * ar + zi_ref[rb, :]))
            return (pwr * ar - pwi * ai, pwr * ai + pwi * ar, tuple(new_st))

        zero = jnp.zeros((segs, p), F32)
        init = (jnp.ones((segs, p), F32), zero, tuple((zero, zero) for _ in range(BATCH)))
        pwr, pwi, st = lax.fori_loop(0, S5_CPS, scan_body, init)

        sub = lax.broadcasted_iota(jnp.int32, (segs, p), 0)
        for b in range(BATCH):
            fr, fi = st[b]
            gr, gi_ = zero, zero
            for _ in range(segs - 1):
                nr = pwr * gr - pwi * gi_ + fr
                ni = pwr * gi_ + pwi * gr + fi
                gr = jnp.where(sub == 0, 0.0, pltpu.roll(nr, 1, 0))
                gi_ = jnp.where(sub == 0, 0.0, pltpu.roll(ni, 1, 0))
            rb = pl.ds(b * rows_b, rows_b)
            gtr = jnp.concatenate([gr] * S5_CPS, axis=0)
            gti = jnp.concatenate([gi_] * S5_CPS, axis=0)
            pw_r = pr_ref[...]
            pw_i = pi_ref[...]
            xr_ref[rb, :] = xr_ref[rb, :] + pw_r * gtr - pw_i * gti
            xi_ref[rb, :] = xi_ref[rb, :] + pw_r * gti + pw_i * gtr

        y_inter = (_dot_nt(xr_ref[...].astype(BF16), m3r_ref[gi])
                   + _dot_nt(xi_ref[...].astype(BF16), m3i_ref[gi]))
        y_ref[gi] = (y_intra + y_inter).astype(y_ref.dtype)


def _s5(ug, toep, m1r, m1i, m3r, m3i, a16r, a16i):
    gb = S5_GB
    p = S5_STATE
    blk = lambda *tail: pl.BlockSpec((gb,) + tail, lambda i: (i,) + (0,) * len(tail))
    rows_b = S5_ROWS // BATCH
    return pl.pallas_call(
        _s5_kernel,
        grid=(S5_GROUPS // gb,),
        in_specs=[blk(S5_ROWS, S5_W), blk(S5_W, S5_W), blk(S5_W, p), blk(S5_W, p), blk(S5_W, p),
                  blk(S5_W, p), blk(1, p), blk(1, p)],
        out_specs=blk(S5_ROWS, S5_W),
        out_shape=jax.ShapeDtypeStruct((S5_GROUPS, S5_ROWS, S5_W), F32),
        scratch_shapes=[pltpu.VMEM((S5_ROWS, p), F32), pltpu.VMEM((S5_ROWS, p), F32),
                        pltpu.VMEM((S5_ROWS, p), F32), pltpu.VMEM((S5_ROWS, p), F32),
                        pltpu.VMEM((rows_b, p), F32), pltpu.VMEM((rows_b, p), F32)],
        compiler_params=pltpu.CompilerParams(dimension_semantics=("arbitrary",),
                                             vmem_limit_bytes=VMEM_LIMIT),
        name="s5",
    )(ug, toep, m1r, m1i, m3r, m3i, a16r, a16i)


def _gelu_tanh(x):
    return 0.5 * x * (1.0 + jnp.tanh(math.sqrt(2.0 / math.pi) * (x + 0.044715 * (x * x * x))))


def _out0_kernel(x_ref, ys_ref, zb_ref, ya_ref, wglu_ref, wout_ref, npost_ref, o_ref):
    y = _gelu_tanh(ys_ref[...])
    y = y * _sigmoid(_dot(y.astype(BF16), wglu_ref[...]))
    yb = (y * zb_ref[...].astype(F32)).astype(BF16)
    half = GDN_HEADS * GDN_DK
    mix = _dot(ya_ref[...], wout_ref[0:half, :]) + _dot(yb, wout_ref[half:, :])
    o_ref[...] = x_ref[...] + _rms_scale(mix) * npost_ref[...]


def _out0(x2, ys, zb, ya, wglu, wout, npost):
    tok = lambda cols: pl.BlockSpec((TM, cols), lambda i: (i, 0))
    return pl.pallas_call(
        _out0_kernel,
        grid=(TOKENS // TM,),
        in_specs=[tok(D_MODEL), tok(D_MODEL), tok(D_MODEL), tok(D_MODEL), _const_spec(wglu.shape),
                  _const_spec(wout.shape), _const_spec((1, D_MODEL))],
        out_specs=tok(D_MODEL),
        out_shape=jax.ShapeDtypeStruct((TOKENS, D_MODEL), F32),
        compiler_params=pltpu.CompilerParams(dimension_semantics=("arbitrary",),
                                             vmem_limit_bytes=VMEM_LIMIT),
        name="out0",
    )(x2, ys, zb, ya, wglu, wout, npost)


def _layer1_kernel(x_ref, npre_ref, win_ref, convw_ref, wout_ref, npost_ref, o_ref, ext_ref):
    tiles_per_seq = SEQ // TM

    @pl.when(pl.program_id(0) % tiles_per_seq == 0)
    def _():
        ext_ref[0:HALO, :] = jnp.zeros((HALO, SC_WIDTH), F32)

    x = x_ref[...]
    h = (_rms_scale(x) * npre_ref[...]).astype(BF16)
    cw = 512
    acc = jnp.zeros((TM, D_MODEL), F32)
    for c in range(SC_WIDTH // cw):
        cols = slice(c * cw, (c + 1) * cw)
        part = lambda k: _dot(h, win_ref[:, k * SC_WIDTH + c * cw:k * SC_WIDTH + (c + 1) * cw])
        prod = part(1) * part(2)
        ext_ref[HALO:HALO + TM, cols] = prod
        w = convw_ref[:, cols]
        conv = prod * w[2:3]
        for j in range(SC_CONV - 1):
            shift = SC_CONV - 1 - j
            conv = conv + ext_ref[HALO - shift:HALO - shift + TM, cols] * w[j:j + 1]
        y = part(0) * conv * _silu(part(3))
        acc = acc + _dot(y.astype(BF16), wout_ref[cols, :])
    ext_ref[0:HALO, :] = ext_ref[TM:TM + HALO, :]
    o_ref[...] = x + _rms_scale(acc) * npost_ref[...]


def _layer1(x1, npre, win, convw, wout, npost):
    tok = lambda cols: pl.BlockSpec((TM, cols), lambda i: (i, 0))
    return pl.pallas_call(
        _layer1_kernel,
        grid=(TOKENS // TM,),
        in_specs=[tok(D_MODEL), _const_spec((1, D_MODEL)), _const_spec(win.shape),
                  _const_spec(convw.shape), _const_spec(wout.shape), _const_spec((1, D_MODEL))],
        out_specs=tok(D_MODEL),
        out_shape=jax.ShapeDtypeStruct((TOKENS, D_MODEL), F32),
        scratch_shapes=[pltpu.VMEM((TM + HALO, SC_WIDTH), F32)],
        compiler_params=pltpu.CompilerParams(dimension_semantics=("arbitrary",),
                                             vmem_limit_bytes=VMEM_LIMIT),
        name="layer1",
    )(x1, npre, win, convw, wout, npost)


def kernel(x, norm_pre, norm_post, w_in_even, conv_qkv, a_log, dt_bias, gdn_norm_w,
           s5_lam_re, s5_lam_im, s5_b_re, s5_b_im, s5_c_re, s5_c_im, s5_log_dt, s5_d,
           w_glu, w_out_even, w_in_odd, conv_short, w_out_odd):
    x2 = x.reshape(TOKENS, D_MODEL)
    hd = GDN_HEADS * GDN_DK

    w0 = w_in_even[0]
    o_za = GDN_QKV
    o_b = o_za + hd
    o_u = o_b + 2 * GDN_HEADS
    o_zb = o_u + D_MODEL
    wqkv = w0[:, :o_za].astype(BF16)
    wza = w0[:, o_za:o_b].astype(BF16)
    wba = jnp.pad(w0[:, o_b:o_u], ((0, 0), (0, 128 - 2 * GDN_HEADS))).astype(BF16)
    wu = w0[:, o_u:o_zb].astype(BF16)
    wzb = w0[:, o_zb:].astype(BF16)
    pad_g = lambda v: jnp.pad(v, (GDN_HEADS, 128 - 2 * GDN_HEADS)).reshape(1, 128)

    q, k, v, za, bg, u, zb = _proj0(x2, norm_pre[0].reshape(1, D_MODEL), wqkv, wza, wba, wu, wzb,
                                    conv_qkv[0], pad_g(a_log[0]), pad_g(dt_bias[0]))
    ya = _gdn(q, k, v, bg, za, gdn_norm_w[0].reshape(1, GDN_DK))

    p = S5_STATE
    drow = jnp.tile(s5_d[0].reshape(S5_GROUPS, 1, S5_GROUP), (1, 1, S5_CHUNK))
    toep, m1r, m1i, m3r, m3i, a16r, a16i = _s5_prep(
        s5_lam_re[0].reshape(S5_GROUPS, 1, p), s5_lam_im[0].reshape(S5_GROUPS, 1, p),
        s5_log_dt[0].reshape(S5_GROUPS, 1, 1),
        jnp.swapaxes(s5_b_re[0], 1, 2), jnp.swapaxes(s5_b_im[0], 1, 2),
        s5_c_re[0], s5_c_im[0], drow)
    ug = u.reshape(BATCH, S5_SEGS, S5_CPS, S5_CHUNK, S5_GROUPS, S5_GROUP)
    ug = ug.transpose(4, 0, 2, 1, 3, 5).reshape(S5_GROUPS, S5_ROWS, S5_W)
    yg = _s5(ug, toep, m1r, m1i, m3r, m3i, a16r, a16i)
    ys = yg.reshape(S5_GROUPS, BATCH, S5_CPS, S5_SEGS, S5_CHUNK, S5_GROUP)
    ys = ys.transpose(1, 3, 2, 4, 0, 5).reshape(TOKENS, D_MODEL)

    x1 = _out0(x2, ys, zb, ya, w_glu[0].astype(BF16), w_out_even[0].astype(BF16),
               norm_post[0].reshape(1, D_MODEL))

    out = _layer1(x1, norm_pre[1].reshape(1, D_MODEL), w_in_odd[0].astype(BF16), conv_short[0],
                  w_out_odd[0].astype(BF16), norm_post[1].reshape(1, D_MODEL))
    return out.reshape(BATCH, SEQ, D_MODEL)
```

```python
import functools
import math

import jax
import jax.numpy as jnp
from jax import lax
from jax.experimental import pallas as pl
from jax.experimental.pallas import tpu as pltpu

F32 = jnp.float32
BF16 = jnp.bfloat16

D_MODEL = 1024
BATCH = 2
SEQ = 8192
TOKENS = BATCH * SEQ
RMS_EPS = 1e-6

GDN_HEADS = 8
GDN_DK = 128
GDN_CONV = 4
GDN_CHUNK = 64
GDN_QKV = 3 * GDN_HEADS * GDN_DK

S5_GROUP = 16
S5_GROUPS = D_MODEL // S5_GROUP
S5_STATE = 64
S5_MIN_NEG = 1e-4
S5_CHUNK = 16
S5_SEGS = 8
S5_CPS = SEQ // (S5_CHUNK * S5_SEGS)
S5_ROWS = BATCH * SEQ // S5_CHUNK
S5_W = S5_CHUNK * S5_GROUP

SC_WIDTH = 2 * D_MODEL
SC_CONV = 3

HALO = 8
VMEM_LIMIT = 56 * 1024 * 1024

TM = 256


def _dot(a, b):
    return jnp.dot(a, b, preferred_element_type=F32)


def _dot_nt(a, b):
    return lax.dot_general(a, b, (((1,), (1,)), ((), ())), preferred_element_type=F32)


def _dot_tn(a, b):
    return lax.dot_general(a, b, (((0,), (0,)), ((), ())), preferred_element_type=F32)


def _split3(x):
    x1 = x.astype(BF16)
    r = x - x1.astype(F32)
    x2 = r.astype(BF16)
    x3 = (r - x2.astype(F32)).astype(BF16)
    return x1, x2, x3


def _dot3_nt(a, b):
    a1, a2, a3 = _split3(a)
    b1, b2, b3 = _split3(b)
    return (_dot_nt(a1, b1) + _dot_nt(a1, b2) + _dot_nt(a2, b1)
            + _dot_nt(a2, b2) + _dot_nt(a1, b3) + _dot_nt(a3, b1))


def _sigmoid(x):
    return 1.0 / (1.0 + jnp.exp(-x))


def _silu(x):
    return x * _sigmoid(x)


def _rms_scale(x):
    return x * lax.rsqrt(jnp.mean(x * x, axis=-1, keepdims=True) + RMS_EPS)


def _proj0_kernel(x_ref, npre_ref, wqkv_ref, wza_ref, wba_ref, wu_ref, wzb_ref, convw_ref,
                  alog_ref, dtb_ref,
                  q_ref, k_ref, v_ref, za_ref, bg_ref, u_ref, zb_ref, ext_ref):
    tiles_per_seq = SEQ // TM

    @pl.when(pl.program_id(0) % tiles_per_seq == 0)
    def _():
        ext_ref[0:HALO, :] = jnp.zeros((HALO, GDN_QKV), F32)

    h = (_rms_scale(x_ref[...]) * npre_ref[...]).astype(BF16)

    cw = 512
    heads_per_chunk = cw // GDN_DK
    outs = (q_ref, k_ref, v_ref)
    for c in range(GDN_QKV // cw):
        cols = slice(c * cw, (c + 1) * cw)
        raw = _dot(h, wqkv_ref[:, cols])
        ext_ref[HALO:HALO + TM, cols] = raw
        w = convw_ref[:, cols]
        acc = raw * w[3:4]
        for j in range(GDN_CONV - 1):
            shift = GDN_CONV - 1 - j
            acc = acc + ext_ref[HALO - shift:HALO - shift + TM, cols] * w[j:j + 1]
        s = _silu(acc)
        which = (c * cw) // (GDN_HEADS * GDN_DK)
        out_ref = outs[which]
        col0 = c * cw - which * GDN_HEADS * GDN_DK
        for hh in range(heads_per_chunk):
            sh = s[:, hh * GDN_DK:(hh + 1) * GDN_DK]
            if which < 2:
                sh = sh * lax.rsqrt(jnp.sum(sh * sh, axis=-1, keepdims=True) + RMS_EPS)
            if which == 0:
                sh = sh * (GDN_DK ** -0.5)
            out_ref[:, col0 + hh * GDN_DK:col0 + (hh + 1) * GDN_DK] = sh.astype(out_ref.dtype)
    ext_ref[0:HALO, :] = ext_ref[TM:TM + HALO, :]

    za_ref[...] = _silu(_dot(h, wza_ref[...])).astype(za_ref.dtype)
    zb_ref[...] = _silu(_dot(h, wzb_ref[...])).astype(zb_ref.dtype)
    u_ref[...] = _dot(h, wu_ref[...]).astype(u_ref.dtype)

    ba = _dot(h, wba_ref[...])
    beta = _sigmoid(ba)
    xs = ba + dtb_ref[...]
    softplus = jnp.maximum(xs, 0.0) + jnp.log(1.0 + jnp.exp(-jnp.abs(xs)))
    g = -jnp.exp(alog_ref[...]) * softplus
    lane = lax.broadcasted_iota(jnp.int32, ba.shape, 1)
    bg_ref[...] = jnp.where(lane < GDN_HEADS, beta, g)


def _const_spec(shape):
    nd = len(shape)
    return pl.BlockSpec(shape, lambda *_: (0,) * nd)


def _proj0(x2, npre, wqkv, wza, wba, wu, wzb, convw, alog, dtb):
    n_tiles = TOKENS // TM
    tok = lambda cols: pl.BlockSpec((TM, cols), lambda i: (i, 0))
    hd = GDN_HEADS * GDN_DK
    out_shape = (
        jax.ShapeDtypeStruct((TOKENS, hd), BF16),
        jax.ShapeDtypeStruct((TOKENS, hd), BF16),
        jax.ShapeDtypeStruct((TOKENS, hd), BF16),
        jax.ShapeDtypeStruct((TOKENS, hd), BF16),
        jax.ShapeDtypeStruct((TOKENS, 128), F32),
        jax.ShapeDtypeStruct((TOKENS, D_MODEL), BF16),
        jax.ShapeDtypeStruct((TOKENS, D_MODEL), BF16),
    )
    return pl.pallas_call(
        _proj0_kernel,
        grid=(n_tiles,),
        in_specs=[tok(D_MODEL), _const_spec((1, D_MODEL)), _const_spec(wqkv.shape),
                  _const_spec(wza.shape), _const_spec(wba.shape), _const_spec(wu.shape),
                  _const_spec(wzb.shape), _const_spec(convw.shape), _const_spec((1, 128)),
                  _const_spec((1, 128))],
        out_specs=(tok(hd), tok(hd), tok(hd), tok(hd), tok(128), tok(D_MODEL), tok(D_MODEL)),
        out_shape=out_shape,
        scratch_shapes=[pltpu.VMEM((TM + HALO, GDN_QKV), F32)],
        compiler_params=pltpu.CompilerParams(dimension_semantics=("arbitrary",),
                                             vmem_limit_bytes=VMEM_LIMIT),
        name="proj0",
    )(x2, npre, wqkv, wza, wba, wu, wzb, convw, alog, dtb)


GDN_PACK = 4
GDN_NGRP = GDN_HEADS // GDN_PACK
GDN_PW = GDN_PACK * GDN_CHUNK
GDN_GW = GDN_PACK * GDN_DK
GDN_PREP_TILE = 256


def _lane_bcast(x, lane, width):
    return jnp.broadcast_to(x[:, lane:lane + 1], (x.shape[0], width))


def _gdn_prep_kernel(q_ref, k_ref, v_ref, bg_ref,
                     u_ref, w_ref, qd_ref, kd_ref, attn_ref, el_ref):
    c = GDN_CHUNK
    pw = GDN_PW
    n_chunks = GDN_PREP_TILE // c

    row_p = lax.broadcasted_iota(jnp.int32, (c, pw), 0)
    lane_p = lax.broadcasted_iota(jnp.int32, (c, pw), 1)
    col_p = lane_p % c
    causal_p = row_p >= col_p
    strict_p = row_p > col_p
    eye_p = jnp.where(row_p == col_p, 1.0, 0.0).astype(F32)
    r2 = lax.broadcasted_iota(jnp.int32, (pw, pw), 0)
    c2 = lax.broadcasted_iota(jnp.int32, (pw, pw), 1)
    bd_mask = (r2 // c) == (c2 // c)
    r3 = lax.broadcasted_iota(jnp.int32, (pw, GDN_GW), 0)
    c3 = lax.broadcasted_iota(jnp.int32, (pw, GDN_GW), 1)
    kbd_mask = (r3 // c) == (c3 // GDN_DK)
    r4 = lax.broadcasted_iota(jnp.int32, (pw, 128), 0)
    c4 = lax.broadcasted_iota(jnp.int32, (pw, 128), 1)
    r1 = lax.broadcasted_iota(jnp.int32, (c, c), 0)
    c1 = lax.broadcasted_iota(jnp.int32, (c, c), 1)
    tri = jnp.where(r1 >= c1, 1.0, 0.0).astype(BF16)
    lane128 = lax.broadcasted_iota(jnp.int32, (c, 128), 1)
    low_half = lane128 < c
    ones_lhs = jnp.ones((c, 128), BF16)

    def block_diag(x_b):
        return jnp.where(bd_mask, jnp.concatenate([x_b] * GDN_PACK, axis=0), 0)

    chunks = []
    for ci in range(n_chunks):
        rows = slice(ci * c, (ci + 1) * c)
        bg = bg_ref[rows, :]
        g_pad = jnp.where((lane128 >= GDN_HEADS) & (lane128 < 2 * GDN_HEADS), bg, 0.0)
        chunks.append(dict(ci=ci, rows=rows, bg=bg, g3=_split3(g_pad)))
    for ch in chunks:
        g1, g2, g3 = ch["g3"]
        gc = _dot(tri, g1) + _dot(tri, g2) + _dot(tri, g3)
        ch["gc"] = gc
        ch["e_gc"] = jnp.exp(gc)
        ch["e_rev"] = jnp.exp(gc[c - 1:c, :] - gc)

    ctxs = []
    for ch in chunks:
        rows, bg, gc = ch["rows"], ch["bg"], ch["gc"]
        for gh in range(GDN_NGRP):
            heads = range(gh * GDN_PACK, (gh + 1) * GDN_PACK)
            cols = slice(gh * GDN_GW, (gh + 1) * GDN_GW)
            k_b = k_ref[rows, cols]
            q_b = q_ref[rows, cols]
            wide = lambda src, off: jnp.concatenate(
                [_lane_bcast(src, off + hd, GDN_DK) for hd in heads], axis=1)
            beta_f = wide(bg, 0)
            egc_f = wide(ch["e_gc"], GDN_HEADS)
            erev_f = wide(ch["e_rev"], GDN_HEADS)
            kf = k_b.astype(F32)
            kb = kf * beta_f
            kbd = jnp.where(kbd_mask, jnp.concatenate([k_b] * GDN_PACK, axis=0), 0)
            kkqk = _dot_nt(jnp.concatenate([kb.astype(BF16), q_b], axis=0), kbd)

            bc = [_lane_bcast(gc, GDN_HEADS + hd, 128) for hd in heads]
            gcol = jnp.concatenate([jnp.where(low_half, bc[0], bc[1]),
                                    jnp.where(low_half, bc[2], bc[3])], axis=1)
            sel = (c4 == GDN_HEADS + gh * GDN_PACK + r4 // c)
            gsel = jnp.where(sel, jnp.concatenate([gc] * GDN_PACK, axis=0), 0.0)
            s1, s2, s3 = _split3(gsel)
            grow = _dot_nt(ones_lhs, s1) + _dot_nt(ones_lhs, s2) + _dot_nt(ones_lhs, s3)
            decay = jnp.exp(jnp.minimum(gcol - grow, 0.0))
            a = jnp.where(strict_p, kkqk[:c] * decay, 0.0)
            attn = jnp.where(causal_p, kkqk[c:] * decay, 0.0)

            vb = (v_ref[rows, cols].astype(F32) * beta_f).astype(BF16)
            kbe = (kb * egc_f).astype(BF16)
            rhs = jnp.concatenate(
                [jnp.concatenate([vb[:, i * GDN_DK:(i + 1) * GDN_DK],
                                  kbe[:, i * GDN_DK:(i + 1) * GDN_DK]], axis=1)
                 for i in range(GDN_PACK)], axis=0)

            qd_ref[rows, cols] = (q_b.astype(F32) * egc_f).astype(qd_ref.dtype)
            kd_ref[rows, cols] = (kf * erev_f).astype(kd_ref.dtype)
            el_ref[ch["ci"], :, cols] = egc_f[c - 1:c, :]
            halves = [attn[:, 0:128], attn[:, 128:256]]
            for i in range(GDN_PACK):
                part = halves[i // 2]
                if i % 2:
                    part = pltpu.roll(part, c, 1)
                hd = gh * GDN_PACK + i
                attn_ref[rows, hd * GDN_DK:(hd + 1) * GDN_DK] = jnp.where(
                    low_half, part, 0.0).astype(attn_ref.dtype)
            ctxs.append(dict(rows=rows, gh=gh, t=eye_p - a, p=a, rhs=rhs))

    for cx in ctxs:
        p_b = cx["p"].astype(BF16)
        cx["p"] = _dot(p_b, block_diag(p_b))
    for lvl in range(1, 6):
        for cx in ctxs:
            p_b = cx["p"].astype(BF16)
            t_b = cx["t"].astype(BF16)
            if lvl < 5:
                res = _dot(jnp.concatenate([t_b, p_b], axis=0), block_diag(p_b))
                cx["t"] = cx["t"] + res[:c]
                cx["p"] = res[c:]
            else:
                cx["t"] = cx["t"] + _dot(t_b, block_diag(p_b))

    for cx in ctxs:
        sol = _dot(block_diag(cx["t"].astype(BF16)), cx["rhs"])
        for i in range(GDN_PACK):
            hd = cx["gh"] * GDN_PACK + i
            blk = sol[i * c:(i + 1) * c]
            u_ref[cx["rows"], hd * GDN_DK:(hd + 1) * GDN_DK] = blk[:, :GDN_DK].astype(u_ref.dtype)
            w_ref[cx["rows"], hd * GDN_DK:(hd + 1) * GDN_DK] = blk[:, GDN_DK:].astype(w_ref.dtype)


def _gdn_prep(q, k, v, bg):
    hd = GDN_HEADS * GDN_DK
    n_chunks = GDN_PREP_TILE // GDN_CHUNK
    tok = lambda cols: pl.BlockSpec((GDN_PREP_TILE, cols), lambda i: (i, 0))
    wide = jax.ShapeDtypeStruct((TOKENS, hd), BF16)
    return pl.pallas_call(
        _gdn_prep_kernel,
        grid=(TOKENS // GDN_PREP_TILE,),
        in_specs=[tok(hd), tok(hd), tok(hd), tok(128)],
        out_specs=(tok(hd), tok(hd), tok(hd), tok(hd), tok(hd),
                   pl.BlockSpec((n_chunks, 1, hd), lambda i: (i, 0, 0))),
        out_shape=(wide, wide, wide, wide, wide,
                   jax.ShapeDtypeStruct((TOKENS // GDN_CHUNK, 1, hd), F32)),
        compiler_params=pltpu.CompilerParams(dimension_semantics=("arbitrary",),
                                             vmem_limit_bytes=VMEM_LIMIT),
        name="gdn_prep",
    )(q, k, v, bg)


GDN_SCAN_TILE = 256


def _gdn_scan_kernel(u_ref, w_ref, qd_ref, kd_ref, attn_ref, el_ref, za_ref, gnw_ref,
                     ya_ref, state_ref):
    c = GDN_CHUNK

    @pl.when(pl.program_id(0) == 0)
    def _():
        state_ref[...] = jnp.zeros(state_ref.shape, F32)

    gnw = gnw_ref[...]
    zpad = jnp.zeros((c, GDN_DK), BF16)
    chains = [(b, hd) for b in range(BATCH) for hd in range(GDN_HEADS)]

    def chunk_body(ci, carry):
        rows = pl.ds(pl.multiple_of(ci * c, c), c)
        wqs = {}
        for b, hd in chains:
            cols = slice(hd * GDN_DK, (hd + 1) * GDN_DK)
            wq = jnp.concatenate([w_ref[b, rows, cols], qd_ref[b, rows, cols]], axis=0)
            wqs[b, hd] = _dot(wq, state_ref[b, hd].astype(BF16))
        for b, hd in chains:
            cols = slice(hd * GDN_DK, (hd + 1) * GDN_DK)
            v_new = u_ref[b, rows, cols].astype(F32) - wqs[b, hd][:c]
            v_b = v_new.astype(BF16)
            o = wqs[b, hd][c:] + _dot(attn_ref[b, rows, cols], jnp.concatenate([v_b, zpad], axis=0))
            e_last = el_ref[b, ci, :, cols]
            state_ref[b, hd] = state_ref[b, hd] * e_last + _dot_tn(kd_ref[b, rows, cols], v_b)
            y = _rms_scale(o) * gnw * za_ref[b, rows, cols].astype(F32)
            ya_ref[b, rows, cols] = y.astype(ya_ref.dtype)
        return carry

    lax.fori_loop(0, GDN_SCAN_TILE // c, chunk_body, 0)


def _gdn_scan(u, w, qd, kd, attn, el, za, gnw):
    hd = GDN_HEADS * GDN_DK
    n_chunks = GDN_SCAN_TILE // GDN_CHUNK
    as3 = lambda a: a.reshape(BATCH, SEQ, hd)
    tok = pl.BlockSpec((BATCH, GDN_SCAN_TILE, hd), lambda j: (0, j, 0))
    el4 = el.reshape(BATCH, SEQ // GDN_CHUNK, 1, hd)
    out = pl.pallas_call(
        _gdn_scan_kernel,
        grid=(SEQ // GDN_SCAN_TILE,),
        in_specs=[tok, tok, tok, tok, tok,
                  pl.BlockSpec((BATCH, n_chunks, 1, hd), lambda j: (0, j, 0, 0)),
                  tok, _const_spec((1, GDN_DK))],
        out_specs=tok,
        out_shape=jax.ShapeDtypeStruct((BATCH, SEQ, hd), BF16),
        scratch_shapes=[pltpu.VMEM((BATCH, GDN_HEADS, GDN_DK, GDN_DK), F32)],
        compiler_params=pltpu.CompilerParams(dimension_semantics=("arbitrary",),
                                             vmem_limit_bytes=VMEM_LIMIT),
        name="gdn_scan",
    )(as3(u), as3(w), as3(qd), as3(kd), as3(attn), el4, as3(za), gnw)
    return out.reshape(TOKENS, hd)


S5_PREP_GROUPS = 8


def _s5_prep_kernel(lre_ref, lim_ref, ldt_ref, bre_ref, bim_ref, cre_ref, cim_ref, d_ref,
                    toep_ref, m1r_ref, m1i_ref, m3r_ref, m3i_ref, a16r_ref, a16i_ref):
    n = S5_CHUNK
    row = lax.broadcasted_iota(jnp.int32, (S5_W, S5_W), 0)
    col = lax.broadcasted_iota(jnp.int32, (S5_W, S5_W), 1)
    colblk = col // S5_GROUP
    for gi in range(S5_PREP_GROUPS):
        lr = jnp.minimum(lre_ref[gi], -S5_MIN_NEG)
        li = lim_ref[gi]
        dt = jnp.exp(ldt_ref[gi])
        mag = jnp.exp(lr * dt)
        ar = mag * jnp.cos(li * dt)
        ai = mag * jnp.sin(li * dt)
        den = lr * lr + li * li
        nr, ni = ar - 1.0, ai
        fr = (nr * lr + ni * li) / den
        fi = (ni * lr - nr * li) / den
        br, bi = bre_ref[gi], bim_ref[gi]
        bbr = fr * br - fi * bi
        bbi = fr * bi + fi * br
        pr = [jnp.ones_like(ar)]
        pi = [jnp.zeros_like(ar)]
        for _ in range(n):
            pr.append(pr[-1] * ar - pi[-1] * ai)
            pi.append(pr[-2] * ai + pi[-1] * ar)
        m1r = jnp.concatenate([pr[n - 1 - s] * bbr - pi[n - 1 - s] * bbi for s in range(n)], axis=0)
        m1i = jnp.concatenate([pr[n - 1 - s] * bbi + pi[n - 1 - s] * bbr for s in range(n)], axis=0)
        cr, ci = cre_ref[gi], cim_ref[gi]
        m3r = jnp.concatenate([cr * pr[t + 1] - ci * pi[t + 1] for t in range(n)], axis=0)
        m3i = jnp.concatenate([-(cr * pi[t + 1] + ci * pr[t + 1]) for t in range(n)], axis=0)
        cer = jnp.concatenate([cr] * n, axis=0)
        cei = jnp.concatenate([ci] * n, axis=0)
        rt = _dot3_nt(m1r, cer) - _dot3_nt(m1i, cei)
        toep = jnp.zeros((S5_W, S5_W), F32)
        for t in range(n):
            sh = (n - 1 - t) * S5_GROUP
            if sh:
                shifted = jnp.concatenate([rt[sh:], jnp.zeros((sh, S5_W), F32)], axis=0)
            else:
                shifted = rt
            toep = jnp.where(colblk == t, shifted, toep)
        toep = toep + jnp.where(row == col, d_ref[gi], 0.0)
        toep_ref[gi] = toep.astype(toep_ref.dtype)
        m1r_ref[gi] = m1r.astype(m1r_ref.dtype)
        m1i_ref[gi] = m1i.astype(m1i_ref.dtype)
        m3r_ref[gi] = m3r.astype(m3r_ref.dtype)
        m3i_ref[gi] = m3i.astype(m3i_ref.dtype)
        a16r_ref[gi] = pr[n]
        a16i_ref[gi] = pi[n]


def _s5_prep(lre, lim, ldt, bre_t, bim_t, cre, cim, drow):
    gp = S5_PREP_GROUPS
    blk = lambda *tail: pl.BlockSpec((gp,) + tail, lambda i: (i,) + (0,) * len(tail))
    p = S5_STATE
    out_shape = (
        jax.ShapeDtypeStruct((S5_GROUPS, S5_W, S5_W), BF16),
        jax.ShapeDtypeStruct((S5_GROUPS, S5_W, p), BF16),
        jax.ShapeDtypeStruct((S5_GROUPS, S5_W, p), BF16),
        jax.ShapeDtypeStruct((S5_GROUPS, S5_W, p), BF16),
        jax.ShapeDtypeStruct((S5_GROUPS, S5_W, p), BF16),
        jax.ShapeDtypeStruct((S5_GROUPS, 1, p), F32),
        jax.ShapeDtypeStruct((S5_GROUPS, 1, p), F32),
    )
    return pl.pallas_call(
        _s5_prep_kernel,
        grid=(S5_GROUPS // gp,),
        in_specs=[blk(1, p), blk(1, p), blk(1, 1), blk(S5_GROUP, p), blk(S5_GROUP, p),
                  blk(S5_GROUP, p), blk(S5_GROUP, p), blk(1, S5_W)],
        out_specs=(blk(S5_W, S5_W), blk(S5_W, p), blk(S5_W, p), blk(S5_W, p), blk(S5_W, p),
                   blk(1, p), blk(1, p)),
        out_shape=out_shape,
        compiler_params=pltpu.CompilerParams(dimension_semantics=("arbitrary",),
                                             vmem_limit_bytes=VMEM_LIMIT),
        name="s5_prep",
    )(lre, lim, ldt, bre_t, bim_t, cre, cim, drow)


S5_GB = 2


def _s5_kernel(u_ref, toep_ref, m1r_ref, m1i_ref, m3r_ref, m3i_ref, a16r_ref, a16i_ref,
               y_ref, zr_ref, zi_ref, xr_ref, xi_ref, pr_ref, pi_ref):
    p = S5_STATE
    segs = S5_SEGS
    rows_b = S5_ROWS // BATCH
    for gi in range(S5_GB):
        u = u_ref[gi]
        y_intra = _dot(u, toep_ref[gi])
        zr_ref[...] = _dot(u, m1r_ref[gi])
        zi_ref[...] = _dot(u, m1i_ref[gi])
        ar = jnp.broadcast_to(a16r_ref[gi], (segs, p))
        ai = jnp.broadcast_to(a16i_ref[gi], (segs, p))

        def scan_body(i, carry):
            pwr, pwi, st = carry
            r0 = pl.multiple_of(i * segs, segs)
            pr_ref[pl.ds(r0, segs), :] = pwr
            pi_ref[pl.ds(r0, segs), :] = pwi
            new_st = []
            for b in range(BATCH):
                sr, si = st[b]
                rb = pl.ds(pl.multiple_of(b * rows_b + i * segs, segs), segs)
                xr_ref[rb, :] = sr
                xi_ref[rb, :] = si
                new_st.append((sr * ar - si * ai + zr_ref[rb, :], sr * ai + si * ar + zi_ref[rb, :]))
            return (pwr * ar - pwi * ai, pwr * ai + pwi * ar, tuple(new_st))

        zero = jnp.zeros((segs, p), F32)
        init = (jnp.ones((segs, p), F32), zero, tuple((zero, zero) for _ in range(BATCH)))
        pwr, pwi, st = lax.fori_loop(0, S5_CPS, scan_body, init)

        sub = lax.broadcasted_iota(jnp.int32, (segs, p), 0)
        for b in range(BATCH):
            fr, fi = st[b]
            gr, gi_ = zero, zero
            for _ in range(segs - 1):
                nr = pwr * gr - pwi * gi_ + fr
                ni = pwr * gi_ + pwi * gr + fi
                gr = jnp.where(sub == 0, 0.0, pltpu.roll(nr, 1, 0))
                gi_ = jnp.where(sub == 0, 0.0, pltpu.roll(ni, 1, 0))
            rb = pl.ds(b * rows_b, rows_b)
            gtr = jnp.concatenate([gr] * S5_CPS, axis=0)
            gti = jnp.concatenate([gi_] * S5_CPS, axis=0)
            pw_r = pr_ref[...]
            pw_i = pi_ref[...]
            xr_ref[rb, :] = xr_ref[rb, :] + pw_r * gtr - pw_i * gti
            xi_ref[rb, :] = xi_ref[rb, :] + pw_r * gti + pw_i * gtr

        y_inter = (_dot_nt(xr_ref[...].astype(BF16), m3r_ref[gi])
                   + _dot_nt(xi_ref[...].astype(BF16), m3i_ref[gi]))
        y_ref[gi] = (y_intra + y_inter).astype(y_ref.dtype)


def _s5(ug, toep, m1r, m1i, m3r, m3i, a16r, a16i):
    gb = S5_GB
    p = S5_STATE
    blk = lambda *tail: pl.BlockSpec((gb,) + tail, lambda i: (i,) + (0,) * len(tail))
    rows_b = S5_ROWS // BATCH
    return pl.pallas_call(
        _s5_kernel,
        grid=(S5_GROUPS // gb,),
        in_specs=[blk(S5_ROWS, S5_W), blk(S5_W, S5_W), blk(S5_W, p), blk(S5_W, p), blk(S5_W, p),
                  blk(S5_W, p), blk(1, p), blk(1, p)],
        out_specs=blk(S5_ROWS, S5_W),
        out_shape=jax.ShapeDtypeStruct((S5_GROUPS, S5_ROWS, S5_W), F32),
        scratch_shapes=[pltpu.VMEM((S5_ROWS, p), F32), pltpu.VMEM((S5_ROWS, p), F32),
                        pltpu.VMEM((S5_ROWS, p), F32), pltpu.VMEM((S5_ROWS, p), F32),
                        pltpu.VMEM((rows_b, p), F32), pltpu.VMEM((rows_b, p), F32)],
        compiler_params=pltpu.CompilerParams(dimension_semantics=("arbitrary",),
                                             vmem_limit_bytes=VMEM_LIMIT),
        name="s5",
    )(ug, toep, m1r, m1i, m3r, m3i, a16r, a16i)


def _gelu_tanh(x):
    return 0.5 * x * (1.0 + jnp.tanh(math.sqrt(2.0 / math.pi) * (x + 0.044715 * (x * x * x))))


def _out0_kernel(x_ref, ys_ref, zb_ref, ya_ref, wglu_ref, wout_ref, npost_ref, o_ref):
    y = _gelu_tanh(ys_ref[...])
    y = y * _sigmoid(_dot(y.astype(BF16), wglu_ref[...]))
    yb = (y * zb_ref[...].astype(F32)).astype(BF16)
    half = GDN_HEADS * GDN_DK
    mix = _dot(ya_ref[...], wout_ref[0:half, :]) + _dot(yb, wout_ref[half:, :])
    o_ref[...] = x_ref[...] + _rms_scale(mix) * npost_ref[...]


def _out0(x2, ys, zb, ya, wglu, wout, npost):
    tok = lambda cols: pl.BlockSpec((TM, cols), lambda i: (i, 0))
    return pl.pallas_call(
        _out0_kernel,
        grid=(TOKENS // TM,),
        in_specs=[tok(D_MODEL), tok(D_MODEL), tok(D_MODEL), tok(D_MODEL), _const_spec(wglu.shape),
                  _const_spec(wout.shape), _const_spec((1, D_MODEL))],
        out_specs=tok(D_MODEL),
        out_shape=jax.ShapeDtypeStruct((TOKENS, D_MODEL), F32),
        compiler_params=pltpu.CompilerParams(dimension_semantics=("arbitrary",),
                                             vmem_limit_bytes=VMEM_LIMIT),
        name="out0",
    )(x2, ys, zb, ya, wglu, wout, npost)


def _layer1_kernel(x_ref, npre_ref, win_ref, convw_ref, wout_ref, npost_ref, o_ref, ext_ref):
    tiles_per_seq = SEQ // TM

    @pl.when(pl.program_id(0) % tiles_per_seq == 0)
    def _():
        ext_ref[0:HALO, :] = jnp.zeros((HALO, SC_WIDTH), F32)

    x = x_ref[...]
    h = (_rms_scale(x) * npre_ref[...]).astype(BF16)
    cw = 512
    acc = jnp.zeros((TM, D_MODEL), F32)
    for c in range(SC_WIDTH // cw):
        cols = slice(c * cw, (c + 1) * cw)
        part = lambda k: _dot(h, win_ref[:, k * SC_WIDTH + c * cw:k * SC_WIDTH + (c + 1) * cw])
        prod = part(1) * part(2)
        ext_ref[HALO:HALO + TM, cols] = prod
        w = convw_ref[:, cols]
        conv = prod * w[2:3]
        for j in range(SC_CONV - 1):
            shift = SC_CONV - 1 - j
            conv = conv + ext_ref[HALO - shift:HALO - shift + TM, cols] * w[j:j + 1]
        y = part(0) * conv * _silu(part(3))
        acc = acc + _dot(y.astype(BF16), wout_ref[cols, :])
    ext_ref[0:HALO, :] = ext_ref[TM:TM + HALO, :]
    o_ref[...] = x + _rms_scale(acc) * npost_ref[...]


def _layer1(x1, npre, win, convw, wout, npost):
    tok = lambda cols: pl.BlockSpec((TM, cols), lambda i: (i, 0))
    return pl.pallas_call(
        _layer1_kernel,
        grid=(TOKENS // TM,),
        in_specs=[tok(D_MODEL), _const_spec((1, D_MODEL)), _const_spec(win.shape),
                  _const_spec(convw.shape), _const_spec(wout.shape), _const_spec((1, D_MODEL))],
        out_specs=tok(D_MODEL),
        out_shape=jax.ShapeDtypeStruct((TOKENS, D_MODEL), F32),
        scratch_shapes=[pltpu.VMEM((TM + HALO, SC_WIDTH), F32)],
        compiler_params=pltpu.CompilerParams(dimension_semantics=("arbitrary",),
                                             vmem_limit_bytes=VMEM_LIMIT),
        name="layer1",
    )(x1, npre, win, convw, wout, npost)


def kernel(x, norm_pre, norm_post, w_in_even, conv_qkv, a_log, dt_bias, gdn_norm_w,
           s5_lam_re, s5_lam_im, s5_b_re, s5_b_im, s5_c_re, s5_c_im, s5_log_dt, s5_d,
           w_glu, w_out_even, w_in_odd, conv_short, w_out_odd):
    x2 = x.reshape(TOKENS, D_MODEL)
    hd = GDN_HEADS * GDN_DK

    w0 = w_in_even[0]
    o_za = GDN_QKV
    o_b = o_za + hd
    o_u = o_b + 2 * GDN_HEADS
    o_zb = o_u + D_MODEL
    wqkv = w0[:, :o_za].astype(BF16)
    wza = w0[:, o_za:o_b].astype(BF16)
    wba = jnp.pad(w0[:, o_b:o_u], ((0, 0), (0, 128 - 2 * GDN_HEADS))).astype(BF16)
    wu = w0[:, o_u:o_zb].astype(BF16)
    wzb = w0[:, o_zb:].astype(BF16)
    pad_g = lambda v: jnp.pad(v, (GDN_HEADS, 128 - 2 * GDN_HEADS)).reshape(1, 128)

    q, k, v, za, bg, u, zb = _proj0(x2, norm_pre[0].reshape(1, D_MODEL), wqkv, wza, wba, wu, wzb,
                                    conv_qkv[0], pad_g(a_log[0]), pad_g(dt_bias[0]))
    u_c, w_c, q_dec, k_dec, attn, e_last = _gdn_prep(q, k, v, bg)
    ya = _gdn_scan(u_c, w_c, q_dec, k_dec, attn, e_last, za, gdn_norm_w[0].reshape(1, GDN_DK))

    p = S5_STATE
    drow = jnp.tile(s5_d[0].reshape(S5_GROUPS, 1, S5_GROUP), (1, 1, S5_CHUNK))
    toep, m1r, m1i, m3r, m3i, a16r, a16i = _s5_prep(
        s5_lam_re[0].reshape(S5_GROUPS, 1, p), s5_lam_im[0].reshape(S5_GROUPS, 1, p),
        s5_log_dt[0].reshape(S5_GROUPS, 1, 1),
        jnp.swapaxes(s5_b_re[0], 1, 2), jnp.swapaxes(s5_b_im[0], 1, 2),
        s5_c_re[0], s5_c_im[0], drow)
    ug = u.reshape(BATCH, S5_SEGS, S5_CPS, S5_CHUNK, S5_GROUPS, S5_GROUP)
    ug = ug.transpose(4, 0, 2, 1, 3, 5).reshape(S5_GROUPS, S5_ROWS, S5_W)
    yg = _s5(ug, toep, m1r, m1i, m3r, m3i, a16r, a16i)
    ys = yg.reshape(S5_GROUPS, BATCH, S5_CPS, S5_SEGS, S5_CHUNK, S5_GROUP)
    ys = ys.transpose(1, 3, 2, 4, 0, 5).reshape(TOKENS, D_MODEL)

    x1 = _out0(x2, ys, zb, ya, w_glu[0].astype(BF16), w_out_even[0].astype(BF16),
               norm_post[0].reshape(1, D_MODEL))

    out = _layer1(x1, norm_pre[1].reshape(1, D_MODEL), w_in_odd[0].astype(BF16), conv_short[0],
                  w_out_odd[0].astype(BF16), norm_post[1].reshape(1, D_MODEL))
    return out.reshape(BATCH, SEQ, D_MODEL)
```

```python
import functools
import math

import jax
import jax.numpy as jnp
from jax import lax
from jax.experimental import pallas as pl
from jax.experimental.pallas import tpu as pltpu

F32 = jnp.float32
BF16 = jnp.bfloat16

D_MODEL = 1024
BATCH = 2
SEQ = 8192
TOKENS = BATCH * SEQ
RMS_EPS = 1e-6

GDN_HEADS = 8
GDN_DK = 128
GDN_CONV = 4
GDN_CHUNK = 64
GDN_QKV = 3 * GDN_HEADS * GDN_DK

S5_GROUP = 16
S5_GROUPS = D_MODEL // S5_GROUP
S5_STATE = 64
S5_MIN_NEG = 1e-4
S5_CHUNK = 16
S5_SEGS = 8
S5_CPS = SEQ // (S5_CHUNK * S5_SEGS)
S5_ROWS = BATCH * SEQ // S5_CHUNK
S5_W = S5_CHUNK * S5_GROUP

SC_WIDTH = 2 * D_MODEL
SC_CONV = 3

HALO = 8
VMEM_LIMIT = 56 * 1024 * 1024

TM = 256


def _dot(a, b):
    return jnp.dot(a, b, preferred_element_type=F32)


def _dot_nt(a, b):
    return lax.dot_general(a, b, (((1,), (1,)), ((), ())), preferred_element_type=F32)


def _dot_tn(a, b):
    return lax.dot_general(a, b, (((0,), (0,)), ((), ())), preferred_element_type=F32)


def _split3(x):
    x1 = x.astype(BF16)
    r = x - x1.astype(F32)
    x2 = r.astype(BF16)
    x3 = (r - x2.astype(F32)).astype(BF16)
    return x1, x2, x3


def _dot3_nt(a, b):
    a1, a2, a3 = _split3(a)
    b1, b2, b3 = _split3(b)
    return (_dot_nt(a1, b1) + _dot_nt(a1, b2) + _dot_nt(a2, b1)
            + _dot_nt(a2, b2) + _dot_nt(a1, b3) + _dot_nt(a3, b1))


def _sigmoid(x):
    return 1.0 / (1.0 + jnp.exp(-x))


def _silu(x):
    return x * _sigmoid(x)


def _rms_scale(x):
    return x * lax.rsqrt(jnp.mean(x * x, axis=-1, keepdims=True) + RMS_EPS)


S5_LANE_BLOCKS = 128 // S5_GROUP


def _block_transpose(ws):
    n = S5_LANE_BLOCKS
    blk = lax.broadcasted_iota(jnp.int32, ws[0].shape, 1) // S5_GROUP
    step = 1
    while step < n:
        upper = (blk & step) != 0
        new = list(ws)
        for a in range(n):
            if a & step:
                continue
            lo, hi = ws[a], ws[a + step]
            new[a] = jnp.where(upper, pltpu.roll(hi, step * S5_GROUP, 1), lo)
            new[a + step] = jnp.where(upper, hi, pltpu.roll(lo, 128 - step * S5_GROUP, 1))
        ws = new
        step *= 2
    return ws


def _proj0_kernel(x_ref, npre_ref, wqkv_ref, wza_ref, wba_ref, wu_ref, wzb_ref, convw_ref,
                  alog_ref, dtb_ref,
                  q_ref, k_ref, v_ref, za_ref, bg_ref, ug_ref, zb_ref, ext_ref, us_ref):
    tiles_per_seq = SEQ // TM

    @pl.when(pl.program_id(0) % tiles_per_seq == 0)
    def _():
        ext_ref[0:HALO, :] = jnp.zeros((HALO, GDN_QKV), F32)

    h = (_rms_scale(x_ref[...]) * npre_ref[...]).astype(BF16)

    cw = 512
    heads_per_chunk = cw // GDN_DK
    outs = (q_ref, k_ref, v_ref)
    for c in range(GDN_QKV // cw):
        cols = slice(c * cw, (c + 1) * cw)
        raw = _dot(h, wqkv_ref[:, cols])
        ext_ref[HALO:HALO + TM, cols] = raw
        w = convw_ref[:, cols]
        acc = raw * w[3:4]
        for j in range(GDN_CONV - 1):
            shift = GDN_CONV - 1 - j
            acc = acc + ext_ref[HALO - shift:HALO - shift + TM, cols] * w[j:j + 1]
        s = _silu(acc)
        which = (c * cw) // (GDN_HEADS * GDN_DK)
        out_ref = outs[which]
        col0 = c * cw - which * GDN_HEADS * GDN_DK
        for hh in range(heads_per_chunk):
            sh = s[:, hh * GDN_DK:(hh + 1) * GDN_DK]
            if which < 2:
                sh = sh * lax.rsqrt(jnp.sum(sh * sh, axis=-1, keepdims=True) + RMS_EPS)
            if which == 0:
                sh = sh * (GDN_DK ** -0.5)
            out_ref[:, col0 + hh * GDN_DK:col0 + (hh + 1) * GDN_DK] = sh.astype(out_ref.dtype)
    ext_ref[0:HALO, :] = ext_ref[TM:TM + HALO, :]

    za_ref[...] = _silu(_dot(h, wza_ref[...])).astype(za_ref.dtype)
    zb_ref[...] = _silu(_dot(h, wzb_ref[...])).astype(zb_ref.dtype)
    u = _dot(h, wu_ref[...])
    chunks = TM // S5_CHUNK
    for j in range(D_MODEL // 128):
        us_ref[j] = u[:, j * 128:(j + 1) * 128]
    for j in range(D_MODEL // 128):
        for half in range(S5_CHUNK // S5_LANE_BLOCKS):
            ws = [us_ref[j, pl.ds(half * S5_LANE_BLOCKS + a, chunks, stride=S5_CHUNK), :]
                  for a in range(S5_LANE_BLOCKS)]
            ws = _block_transpose(ws)
            for g8 in range(S5_LANE_BLOCKS):
                ug_ref[j * S5_LANE_BLOCKS + g8, :, half * 128:(half + 1) * 128] = (
                    ws[g8].astype(ug_ref.dtype))

    ba = _dot(h, wba_ref[...])
    beta = _sigmoid(ba)
    xs = ba + dtb_ref[...]
    softplus = jnp.maximum(xs, 0.0) + jnp.log(1.0 + jnp.exp(-jnp.abs(xs)))
    g = -jnp.exp(alog_ref[...]) * softplus
    lane = lax.broadcasted_iota(jnp.int32, ba.shape, 1)
    bg_ref[...] = jnp.where(lane < GDN_HEADS, beta, g)


def _const_spec(shape):
    nd = len(shape)
    return pl.BlockSpec(shape, lambda *_: (0,) * nd)


def _proj0(x2, npre, wqkv, wza, wba, wu, wzb, convw, alog, dtb):
    n_tiles = TOKENS // TM
    tok = lambda cols: pl.BlockSpec((TM, cols), lambda i: (i, 0))
    hd = GDN_HEADS * GDN_DK
    out_shape = (
        jax.ShapeDtypeStruct((TOKENS, hd), BF16),
        jax.ShapeDtypeStruct((TOKENS, hd), BF16),
        jax.ShapeDtypeStruct((TOKENS, hd), BF16),
        jax.ShapeDtypeStruct((TOKENS, hd), BF16),
        jax.ShapeDtypeStruct((TOKENS, 128), F32),
        jax.ShapeDtypeStruct((S5_GROUPS, S5_ROWS, S5_W), BF16),
        jax.ShapeDtypeStruct((TOKENS, D_MODEL), BF16),
    )
    ug_spec = pl.BlockSpec((S5_GROUPS, TM // S5_CHUNK, S5_W), lambda i: (0, i, 0))
    return pl.pallas_call(
        _proj0_kernel,
        grid=(n_tiles,),
        in_specs=[tok(D_MODEL), _const_spec((1, D_MODEL)), _const_spec(wqkv.shape),
                  _const_spec(wza.shape), _const_spec(wba.shape), _const_spec(wu.shape),
                  _const_spec(wzb.shape), _const_spec(convw.shape), _const_spec((1, 128)),
                  _const_spec((1, 128))],
        out_specs=(tok(hd), tok(hd), tok(hd), tok(hd), tok(128), ug_spec, tok(D_MODEL)),
        out_shape=out_shape,
        scratch_shapes=[pltpu.VMEM((TM + HALO, GDN_QKV), F32),
                        pltpu.VMEM((D_MODEL // 128, TM, 128), F32)],
        compiler_params=pltpu.CompilerParams(dimension_semantics=("arbitrary",),
                                             vmem_limit_bytes=VMEM_LIMIT),
        name="proj0",
    )(x2, npre, wqkv, wza, wba, wu, wzb, convw, alog, dtb)


GDN_PACK = 4
GDN_NGRP = GDN_HEADS // GDN_PACK
GDN_PW = GDN_PACK * GDN_CHUNK
GDN_GW = GDN_PACK * GDN_DK
GDN_PREP_TILE = 256


def _lane_bcast(x, lane, width):
    return jnp.broadcast_to(x[:, lane:lane + 1], (x.shape[0], width))


def _gdn_prep_kernel(q_ref, k_ref, v_ref, bg_ref,
                     u_ref, w_ref, qd_ref, kd_ref, attn_ref, el_ref):
    c = GDN_CHUNK
    pw = GDN_PW
    n_chunks = GDN_PREP_TILE // c

    row_p = lax.broadcasted_iota(jnp.int32, (c, pw), 0)
    lane_p = lax.broadcasted_iota(jnp.int32, (c, pw), 1)
    col_p = lane_p % c
    causal_p = row_p >= col_p
    strict_p = row_p > col_p
    eye_p = jnp.where(row_p == col_p, 1.0, 0.0).astype(F32)
    r2 = lax.broadcasted_iota(jnp.int32, (pw, pw), 0)
    c2 = lax.broadcasted_iota(jnp.int32, (pw, pw), 1)
    bd_mask = (r2 // c) == (c2 // c)
    r3 = lax.broadcasted_iota(jnp.int32, (pw, GDN_GW), 0)
    c3 = lax.broadcasted_iota(jnp.int32, (pw, GDN_GW), 1)
    kbd_mask = (r3 // c) == (c3 // GDN_DK)
    r4 = lax.broadcasted_iota(jnp.int32, (pw, 128), 0)
    c4 = lax.broadcasted_iota(jnp.int32, (pw, 128), 1)
    r1 = lax.broadcasted_iota(jnp.int32, (c, c), 0)
    c1 = lax.broadcasted_iota(jnp.int32, (c, c), 1)
    tri = jnp.where(r1 >= c1, 1.0, 0.0).astype(BF16)
    lane128 = lax.broadcasted_iota(jnp.int32, (c, 128), 1)
    low_half = lane128 < c
    ones_lhs = jnp.ones((c, 128), BF16)

    def block_diag(x_b):
        return jnp.where(bd_mask, jnp.concatenate([x_b] * GDN_PACK, axis=0), 0)

    chunks = []
    for ci in range(n_chunks):
        rows = slice(ci * c, (ci + 1) * c)
        bg = bg_ref[rows, :]
        g_pad = jnp.where((lane128 >= GDN_HEADS) & (lane128 < 2 * GDN_HEADS), bg, 0.0)
        chunks.append(dict(ci=ci, rows=rows, bg=bg, g3=_split3(g_pad)))
    for ch in chunks:
        g1, g2, g3 = ch["g3"]
        gc = _dot(tri, g1) + _dot(tri, g2) + _dot(tri, g3)
        ch["gc"] = gc
        ch["e_gc"] = jnp.exp(gc)
        ch["e_rev"] = jnp.exp(gc[c - 1:c, :] - gc)

    ctxs = []
    for ch in chunks:
        rows, bg, gc = ch["rows"], ch["bg"], ch["gc"]
        for gh in range(GDN_NGRP):
            heads = range(gh * GDN_PACK, (gh + 1) * GDN_PACK)
            cols = slice(gh * GDN_GW, (gh + 1) * GDN_GW)
            k_b = k_ref[rows, cols]
            q_b = q_ref[rows, cols]
            wide = lambda src, off: jnp.concatenate(
                [_lane_bcast(src, off + hd, GDN_DK) for hd in heads], axis=1)
            beta_f = wide(bg, 0)
            egc_f = wide(ch["e_gc"], GDN_HEADS)
            erev_f = wide(ch["e_rev"], GDN_HEADS)
            kf = k_b.astype(F32)
            kb = kf * beta_f
            kbd = jnp.where(kbd_mask, jnp.concatenate([k_b] * GDN_PACK, axis=0), 0)
            kkqk = _dot_nt(jnp.concatenate([kb.astype(BF16), q_b], axis=0), kbd)

            bc = [_lane_bcast(gc, GDN_HEADS + hd, 128) for hd in heads]
            gcol = jnp.concatenate([jnp.where(low_half, bc[0], bc[1]),
                                    jnp.where(low_half, bc[2], bc[3])], axis=1)
            sel = (c4 == GDN_HEADS + gh * GDN_PACK + r4 // c)
            gsel = jnp.where(sel, jnp.concatenate([gc] * GDN_PACK, axis=0), 0.0)
            s1, s2, s3 = _split3(gsel)
            grow = _dot_nt(ones_lhs, s1) + _dot_nt(ones_lhs, s2) + _dot_nt(ones_lhs, s3)
            decay = jnp.exp(jnp.minimum(gcol - grow, 0.0))
            a = jnp.where(strict_p, kkqk[:c] * decay, 0.0)
            attn = jnp.where(causal_p, kkqk[c:] * decay, 0.0)

            vb = (v_ref[rows, cols].astype(F32) * beta_f).astype(BF16)
            kbe = (kb * egc_f).astype(BF16)
            rhs = jnp.concatenate(
                [jnp.concatenate([vb[:, i * GDN_DK:(i + 1) * GDN_DK],
                                  kbe[:, i * GDN_DK:(i + 1) * GDN_DK]], axis=1)
                 for i in range(GDN_PACK)], axis=0)

            qd_ref[rows, cols] = (q_b.astype(F32) * egc_f).astype(qd_ref.dtype)
            kd_ref[rows, cols] = (kf * erev_f).astype(kd_ref.dtype)
            el_ref[ch["ci"], :, cols] = egc_f[c - 1:c, :]
            halves = [attn[:, 0:128], attn[:, 128:256]]
            for i in range(GDN_PACK):
                part = halves[i // 2]
                if i % 2:
                    part = pltpu.roll(part, c, 1)
                hd = gh * GDN_PACK + i
                attn_ref[rows, hd * GDN_DK:(hd + 1) * GDN_DK] = jnp.where(
                    low_half, part, 0.0).astype(attn_ref.dtype)
            ctxs.append(dict(rows=rows, gh=gh, t=eye_p - a, p=a, rhs=rhs))

    for cx in ctxs:
        p_b = cx["p"].astype(BF16)
        cx["p"] = _dot(p_b, block_diag(p_b))
    for lvl in range(1, 6):
        for cx in ctxs:
            p_b = cx["p"].astype(BF16)
            t_b = cx["t"].astype(BF16)
            if lvl < 5:
                res = _dot(jnp.concatenate([t_b, p_b], axis=0), block_diag(p_b))
                cx["t"] = cx["t"] + res[:c]
                cx["p"] = res[c:]
            else:
                cx["t"] = cx["t"] + _dot(t_b, block_diag(p_b))

    for cx in ctxs:
        sol = _dot(block_diag(cx["t"].astype(BF16)), cx["rhs"])
        for i in range(GDN_PACK):
            hd = cx["gh"] * GDN_PACK + i
            blk = sol[i * c:(i + 1) * c]
            u_ref[cx["rows"], hd * GDN_DK:(hd + 1) * GDN_DK] = blk[:, :GDN_DK].astype(u_ref.dtype)
            w_ref[cx["rows"], hd * GDN_DK:(hd + 1) * GDN_DK] = blk[:, GDN_DK:].astype(w_ref.dtype)


def _gdn_prep(q, k, v, bg):
    hd = GDN_HEADS * GDN_DK
    n_chunks = GDN_PREP_TILE // GDN_CHUNK
    tok = lambda cols: pl.BlockSpec((GDN_PREP_TILE, cols), lambda i: (i, 0))
    wide = jax.ShapeDtypeStruct((TOKENS, hd), BF16)
    return pl.pallas_call(
        _gdn_prep_kernel,
        grid=(TOKENS // GDN_PREP_TILE,),
        in_specs=[tok(hd), tok(hd), tok(hd), tok(128)],
        out_specs=(tok(hd), tok(hd), tok(hd), tok(hd), tok(hd),
                   pl.BlockSpec((n_chunks, 1, hd), lambda i: (i, 0, 0))),
        out_shape=(wide, wide, wide, wide, wide,
                   jax.ShapeDtypeStruct((TOKENS // GDN_CHUNK, 1, hd), F32)),
        compiler_params=pltpu.CompilerParams(dimension_semantics=("arbitrary",),
                                             vmem_limit_bytes=VMEM_LIMIT),
        name="gdn_prep",
    )(q, k, v, bg)


GDN_SCAN_TILE = 256


def _gdn_scan_kernel(u_ref, w_ref, qd_ref, kd_ref, attn_ref, el_ref, za_ref, gnw_ref,
                     ya_ref, state_ref):
    c = GDN_CHUNK

    @pl.when(pl.program_id(0) == 0)
    def _():
        state_ref[...] = jnp.zeros(state_ref.shape, F32)

    gnw = gnw_ref[...]
    zpad = jnp.zeros((c, GDN_DK), BF16)
    chains = [(b, hd) for b in range(BATCH) for hd in range(GDN_HEADS)]

    def chunk_body(ci, carry):
        rows = pl.ds(pl.multiple_of(ci * c, c), c)
        wqs = {}
        for b, hd in chains:
            cols = slice(hd * GDN_DK, (hd + 1) * GDN_DK)
            wq = jnp.concatenate([w_ref[b, rows, cols], qd_ref[b, rows, cols]], axis=0)
            wqs[b, hd] = _dot(wq, state_ref[b, hd].astype(BF16))
        for b, hd in chains:
            cols = slice(hd * GDN_DK, (hd + 1) * GDN_DK)
            v_new = u_ref[b, rows, cols].astype(F32) - wqs[b, hd][:c]
            v_b = v_new.astype(BF16)
            o = wqs[b, hd][c:] + _dot(attn_ref[b, rows, cols], jnp.concatenate([v_b, zpad], axis=0))
            e_last = el_ref[b, ci, :, cols]
            state_ref[b, hd] = state_ref[b, hd] * e_last + _dot_tn(kd_ref[b, rows, cols], v_b)
            y = _rms_scale(o) * gnw * za_ref[b, rows, cols].astype(F32)
            ya_ref[b, rows, cols] = y.astype(ya_ref.dtype)
        return carry

    lax.fori_loop(0, GDN_SCAN_TILE // c, chunk_body, 0)


def _gdn_scan(u, w, qd, kd, attn, el, za, gnw):
    hd = GDN_HEADS * GDN_DK
    n_chunks = GDN_SCAN_TILE // GDN_CHUNK
    as3 = lambda a: a.reshape(BATCH, SEQ, hd)
    tok = pl.BlockSpec((BATCH, GDN_SCAN_TILE, hd), lambda j: (0, j, 0))
    el4 = el.reshape(BATCH, SEQ // GDN_CHUNK, 1, hd)
    out = pl.pallas_call(
        _gdn_scan_kernel,
        grid=(SEQ // GDN_SCAN_TILE,),
        in_specs=[tok, tok, tok, tok, tok,
                  pl.BlockSpec((BATCH, n_chunks, 1, hd), lambda j: (0, j, 0, 0)),
                  tok, _const_spec((1, GDN_DK))],
        out_specs=tok,
        out_shape=jax.ShapeDtypeStruct((BATCH, SEQ, hd), BF16),
        scratch_shapes=[pltpu.VMEM((BATCH, GDN_HEADS, GDN_DK, GDN_DK), F32)],
        compiler_params=pltpu.CompilerParams(dimension_semantics=("arbitrary",),
                                             vmem_limit_bytes=VMEM_LIMIT),
        name="gdn_scan",
    )(as3(u), as3(w), as3(qd), as3(kd), as3(attn), el4, as3(za), gnw)
    return out.reshape(TOKENS, hd)


S5_PREP_GROUPS = 8
S5_PAIR_STATE = 2 * S5_STATE


def _s5_prep_kernel(lre_ref, lim_ref, ldt_ref, bre_ref, bim_ref, cre_ref, cim_ref, d_ref,
                    toep_ref, m1r_ref, m1i_ref, m3r_ref, m3i_ref, a16r_ref, a16i_ref):
    n = S5_CHUNK
    row = lax.broadcasted_iota(jnp.int32, (S5_W, S5_W), 0)
    col = lax.broadcasted_iota(jnp.int32, (S5_W, S5_W), 1)
    colblk = col // S5_GROUP
    for gi in range(S5_PREP_GROUPS):
        lr = jnp.minimum(lre_ref[gi], -S5_MIN_NEG)
        li = lim_ref[gi]
        dt = jnp.exp(ldt_ref[gi])
        mag = jnp.exp(lr * dt)
        ar = mag * jnp.cos(li * dt)
        ai = mag * jnp.sin(li * dt)
        den = lr * lr + li * li
        nr, ni = ar - 1.0, ai
        fr = (nr * lr + ni * li) / den
        fi = (ni * lr - nr * li) / den
        br, bi = bre_ref[gi], bim_ref[gi]
        bbr = fr * br - fi * bi
        bbi = fr * bi + fi * br
        pr = [jnp.ones_like(ar)]
        pi = [jnp.zeros_like(ar)]
        for _ in range(n):
            pr.append(pr[-1] * ar - pi[-1] * ai)
            pi.append(pr[-2] * ai + pi[-1] * ar)
        m1r = jnp.concatenate([pr[n - 1 - s] * bbr - pi[n - 1 - s] * bbi for s in range(n)], axis=0)
        m1i = jnp.concatenate([pr[n - 1 - s] * bbi + pi[n - 1 - s] * bbr for s in range(n)], axis=0)
        cr, ci = cre_ref[gi], cim_ref[gi]
        m3r = jnp.concatenate([cr * pr[t + 1] - ci * pi[t + 1] for t in range(n)], axis=0)
        m3i = jnp.concatenate([-(cr * pi[t + 1] + ci * pr[t + 1]) for t in range(n)], axis=0)
        cer = jnp.concatenate([cr] * n, axis=0)
        cei = jnp.concatenate([ci] * n, axis=0)
        rt = _dot3_nt(m1r, cer) - _dot3_nt(m1i, cei)
        toep = jnp.zeros((S5_W, S5_W), F32)
        for t in range(n):
            sh = (n - 1 - t) * S5_GROUP
            if sh:
                shifted = jnp.concatenate([rt[sh:], jnp.zeros((sh, S5_W), F32)], axis=0)
            else:
                shifted = rt
            toep = jnp.where(colblk == t, shifted, toep)
        toep = toep + jnp.where(row == col, d_ref[gi], 0.0)
        toep_ref[gi] = toep.astype(toep_ref.dtype)
        m1r_ref[gi] = m1r.astype(m1r_ref.dtype)
        m1i_ref[gi] = m1i.astype(m1i_ref.dtype)
        m3r_ref[gi] = m3r.astype(m3r_ref.dtype)
        m3i_ref[gi] = m3i.astype(m3i_ref.dtype)
        own = (lax.broadcasted_iota(jnp.int32, ar.shape, 1) // S5_STATE) == (gi % 2)
        a16r_ref[gi] = jnp.where(own, pr[n], 0.0)
        a16i_ref[gi] = jnp.where(own, pi[n], 0.0)


def _s5_prep(lre, lim, ldt, bre_t, bim_t, cre, cim, drow):
    gp = S5_PREP_GROUPS
    blk = lambda *tail: pl.BlockSpec((gp,) + tail, lambda i: (i,) + (0,) * len(tail))
    p = S5_PAIR_STATE
    out_shape = (
        jax.ShapeDtypeStruct((S5_GROUPS, S5_W, S5_W), BF16),
        jax.ShapeDtypeStruct((S5_GROUPS, S5_W, p), BF16),
        jax.ShapeDtypeStruct((S5_GROUPS, S5_W, p), BF16),
        jax.ShapeDtypeStruct((S5_GROUPS, S5_W, p), BF16),
        jax.ShapeDtypeStruct((S5_GROUPS, S5_W, p), BF16),
        jax.ShapeDtypeStruct((S5_GROUPS, 1, p), F32),
        jax.ShapeDtypeStruct((S5_GROUPS, 1, p), F32),
    )
    return pl.pallas_call(
        _s5_prep_kernel,
        grid=(S5_GROUPS // gp,),
        in_specs=[blk(1, p), blk(1, p), blk(1, 1), blk(S5_GROUP, p), blk(S5_GROUP, p),
                  blk(S5_GROUP, p), blk(S5_GROUP, p), blk(1, S5_W)],
        out_specs=(blk(S5_W, S5_W), blk(S5_W, p), blk(S5_W, p), blk(S5_W, p), blk(S5_W, p),
                   blk(1, p), blk(1, p)),
        out_shape=out_shape,
        compiler_params=pltpu.CompilerParams(dimension_semantics=("arbitrary",),
                                             vmem_limit_bytes=VMEM_LIMIT),
        name="s5_prep",
    )(lre, lim, ldt, bre_t, bim_t, cre, cim, drow)


S5_GB = 2


def _s5_kernel(u_ref, toep_ref, m1r_ref, m1i_ref, m3r_ref, m3i_ref, a16r_ref, a16i_ref,
               y_ref, zr_ref, zi_ref, xr_ref, xi_ref, pr_ref, pi_ref):
    p = S5_PAIR_STATE
    segs = S5_SEGS
    rows_b = S5_ROWS // BATCH
    zr_ref[...] = _dot(u_ref[0], m1r_ref[0]) + _dot(u_ref[1], m1r_ref[1])
    zi_ref[...] = _dot(u_ref[0], m1i_ref[0]) + _dot(u_ref[1], m1i_ref[1])
    ar = jnp.broadcast_to(a16r_ref[0] + a16r_ref[1], (segs, p))
    ai = jnp.broadcast_to(a16i_ref[0] + a16i_ref[1], (segs, p))

    def scan_body(i, carry):
        pwr, pwi, st = carry
        pr_ref[pl.ds(i, segs, stride=S5_CPS), :] = pwr
        pi_ref[pl.ds(i, segs, stride=S5_CPS), :] = pwi
        new_st = []
        for b in range(BATCH):
            sr, si = st[b]
            rb = pl.ds(b * rows_b + i, segs, stride=S5_CPS)
            xr_ref[rb, :] = sr
            xi_ref[rb, :] = si
            new_st.append((sr * ar - si * ai + zr_ref[rb, :], sr * ai + si * ar + zi_ref[rb, :]))
        return (pwr * ar - pwi * ai, pwr * ai + pwi * ar, tuple(new_st))

    zero = jnp.zeros((segs, p), F32)
    init = (jnp.ones((segs, p), F32), zero, tuple((zero, zero) for _ in range(BATCH)))
    pwr, pwi, st = lax.fori_loop(0, S5_CPS, scan_body, init)

    sub = lax.broadcasted_iota(jnp.int32, (segs, p), 0)
    for b in range(BATCH):
        fr, fi = st[b]
        gr, gi = zero, zero
        for _ in range(segs - 1):
            nr = pwr * gr - pwi * gi + fr
            ni = pwr * gi + pwi * gr + fi
            gr = jnp.where(sub == 0, 0.0, pltpu.roll(nr, 1, 0))
            gi = jnp.where(sub == 0, 0.0, pltpu.roll(ni, 1, 0))
        rb = pl.ds(b * rows_b, rows_b)
        rep = lambda g: jnp.concatenate(
            [jnp.broadcast_to(g[s:s + 1], (S5_CPS, p)) for s in range(segs)], axis=0)
        gtr = rep(gr)
        gti = rep(gi)
        pw_r = pr_ref[...]
        pw_i = pi_ref[...]
        xr_ref[rb, :] = xr_ref[rb, :] + pw_r * gtr - pw_i * gti
        xi_ref[rb, :] = xi_ref[rb, :] + pw_r * gti + pw_i * gtr

    xr_b = xr_ref[...].astype(BF16)
    xi_b = xi_ref[...].astype(BF16)
    for g in range(S5_GB):
        y = _dot(u_ref[g], toep_ref[g]) + _dot_nt(xr_b, m3r_ref[g]) + _dot_nt(xi_b, m3i_ref[g])
        y_ref[g] = y.astype(y_ref.dtype)


def _s5(ug, toep, m1r, m1i, m3r, m3i, a16r, a16i):
    gb = S5_GB
    p = S5_PAIR_STATE
    blk = lambda *tail: pl.BlockSpec((gb,) + tail, lambda i: (i,) + (0,) * len(tail))
    rows_b = S5_ROWS // BATCH
    return pl.pallas_call(
        _s5_kernel,
        grid=(S5_GROUPS // gb,),
        in_specs=[blk(S5_ROWS, S5_W), blk(S5_W, S5_W), blk(S5_W, p), blk(S5_W, p), blk(S5_W, p),
                  blk(S5_W, p), blk(1, p), blk(1, p)],
        out_specs=blk(S5_ROWS, S5_W),
        out_shape=jax.ShapeDtypeStruct((S5_GROUPS, S5_ROWS, S5_W), BF16),
        scratch_shapes=[pltpu.VMEM((S5_ROWS, p), F32), pltpu.VMEM((S5_ROWS, p), F32),
                        pltpu.VMEM((S5_ROWS, p), F32), pltpu.VMEM((S5_ROWS, p), F32),
                        pltpu.VMEM((rows_b, p), F32), pltpu.VMEM((rows_b, p), F32)],
        compiler_params=pltpu.CompilerParams(dimension_semantics=("arbitrary",),
                                             vmem_limit_bytes=VMEM_LIMIT),
        name="s5",
    )(ug, toep, m1r, m1i, m3r, m3i, a16r, a16i)


def _gelu_tanh(x):
    return 0.5 * x * (1.0 + jnp.tanh(math.sqrt(2.0 / math.pi) * (x + 0.044715 * (x * x * x))))


def _out0_kernel(x_ref, yg_ref, zb_ref, ya_ref, wglu_ref, wout_ref, npost_ref, o_ref, ys_ref):
    chunks = TM // S5_CHUNK
    for j in range(D_MODEL // 128):
        for half in range(S5_CHUNK // S5_LANE_BLOCKS):
            ws = [yg_ref[j * S5_LANE_BLOCKS + g8, :, half * 128:(half + 1) * 128].astype(F32)
                  for g8 in range(S5_LANE_BLOCKS)]
            ws = _block_transpose(ws)
            for a in range(S5_LANE_BLOCKS):
                ys_ref[j, pl.ds(half * S5_LANE_BLOCKS + a, chunks, stride=S5_CHUNK), :] = ws[a]
    y = _gelu_tanh(jnp.concatenate([ys_ref[j] for j in range(D_MODEL // 128)], axis=1))
    y = y * _sigmoid(_dot(y.astype(BF16), wglu_ref[...]))
    yb = (y * zb_ref[...].astype(F32)).astype(BF16)
    half = GDN_HEADS * GDN_DK
    mix = _dot(ya_ref[...], wout_ref[0:half, :]) + _dot(yb, wout_ref[half:, :])
    o_ref[...] = x_ref[...] + _rms_scale(mix) * npost_ref[...]


def _out0(x2, yg, zb, ya, wglu, wout, npost):
    tok = lambda cols: pl.BlockSpec((TM, cols), lambda i: (i, 0))
    yg_spec = pl.BlockSpec((S5_GROUPS, TM // S5_CHUNK, S5_W), lambda i: (0, i, 0))
    return pl.pallas_call(
        _out0_kernel,
        grid=(TOKENS // TM,),
        in_specs=[tok(D_MODEL), yg_spec, tok(D_MODEL), tok(D_MODEL), _const_spec(wglu.shape),
                  _const_spec(wout.shape), _const_spec((1, D_MODEL))],
        out_specs=tok(D_MODEL),
        out_shape=jax.ShapeDtypeStruct((TOKENS, D_MODEL), F32),
        scratch_shapes=[pltpu.VMEM((D_MODEL // 128, TM, 128), F32)],
        compiler_params=pltpu.CompilerParams(dimension_semantics=("arbitrary",),
                                             vmem_limit_bytes=VMEM_LIMIT),
        name="out0",
    )(x2, yg, zb, ya, wglu, wout, npost)


def _layer1_kernel(x_ref, npre_ref, win_ref, convw_ref, wout_ref, npost_ref, o_ref, ext_ref):
    tiles_per_seq = SEQ // TM

    @pl.when(pl.program_id(0) % tiles_per_seq == 0)
    def _():
        ext_ref[0:HALO, :] = jnp.zeros((HALO, SC_WIDTH), F32)

    x = x_ref[...]
    h = (_rms_scale(x) * npre_ref[...]).astype(BF16)
    cw = 512
    acc = jnp.zeros((TM, D_MODEL), F32)
    for c in range(SC_WIDTH // cw):
        cols = slice(c * cw, (c + 1) * cw)
        part = lambda k: _dot(h, win_ref[:, k * SC_WIDTH + c * cw:k * SC_WIDTH + (c + 1) * cw])
        prod = part(1) * part(2)
        ext_ref[HALO:HALO + TM, cols] = prod
        w = convw_ref[:, cols]
        conv = prod * w[2:3]
        for j in range(SC_CONV - 1):
            shift = SC_CONV - 1 - j
            conv = conv + ext_ref[HALO - shift:HALO - shift + TM, cols] * w[j:j + 1]
        y = part(0) * conv * _silu(part(3))
        acc = acc + _dot(y.astype(BF16), wout_ref[cols, :])
    ext_ref[0:HALO, :] = ext_ref[TM:TM + HALO, :]
    o_ref[...] = x + _rms_scale(acc) * npost_ref[...]


def _layer1(x1, npre, win, convw, wout, npost):
    tok = lambda cols: pl.BlockSpec((TM, cols), lambda i: (i, 0))
    return pl.pallas_call(
        _layer1_kernel,
        grid=(TOKENS // TM,),
        in_specs=[tok(D_MODEL), _const_spec((1, D_MODEL)), _const_spec(win.shape),
                  _const_spec(convw.shape), _const_spec(wout.shape), _const_spec((1, D_MODEL))],
        out_specs=tok(D_MODEL),
        out_shape=jax.ShapeDtypeStruct((TOKENS, D_MODEL), F32),
        scratch_shapes=[pltpu.VMEM((TM + HALO, SC_WIDTH), F32)],
        compiler_params=pltpu.CompilerParams(dimension_semantics=("arbitrary",),
                                             vmem_limit_bytes=VMEM_LIMIT),
        name="layer1",
    )(x1, npre, win, convw, wout, npost)


def kernel(x, norm_pre, norm_post, w_in_even, conv_qkv, a_log, dt_bias, gdn_norm_w,
           s5_lam_re, s5_lam_im, s5_b_re, s5_b_im, s5_c_re, s5_c_im, s5_log_dt, s5_d,
           w_glu, w_out_even, w_in_odd, conv_short, w_out_odd):
    x2 = x.reshape(TOKENS, D_MODEL)
    hd = GDN_HEADS * GDN_DK

    w0 = w_in_even[0]
    o_za = GDN_QKV
    o_b = o_za + hd
    o_u = o_b + 2 * GDN_HEADS
    o_zb = o_u + D_MODEL
    wqkv = w0[:, :o_za].astype(BF16)
    wza = w0[:, o_za:o_b].astype(BF16)
    wba = jnp.pad(w0[:, o_b:o_u], ((0, 0), (0, 128 - 2 * GDN_HEADS))).astype(BF16)
    wu = w0[:, o_u:o_zb].astype(BF16)
    wzb = w0[:, o_zb:].astype(BF16)
    pad_g = lambda v: jnp.pad(v, (GDN_HEADS, 128 - 2 * GDN_HEADS)).reshape(1, 128)

    q, k, v, za, bg, ug, zb = _proj0(x2, norm_pre[0].reshape(1, D_MODEL), wqkv, wza, wba, wu, wzb,
                                    conv_qkv[0], pad_g(a_log[0]), pad_g(dt_bias[0]))
    u_c, w_c, q_dec, k_dec, attn, e_last = _gdn_prep(q, k, v, bg)
    ya = _gdn_scan(u_c, w_c, q_dec, k_dec, attn, e_last, za, gdn_norm_w[0].reshape(1, GDN_DK))

    p = S5_STATE
    drow = jnp.tile(s5_d[0].reshape(S5_GROUPS, 1, S5_GROUP), (1, 1, S5_CHUNK))
    odd = (jnp.arange(S5_GROUPS) % 2 == 1).reshape(S5_GROUPS, 1, 1)

    def by_parity(a):
        z = jnp.zeros_like(a)
        return jnp.concatenate([jnp.where(odd, z, a), jnp.where(odd, a, z)], axis=-1)

    toep, m1r, m1i, m3r, m3i, a16r, a16i = _s5_prep(
        by_parity(s5_lam_re[0].reshape(S5_GROUPS, 1, p)),
        by_parity(s5_lam_im[0].reshape(S5_GROUPS, 1, p)),
        s5_log_dt[0].reshape(S5_GROUPS, 1, 1),
        by_parity(jnp.swapaxes(s5_b_re[0], 1, 2)), by_parity(jnp.swapaxes(s5_b_im[0], 1, 2)),
        by_parity(s5_c_re[0]), by_parity(s5_c_im[0]), drow)
    yg = _s5(ug, toep, m1r, m1i, m3r, m3i, a16r, a16i)

    x1 = _out0(x2, yg, zb, ya, w_glu[0].astype(BF16), w_out_even[0].astype(BF16),
               norm_post[0].reshape(1, D_MODEL))

    out = _layer1(x1, norm_pre[1].reshape(1, D_MODEL), w_in_odd[0].astype(BF16), conv_short[0],
                  w_out_odd[0].astype(BF16), norm_post[1].reshape(1, D_MODEL))
    return out.reshape(BATCH, SEQ, D_MODEL)
```

```python
import functools
import math

import jax
import jax.numpy as jnp
from jax import lax
from jax.experimental import pallas as pl
from jax.experimental.pallas import tpu as pltpu

F32 = jnp.float32
BF16 = jnp.bfloat16

D_MODEL = 1024
BATCH = 2
SEQ = 8192
TOKENS = BATCH * SEQ
RMS_EPS = 1e-6

GDN_HEADS = 8
GDN_DK = 128
GDN_CONV = 4
GDN_CHUNK = 64
GDN_QKV = 3 * GDN_HEADS * GDN_DK

S5_GROUP = 16
S5_GROUPS = D_MODEL // S5_GROUP
S5_STATE = 64
S5_MIN_NEG = 1e-4
S5_CHUNK = 16
S5_SEGS = 8
S5_CPS = SEQ // (S5_CHUNK * S5_SEGS)
S5_ROWS = BATCH * SEQ // S5_CHUNK
S5_W = S5_CHUNK * S5_GROUP

SC_WIDTH = 2 * D_MODEL
SC_CONV = 3

HALO = 8
CONV_DEC = 4
VMEM_LIMIT = 56 * 1024 * 1024

TM = 512


def _dot(a, b):
    return jnp.dot(a, b, preferred_element_type=F32)


def _dot_nt(a, b):
    return lax.dot_general(a, b, (((1,), (1,)), ((), ())), preferred_element_type=F32)


def _dot_tn(a, b):
    return lax.dot_general(a, b, (((0,), (0,)), ((), ())), preferred_element_type=F32)


def _split3(x):
    x1 = x.astype(BF16)
    r = x - x1.astype(F32)
    x2 = r.astype(BF16)
    x3 = (r - x2.astype(F32)).astype(BF16)
    return x1, x2, x3


def _dot3_nt(a, b):
    a1, a2, a3 = _split3(a)
    b1, b2, b3 = _split3(b)
    return (_dot_nt(a1, b1) + _dot_nt(a1, b2) + _dot_nt(a2, b1)
            + _dot_nt(a2, b2) + _dot_nt(a1, b3) + _dot_nt(a3, b1))


def _sigmoid(x):
    return 1.0 / (1.0 + jnp.exp(-x))


def _silu(x):
    return x * _sigmoid(x)


def _rms_scale(x):
    return x * lax.rsqrt(jnp.mean(x * x, axis=-1, keepdims=True) + RMS_EPS)


S5_LANE_BLOCKS = 128 // S5_GROUP


def _block_transpose(ws):
    n = S5_LANE_BLOCKS
    blk = lax.broadcasted_iota(jnp.int32, ws[0].shape, 1) // S5_GROUP
    step = 1
    while step < n:
        upper = (blk & step) != 0
        new = list(ws)
        for a in range(n):
            if a & step:
                continue
            lo, hi = ws[a], ws[a + step]
            new[a] = jnp.where(upper, pltpu.roll(hi, step * S5_GROUP, 1), lo)
            new[a + step] = jnp.where(upper, hi, pltpu.roll(lo, 128 - step * S5_GROUP, 1))
        ws = new
        step *= 2
    return ws


def _proj0_kernel(x_ref, npre_ref, wqkv_ref, wza_ref, wba_ref, wu_ref, wzb_ref, convw_ref,
                  alog_ref, dtb_ref,
                  q_ref, k_ref, v_ref, za_ref, bg_ref, ug_ref, zb_ref, ext_ref, mix_ref, us_ref):
    tiles_per_seq = SEQ // TM

    @pl.when(pl.program_id(0) % tiles_per_seq == 0)
    def _():
        ext_ref[:, 0:HALO, :] = jnp.zeros((GDN_QKV // 128, HALO, 128), F32)

    h = (_rms_scale(x_ref[...]) * npre_ref[...]).astype(BF16)

    cw = 512
    slabs_per_chunk = cw // 128
    outs = (q_ref, k_ref, v_ref)
    for c in range(GDN_QKV // cw):
        raw = _dot(h, wqkv_ref[:, c * cw:(c + 1) * cw])
        for sl in range(slabs_per_chunk):
            ext_ref[c * slabs_per_chunk + sl, HALO:HALO + TM, :] = raw[:, sl * 128:(sl + 1) * 128]
        which = (c * cw) // (GDN_HEADS * GDN_DK)
        out_ref = outs[which]
        for sl in range(slabs_per_chunk):
            slab = c * slabs_per_chunk + sl
            w = convw_ref[:, slab * 128:(slab + 1) * 128]
            for r in range(CONV_DEC):
                acc = None
                for j in range(GDN_CONV):
                    start = HALO + r - (GDN_CONV - 1) + j
                    term = ext_ref[slab, pl.ds(start, TM // CONV_DEC, stride=CONV_DEC), :] * w[j:j + 1]
                    acc = term if acc is None else acc + term
                sh = _silu(acc)
                if which < 2:
                    scale = lax.rsqrt(jnp.sum(sh * sh, axis=-1, keepdims=True) + RMS_EPS)
                    if which == 0:
                        scale = scale * (GDN_DK ** -0.5)
                    sh = sh * scale
                mix_ref[sl, pl.ds(r, TM // CONV_DEC, stride=CONV_DEC), :] = sh
            col0 = slab * 128 - which * GDN_HEADS * GDN_DK
            out_ref[:, col0:col0 + 128] = mix_ref[sl].astype(out_ref.dtype)
    ext_ref[:, 0:HALO, :] = ext_ref[:, TM:TM + HALO, :]

    za_ref[...] = _silu(_dot(h, wza_ref[...])).astype(za_ref.dtype)
    zb_ref[...] = _silu(_dot(h, wzb_ref[...])).astype(zb_ref.dtype)
    u = _dot(h, wu_ref[...])
    chunks = TM // S5_CHUNK
    for j in range(D_MODEL // 128):
        us_ref[j] = u[:, j * 128:(j + 1) * 128]
    for j in range(D_MODEL // 128):
        for half in range(S5_CHUNK // S5_LANE_BLOCKS):
            ws = [us_ref[j, pl.ds(half * S5_LANE_BLOCKS + a, chunks, stride=S5_CHUNK), :]
                  for a in range(S5_LANE_BLOCKS)]
            ws = _block_transpose(ws)
            for g8 in range(S5_LANE_BLOCKS):
                ug_ref[j * S5_LANE_BLOCKS + g8, :, half * 128:(half + 1) * 128] = (
                    ws[g8].astype(ug_ref.dtype))

    ba = _dot(h, wba_ref[...])
    beta = _sigmoid(ba)
    xs = ba + dtb_ref[...]
    softplus = jnp.maximum(xs, 0.0) + jnp.log(1.0 + jnp.exp(-jnp.abs(xs)))
    g = -jnp.exp(alog_ref[...]) * softplus
    lane = lax.broadcasted_iota(jnp.int32, ba.shape, 1)
    bg_ref[...] = jnp.where(lane < GDN_HEADS, beta, g)


def _const_spec(shape):
    nd = len(shape)
    return pl.BlockSpec(shape, lambda *_: (0,) * nd)


def _proj0(x2, npre, wqkv, wza, wba, wu, wzb, convw, alog, dtb):
    n_tiles = TOKENS // TM
    tok = lambda cols: pl.BlockSpec((TM, cols), lambda i: (i, 0))
    hd = GDN_HEADS * GDN_DK
    out_shape = (
        jax.ShapeDtypeStruct((TOKENS, hd), BF16),
        jax.ShapeDtypeStruct((TOKENS, hd), BF16),
        jax.ShapeDtypeStruct((TOKENS, hd), BF16),
        jax.ShapeDtypeStruct((TOKENS, hd), BF16),
        jax.ShapeDtypeStruct((TOKENS, 128), F32),
        jax.ShapeDtypeStruct((S5_GROUPS, S5_ROWS, S5_W), BF16),
        jax.ShapeDtypeStruct((TOKENS, D_MODEL), BF16),
    )
    ug_spec = pl.BlockSpec((S5_GROUPS, TM // S5_CHUNK, S5_W), lambda i: (0, i, 0))
    return pl.pallas_call(
        _proj0_kernel,
        grid=(n_tiles,),
        in_specs=[tok(D_MODEL), _const_spec((1, D_MODEL)), _const_spec(wqkv.shape),
                  _const_spec(wza.shape), _const_spec(wba.shape), _const_spec(wu.shape),
                  _const_spec(wzb.shape), _const_spec(convw.shape), _const_spec((1, 128)),
                  _const_spec((1, 128))],
        out_specs=(tok(hd), tok(hd), tok(hd), tok(hd), tok(128), ug_spec, tok(D_MODEL)),
        out_shape=out_shape,
        scratch_shapes=[pltpu.VMEM((GDN_QKV // 128, TM + HALO, 128), F32),
                        pltpu.VMEM((4, TM, 128), F32),
                        pltpu.VMEM((D_MODEL // 128, TM, 128), F32)],
        compiler_params=pltpu.CompilerParams(dimension_semantics=("arbitrary",),
                                             vmem_limit_bytes=VMEM_LIMIT),
        name="proj0",
    )(x2, npre, wqkv, wza, wba, wu, wzb, convw, alog, dtb)


GDN_PACK = 4
GDN_NGRP = GDN_HEADS // GDN_PACK
GDN_PW = GDN_PACK * GDN_CHUNK
GDN_GW = GDN_PACK * GDN_DK
GDN_PREP_TILE = 256


def _lane_bcast(x, lane, width):
    return jnp.broadcast_to(x[:, lane:lane + 1], (x.shape[0], width))


def _gdn_prep_kernel(q_ref, k_ref, v_ref, bg_ref,
                     u_ref, w_ref, qd_ref, kd_ref, attn_ref, el_ref):
    c = GDN_CHUNK
    pw = GDN_PW
    n_chunks = GDN_PREP_TILE // c

    row_p = lax.broadcasted_iota(jnp.int32, (c, pw), 0)
    lane_p = lax.broadcasted_iota(jnp.int32, (c, pw), 1)
    col_p = lane_p % c
    causal_p = row_p >= col_p
    strict_p = row_p > col_p
    eye_p = jnp.where(row_p == col_p, 1.0, 0.0).astype(F32)
    r2 = lax.broadcasted_iota(jnp.int32, (pw, pw), 0)
    c2 = lax.broadcasted_iota(jnp.int32, (pw, pw), 1)
    bd_mask = (r2 // c) == (c2 // c)
    r3 = lax.broadcasted_iota(jnp.int32, (pw, GDN_GW), 0)
    c3 = lax.broadcasted_iota(jnp.int32, (pw, GDN_GW), 1)
    kbd_mask = (r3 // c) == (c3 // GDN_DK)
    r1 = lax.broadcasted_iota(jnp.int32, (c, c), 0)
    c1 = lax.broadcasted_iota(jnp.int32, (c, c), 1)
    tri = jnp.where(r1 >= c1, 1.0, 0.0).astype(BF16)
    lane128 = lax.broadcasted_iota(jnp.int32, (c, 128), 1)
    low_half = lane128 < c

    def block_diag(x_b):
        return jnp.where(bd_mask, jnp.concatenate([x_b] * GDN_PACK, axis=0), 0)

    chunks = []
    for ci in range(n_chunks):
        rows = slice(ci * c, (ci + 1) * c)
        bg = bg_ref[rows, :]
        g_pad = jnp.where((lane128 >= GDN_HEADS) & (lane128 < 2 * GDN_HEADS), bg, 0.0)
        chunks.append(dict(ci=ci, rows=rows, bg=bg, g3=_split3(g_pad)))
    for ch in chunks:
        g1, g2, g3 = ch["g3"]
        gc = _dot(tri, g1) + _dot(tri, g2) + _dot(tri, g3)
        ch["gc"] = gc
        ch["gc_t"] = gc.T
        ch["e_gc"] = jnp.exp(gc)
        ch["e_rev"] = jnp.exp(gc[c - 1:c, :] - gc)

    ctxs = []
    for ch in chunks:
        rows, bg, gc = ch["rows"], ch["bg"], ch["gc"]
        for gh in range(GDN_NGRP):
            heads = range(gh * GDN_PACK, (gh + 1) * GDN_PACK)
            cols = slice(gh * GDN_GW, (gh + 1) * GDN_GW)
            k_b = k_ref[rows, cols]
            q_b = q_ref[rows, cols]
            wide = lambda src, off: jnp.concatenate(
                [_lane_bcast(src, off + hd, GDN_DK) for hd in heads], axis=1)
            beta_f = wide(bg, 0)
            egc_f = wide(ch["e_gc"], GDN_HEADS)
            erev_f = wide(ch["e_rev"], GDN_HEADS)
            kf = k_b.astype(F32)
            kb = kf * beta_f
            kbd = jnp.where(kbd_mask, jnp.concatenate([k_b] * GDN_PACK, axis=0), 0)
            kkqk = _dot_nt(jnp.concatenate([kb.astype(BF16), q_b], axis=0), kbd)

            bc = [_lane_bcast(gc, GDN_HEADS + hd, 128) for hd in heads]
            gcol = jnp.concatenate([jnp.where(low_half, bc[0], bc[1]),
                                    jnp.where(low_half, bc[2], bc[3])], axis=1)
            gt = ch["gc_t"]
            rows_t = [gt[GDN_HEADS + hd:GDN_HEADS + hd + 1, :] for hd in heads]
            grow = jnp.concatenate([jnp.concatenate(rows_t[0:2], axis=1),
                                    jnp.concatenate(rows_t[2:4], axis=1)], axis=1)
            decay = jnp.exp(jnp.minimum(gcol - grow, 0.0))
            a = jnp.where(strict_p, kkqk[:c] * decay, 0.0)
            attn = jnp.where(causal_p, kkqk[c:] * decay, 0.0)

            vb = (v_ref[rows, cols].astype(F32) * beta_f).astype(BF16)
            kbe = (kb * egc_f).astype(BF16)
            rhs = jnp.concatenate(
                [jnp.concatenate([vb[:, i * GDN_DK:(i + 1) * GDN_DK],
                                  kbe[:, i * GDN_DK:(i + 1) * GDN_DK]], axis=1)
                 for i in range(GDN_PACK)], axis=0)

            qd_ref[rows, cols] = (q_b.astype(F32) * egc_f).astype(qd_ref.dtype)
            kd_ref[rows, cols] = (kf * erev_f).astype(kd_ref.dtype)
            el_ref[ch["ci"], :, cols] = egc_f[c - 1:c, :]
            halves = [attn[:, 0:128], attn[:, 128:256]]
            for i in range(GDN_PACK):
                part = halves[i // 2]
                if i % 2:
                    part = pltpu.roll(part, c, 1)
                hd = gh * GDN_PACK + i
                attn_ref[rows, hd * GDN_DK:(hd + 1) * GDN_DK] = jnp.where(
                    low_half, part, 0.0).astype(attn_ref.dtype)
            ctxs.append(dict(rows=rows, gh=gh, t=eye_p - a, p=a, rhs=rhs))

    for cx in ctxs:
        p_b = cx["p"].astype(BF16)
        cx["p"] = _dot(p_b, block_diag(p_b))
    for lvl in range(1, 6):
        for cx in ctxs:
            p_b = cx["p"].astype(BF16)
            t_b = cx["t"].astype(BF16)
            if lvl < 5:
                res = _dot(jnp.concatenate([t_b, p_b], axis=0), block_diag(p_b))
                cx["t"] = cx["t"] + res[:c]
                cx["p"] = res[c:]
            else:
                cx["t"] = cx["t"] + _dot(t_b, block_diag(p_b))

    for cx in ctxs:
        sol = _dot(block_diag(cx["t"].astype(BF16)), cx["rhs"])
        for i in range(GDN_PACK):
            hd = cx["gh"] * GDN_PACK + i
            blk = sol[i * c:(i + 1) * c]
            u_ref[cx["rows"], hd * GDN_DK:(hd + 1) * GDN_DK] = blk[:, :GDN_DK].astype(u_ref.dtype)
            w_ref[cx["rows"], hd * GDN_DK:(hd + 1) * GDN_DK] = blk[:, GDN_DK:].astype(w_ref.dtype)


def _gdn_prep(q, k, v, bg):
    hd = GDN_HEADS * GDN_DK
    n_chunks = GDN_PREP_TILE // GDN_CHUNK
    tok = lambda cols: pl.BlockSpec((GDN_PREP_TILE, cols), lambda i: (i, 0))
    wide = jax.ShapeDtypeStruct((TOKENS, hd), BF16)
    return pl.pallas_call(
        _gdn_prep_kernel,
        grid=(TOKENS // GDN_PREP_TILE,),
        in_specs=[tok(hd), tok(hd), tok(hd), tok(128)],
        out_specs=(tok(hd), tok(hd), tok(hd), tok(hd), tok(hd),
                   pl.BlockSpec((n_chunks, 1, hd), lambda i: (i, 0, 0))),
        out_shape=(wide, wide, wide, wide, wide,
                   jax.ShapeDtypeStruct((TOKENS // GDN_CHUNK, 1, hd), F32)),
        compiler_params=pltpu.CompilerParams(dimension_semantics=("arbitrary",),
                                             vmem_limit_bytes=VMEM_LIMIT),
        name="gdn_prep",
    )(q, k, v, bg)


GDN_SCAN_TILE = 256


def _gdn_scan_kernel(u_ref, w_ref, qd_ref, kd_ref, attn_ref, el_ref, za_ref, gnw_ref,
                     ya_ref, state_ref):
    c = GDN_CHUNK

    @pl.when(pl.program_id(0) == 0)
    def _():
        state_ref[...] = jnp.zeros(state_ref.shape, F32)

    gnw = gnw_ref[...]
    zpad = jnp.zeros((c, GDN_DK), BF16)
    chains = [(b, hd) for b in range(BATCH) for hd in range(GDN_HEADS)]

    def chunk_body(ci, carry):
        rows = pl.ds(pl.multiple_of(ci * c, c), c)
        wqs = {}
        for b, hd in chains:
            cols = slice(hd * GDN_DK, (hd + 1) * GDN_DK)
            wq = jnp.concatenate([w_ref[b, rows, cols], qd_ref[b, rows, cols]], axis=0)
            wqs[b, hd] = _dot(wq, state_ref[b, hd].astype(BF16))
        for b, hd in chains:
            cols = slice(hd * GDN_DK, (hd + 1) * GDN_DK)
            v_new = u_ref[b, rows, cols].astype(F32) - wqs[b, hd][:c]
            v_b = v_new.astype(BF16)
            o = wqs[b, hd][c:] + _dot(attn_ref[b, rows, cols], jnp.concatenate([v_b, zpad], axis=0))
            e_last = el_ref[b, ci, :, cols]
            state_ref[b, hd] = state_ref[b, hd] * e_last + _dot_tn(kd_ref[b, rows, cols], v_b)
            y = _rms_scale(o) * gnw * za_ref[b, rows, cols].astype(F32)
            ya_ref[b, rows, cols] = y.astype(ya_ref.dtype)
        return carry

    lax.fori_loop(0, GDN_SCAN_TILE // c, chunk_body, 0)


def _gdn_scan(u, w, qd, kd, attn, el, za, gnw):
    hd = GDN_HEADS * GDN_DK
    n_chunks = GDN_SCAN_TILE // GDN_CHUNK
    as3 = lambda a: a.reshape(BATCH, SEQ, hd)
    tok = pl.BlockSpec((BATCH, GDN_SCAN_TILE, hd), lambda j: (0, j, 0))
    el4 = el.reshape(BATCH, SEQ // GDN_CHUNK, 1, hd)
    out = pl.pallas_call(
        _gdn_scan_kernel,
        grid=(SEQ // GDN_SCAN_TILE,),
        in_specs=[tok, tok, tok, tok, tok,
                  pl.BlockSpec((BATCH, n_chunks, 1, hd), lambda j: (0, j, 0, 0)),
                  tok, _const_spec((1, GDN_DK))],
        out_specs=tok,
        out_shape=jax.ShapeDtypeStruct((BATCH, SEQ, hd), BF16),
        scratch_shapes=[pltpu.VMEM((BATCH, GDN_HEADS, GDN_DK, GDN_DK), F32)],
        compiler_params=pltpu.CompilerParams(dimension_semantics=("arbitrary",),
                                             vmem_limit_bytes=VMEM_LIMIT),
        name="gdn_scan",
    )(as3(u), as3(w), as3(qd), as3(kd), as3(attn), el4, as3(za), gnw)
    return out.reshape(TOKENS, hd)


S5_PREP_GROUPS = 8
S5_PAIR_STATE = 2 * S5_STATE


def _s5_prep_kernel(lre_ref, lim_ref, ldt_ref, bre_ref, bim_ref, cre_ref, cim_ref, d_ref,
                    toep_ref, m1r_ref, m1i_ref, m3r_ref, m3i_ref, a16r_ref, a16i_ref):
    n = S5_CHUNK
    row = lax.broadcasted_iota(jnp.int32, (S5_W, S5_W), 0)
    col = lax.broadcasted_iota(jnp.int32, (S5_W, S5_W), 1)
    colblk = col // S5_GROUP
    for gi in range(S5_PREP_GROUPS):
        lr = jnp.minimum(lre_ref[gi], -S5_MIN_NEG)
        li = lim_ref[gi]
        dt = jnp.exp(ldt_ref[gi])
        mag = jnp.exp(lr * dt)
        ar = mag * jnp.cos(li * dt)
        ai = mag * jnp.sin(li * dt)
        den = lr * lr + li * li
        nr, ni = ar - 1.0, ai
        fr = (nr * lr + ni * li) / den
        fi = (ni * lr - nr * li) / den
        br, bi = bre_ref[gi], bim_ref[gi]
        bbr = fr * br - fi * bi
        bbi = fr * bi + fi * br
        pr = [jnp.ones_like(ar)]
        pi = [jnp.zeros_like(ar)]
        for _ in range(n):
            pr.append(pr[-1] * ar - pi[-1] * ai)
            pi.append(pr[-2] * ai + pi[-1] * ar)
        m1r = jnp.concatenate([pr[n - 1 - s] * bbr - pi[n - 1 - s] * bbi for s in range(n)], axis=0)
        m1i = jnp.concatenate([pr[n - 1 - s] * bbi + pi[n - 1 - s] * bbr for s in range(n)], axis=0)
        cr, ci = cre_ref[gi], cim_ref[gi]
        m3r = jnp.concatenate([cr * pr[t + 1] - ci * pi[t + 1] for t in range(n)], axis=0)
        m3i = jnp.concatenate([-(cr * pi[t + 1] + ci * pr[t + 1]) for t in range(n)], axis=0)
        cer = jnp.concatenate([cr] * n, axis=0)
        cei = jnp.concatenate([ci] * n, axis=0)
        rt = _dot3_nt(m1r, cer) - _dot3_nt(m1i, cei)
        toep = jnp.zeros((S5_W, S5_W), F32)
        for t in range(n):
            sh = (n - 1 - t) * S5_GROUP
            if sh:
                shifted = jnp.concatenate([rt[sh:], jnp.zeros((sh, S5_W), F32)], axis=0)
            else:
                shifted = rt
            toep = jnp.where(colblk == t, shifted, toep)
        toep = toep + jnp.where(row == col, d_ref[gi], 0.0)
        toep_ref[gi] = toep.astype(toep_ref.dtype)
        m1r_ref[gi] = m1r.astype(m1r_ref.dtype)
        m1i_ref[gi] = m1i.astype(m1i_ref.dtype)
        m3r_ref[gi] = m3r.astype(m3r_ref.dtype)
        m3i_ref[gi] = m3i.astype(m3i_ref.dtype)
        own = (lax.broadcasted_iota(jnp.int32, ar.shape, 1) // S5_STATE) == (gi % 2)
        a16r_ref[gi] = jnp.where(own, pr[n], 0.0)
        a16i_ref[gi] = jnp.where(own, pi[n], 0.0)


def _s5_prep(lre, lim, ldt, bre_t, bim_t, cre, cim, drow):
    gp = S5_PREP_GROUPS
    blk = lambda *tail: pl.BlockSpec((gp,) + tail, lambda i: (i,) + (0,) * len(tail))
    p = S5_PAIR_STATE
    out_shape = (
        jax.ShapeDtypeStruct((S5_GROUPS, S5_W, S5_W), BF16),
        jax.ShapeDtypeStruct((S5_GROUPS, S5_W, p), BF16),
        jax.ShapeDtypeStruct((S5_GROUPS, S5_W, p), BF16),
        jax.ShapeDtypeStruct((S5_GROUPS, S5_W, p), BF16),
        jax.ShapeDtypeStruct((S5_GROUPS, S5_W, p), BF16),
        jax.ShapeDtypeStruct((S5_GROUPS, 1, p), F32),
        jax.ShapeDtypeStruct((S5_GROUPS, 1, p), F32),
    )
    return pl.pallas_call(
        _s5_prep_kernel,
        grid=(S5_GROUPS // gp,),
        in_specs=[blk(1, p), blk(1, p), blk(1, 1), blk(S5_GROUP, p), blk(S5_GROUP, p),
                  blk(S5_GROUP, p), blk(S5_GROUP, p), blk(1, S5_W)],
        out_specs=(blk(S5_W, S5_W), blk(S5_W, p), blk(S5_W, p), blk(S5_W, p), blk(S5_W, p),
                   blk(1, p), blk(1, p)),
        out_shape=out_shape,
        compiler_params=pltpu.CompilerParams(dimension_semantics=("arbitrary",),
                                             vmem_limit_bytes=VMEM_LIMIT),
        name="s5_prep",
    )(lre, lim, ldt, bre_t, bim_t, cre, cim, drow)


S5_PAIRS = 2
S5_PITCH = S5_CPS + 8


def _s5_kernel(u_ref, toep_ref, m1r_ref, m1i_ref, m3r_ref, m3i_ref, a16r_ref, a16i_ref,
               y_ref, zr_ref, zi_ref, xr_ref, xi_ref, pr_ref, pi_ref):
    p = S5_PAIR_STATE
    segs = S5_SEGS
    n_blk = BATCH * segs
    blk_rows = lambda blk: slice(blk * S5_PITCH, blk * S5_PITCH + S5_CPS)
    ar, ai = [], []
    for pp in range(S5_PAIRS):
        g0, g1 = 2 * pp, 2 * pp + 1
        m1 = lambda g: jnp.concatenate([m1r_ref[g], m1i_ref[g]], axis=1)
        z = _dot(u_ref[g0], m1(g0)) + _dot(u_ref[g1], m1(g1))
        zr, zi = z[:, :p], z[:, p:]
        for blk in range(n_blk):
            zr_ref[pp, blk_rows(blk), :] = zr[blk * S5_CPS:(blk + 1) * S5_CPS]
            zi_ref[pp, blk_rows(blk), :] = zi[blk * S5_CPS:(blk + 1) * S5_CPS]
        ar.append(jnp.broadcast_to(a16r_ref[g0] + a16r_ref[g1], (segs, p)))
        ai.append(jnp.broadcast_to(a16i_ref[g0] + a16i_ref[g1], (segs, p)))

    def scan_body(i, carry):
        new = []
        for pp in range(S5_PAIRS):
            pwr, pwi, st = carry[pp]
            pr_ref[pp, pl.ds(i, 1), :] = pwr[0:1]
            pi_ref[pp, pl.ds(i, 1), :] = pwi[0:1]
            new_st = []
            for b in range(BATCH):
                sr, si = st[b]
                rb = pl.ds(b * segs * S5_PITCH + i, segs, stride=S5_PITCH)
                xr_ref[pp, rb, :] = sr
                xi_ref[pp, rb, :] = si
                new_st.append((sr * ar[pp] - si * ai[pp] + zr_ref[pp, rb, :],
                               sr * ai[pp] + si * ar[pp] + zi_ref[pp, rb, :]))
            new.append((pwr * ar[pp] - pwi * ai[pp], pwr * ai[pp] + pwi * ar[pp], tuple(new_st)))
        return tuple(new)

    zero = jnp.zeros((segs, p), F32)
    init = tuple((jnp.ones((segs, p), F32), zero, tuple((zero, zero) for _ in range(BATCH)))
                 for _ in range(S5_PAIRS))
    final = lax.fori_loop(0, S5_CPS, scan_body, init)

    sub = lax.broadcasted_iota(jnp.int32, (segs, p), 0)
    for pp in range(S5_PAIRS):
        pwr, pwi, st = final[pp]
        pw_r = pr_ref[pp]
        pw_i = pi_ref[pp]
        xr_parts, xi_parts = [], []
        for b in range(BATCH):
            fr, fi = st[b]
            gr, gi = zero, zero
            for _ in range(segs - 1):
                nr = pwr * gr - pwi * gi + fr
                ni = pwr * gi + pwi * gr + fi
                gr = jnp.where(sub == 0, 0.0, pltpu.roll(nr, 1, 0))
                gi = jnp.where(sub == 0, 0.0, pltpu.roll(ni, 1, 0))
            for s in range(segs):
                rows = blk_rows(b * segs + s)
                g_r, g_i = gr[s:s + 1], gi[s:s + 1]
                xr_parts.append(xr_ref[pp, rows, :] + pw_r * g_r - pw_i * g_i)
                xi_parts.append(xi_ref[pp, rows, :] + pw_r * g_i + pw_i * g_r)
        x_b = jnp.concatenate([jnp.concatenate(xr_parts, axis=0), jnp.concatenate(xi_parts, axis=0)],
                              axis=1).astype(BF16)
        for g in (2 * pp, 2 * pp + 1):
            m3 = jnp.concatenate([m3r_ref[g], m3i_ref[g]], axis=1)
            y = _dot(u_ref[g], toep_ref[g]) + _dot_nt(x_b, m3)
            y_ref[g] = y.astype(y_ref.dtype)


def _s5(ug, toep, m1r, m1i, m3r, m3i, a16r, a16i):
    gb = 2 * S5_PAIRS
    p = S5_PAIR_STATE
    blk = lambda *tail: pl.BlockSpec((gb,) + tail, lambda i: (i,) + (0,) * len(tail))
    seg_rows = BATCH * S5_SEGS * S5_PITCH
    return pl.pallas_call(
        _s5_kernel,
        grid=(S5_GROUPS // gb,),
        in_specs=[blk(S5_ROWS, S5_W), blk(S5_W, S5_W), blk(S5_W, p), blk(S5_W, p), blk(S5_W, p),
                  blk(S5_W, p), blk(1, p), blk(1, p)],
        out_specs=blk(S5_ROWS, S5_W),
        out_shape=jax.ShapeDtypeStruct((S5_GROUPS, S5_ROWS, S5_W), BF16),
        scratch_shapes=[pltpu.VMEM((S5_PAIRS, seg_rows, p), F32) for _ in range(4)]
        + [pltpu.VMEM((S5_PAIRS, S5_CPS, p), F32) for _ in range(2)],
        compiler_params=pltpu.CompilerParams(dimension_semantics=("arbitrary",),
                                             vmem_limit_bytes=VMEM_LIMIT),
        name="s5",
    )(ug, toep, m1r, m1i, m3r, m3i, a16r, a16i)


def _gelu_tanh(x):
    return 0.5 * x * (1.0 + jnp.tanh(math.sqrt(2.0 / math.pi) * (x + 0.044715 * (x * x * x))))


def _out0_kernel(x_ref, yg_ref, zb_ref, ya_ref, wglu_ref, wout_ref, npost_ref, o_ref, ys_ref):
    chunks = TM // S5_CHUNK
    for j in range(D_MODEL // 128):
        for half in range(S5_CHUNK // S5_LANE_BLOCKS):
            ws = [yg_ref[j * S5_LANE_BLOCKS + g8, :, half * 128:(half + 1) * 128].astype(F32)
                  for g8 in range(S5_LANE_BLOCKS)]
            ws = _block_transpose(ws)
            for a in range(S5_LANE_BLOCKS):
                ys_ref[j, pl.ds(half * S5_LANE_BLOCKS + a, chunks, stride=S5_CHUNK), :] = ws[a]
    y = _gelu_tanh(jnp.concatenate([ys_ref[j] for j in range(D_MODEL // 128)], axis=1))
    y = y * _sigmoid(_dot(y.astype(BF16), wglu_ref[...]))
    yb = (y * zb_ref[...].astype(F32)).astype(BF16)
    half = GDN_HEADS * GDN_DK
    mix = _dot(ya_ref[...], wout_ref[0:half, :]) + _dot(yb, wout_ref[half:, :])
    o_ref[...] = x_ref[...] + _rms_scale(mix) * npost_ref[...]


def _out0(x2, yg, zb, ya, wglu, wout, npost):
    tok = lambda cols: pl.BlockSpec((TM, cols), lambda i: (i, 0))
    yg_spec = pl.BlockSpec((S5_GROUPS, TM // S5_CHUNK, S5_W), lambda i: (0, i, 0))
    return pl.pallas_call(
        _out0_kernel,
        grid=(TOKENS // TM,),
        in_specs=[tok(D_MODEL), yg_spec, tok(D_MODEL), tok(D_MODEL), _const_spec(wglu.shape),
                  _const_spec(wout.shape), _const_spec((1, D_MODEL))],
        out_specs=tok(D_MODEL),
        out_shape=jax.ShapeDtypeStruct((TOKENS, D_MODEL), F32),
        scratch_shapes=[pltpu.VMEM((D_MODEL // 128, TM, 128), F32)],
        compiler_params=pltpu.CompilerParams(dimension_semantics=("arbitrary",),
                                             vmem_limit_bytes=VMEM_LIMIT),
        name="out0",
    )(x2, yg, zb, ya, wglu, wout, npost)


def _layer1_kernel(x_ref, npre_ref, win_ref, convw_ref, wout_ref, npost_ref, o_ref, ext_ref):
    tiles_per_seq = SEQ // TM

    @pl.when(pl.program_id(0) % tiles_per_seq == 0)
    def _():
        ext_ref[0:HALO, :] = jnp.zeros((HALO, SC_WIDTH), F32)

    x = x_ref[...]
    h = (_rms_scale(x) * npre_ref[...]).astype(BF16)
    cw = 512
    acc = jnp.zeros((TM, D_MODEL), F32)
    for c in range(SC_WIDTH // cw):
        cols = slice(c * cw, (c + 1) * cw)
        part = lambda k: _dot(h, win_ref[:, k * SC_WIDTH + c * cw:k * SC_WIDTH + (c + 1) * cw])
        prod = part(1) * part(2)
        ext_ref[HALO:HALO + TM, cols] = prod
        w = convw_ref[:, cols]
        conv = prod * w[2:3]
        for j in range(SC_CONV - 1):
            shift = SC_CONV - 1 - j
            conv = conv + ext_ref[HALO - shift:HALO - shift + TM, cols] * w[j:j + 1]
        y = part(0) * conv * _silu(part(3))
        acc = acc + _dot(y.astype(BF16), wout_ref[cols, :])
    ext_ref[0:HALO, :] = ext_ref[TM:TM + HALO, :]
    o_ref[...] = x + _rms_scale(acc) * npost_ref[...]


def _layer1(x1, npre, win, convw, wout, npost):
    tok = lambda cols: pl.BlockSpec((TM, cols), lambda i: (i, 0))
    return pl.pallas_call(
        _layer1_kernel,
        grid=(TOKENS // TM,),
        in_specs=[tok(D_MODEL), _const_spec((1, D_MODEL)), _const_spec(win.shape),
                  _const_spec(convw.shape), _const_spec(wout.shape), _const_spec((1, D_MODEL))],
        out_specs=tok(D_MODEL),
        out_shape=jax.ShapeDtypeStruct((TOKENS, D_MODEL), F32),
        scratch_shapes=[pltpu.VMEM((TM + HALO, SC_WIDTH), F32)],
        compiler_params=pltpu.CompilerParams(dimension_semantics=("arbitrary",),
                                             vmem_limit_bytes=VMEM_LIMIT),
        name="layer1",
    )(x1, npre, win, convw, wout, npost)


def kernel(x, norm_pre, norm_post, w_in_even, conv_qkv, a_log, dt_bias, gdn_norm_w,
           s5_lam_re, s5_lam_im, s5_b_re, s5_b_im, s5_c_re, s5_c_im, s5_log_dt, s5_d,
           w_glu, w_out_even, w_in_odd, conv_short, w_out_odd):
    x2 = x.reshape(TOKENS, D_MODEL)
    hd = GDN_HEADS * GDN_DK

    w0 = w_in_even[0]
    o_za = GDN_QKV
    o_b = o_za + hd
    o_u = o_b + 2 * GDN_HEADS
    o_zb = o_u + D_MODEL
    wqkv = w0[:, :o_za].astype(BF16)
    wza = w0[:, o_za:o_b].astype(BF16)
    wba = jnp.pad(w0[:, o_b:o_u], ((0, 0), (0, 128 - 2 * GDN_HEADS))).astype(BF16)
    wu = w0[:, o_u:o_zb].astype(BF16)
    wzb = w0[:, o_zb:].astype(BF16)
    pad_g = lambda v: jnp.pad(v, (GDN_HEADS, 128 - 2 * GDN_HEADS)).reshape(1, 128)

    q, k, v, za, bg, ug, zb = _proj0(x2, norm_pre[0].reshape(1, D_MODEL), wqkv, wza, wba, wu, wzb,
                                    conv_qkv[0], pad_g(a_log[0]), pad_g(dt_bias[0]))
    u_c, w_c, q_dec, k_dec, attn, e_last = _gdn_prep(q, k, v, bg)
    ya = _gdn_scan(u_c, w_c, q_dec, k_dec, attn, e_last, za, gdn_norm_w[0].reshape(1, GDN_DK))

    p = S5_STATE
    drow = jnp.tile(s5_d[0].reshape(S5_GROUPS, 1, S5_GROUP), (1, 1, S5_CHUNK))
    odd = (jnp.arange(S5_GROUPS) % 2 == 1).reshape(S5_GROUPS, 1, 1)

    def by_parity(a):
        z = jnp.zeros_like(a)
        return jnp.concatenate([jnp.where(odd, z, a), jnp.where(odd, a, z)], axis=-1)

    toep, m1r, m1i, m3r, m3i, a16r, a16i = _s5_prep(
        by_parity(s5_lam_re[0].reshape(S5_GROUPS, 1, p)),
        by_parity(s5_lam_im[0].reshape(S5_GROUPS, 1, p)),
        s5_log_dt[0].reshape(S5_GROUPS, 1, 1),
        by_parity(jnp.swapaxes(s5_b_re[0], 1, 2)), by_parity(jnp.swapaxes(s5_b_im[0], 1, 2)),
        by_parity(s5_c_re[0]), by_parity(s5_c_im[0]), drow)
    yg = _s5(ug, toep, m1r, m1i, m3r, m3i, a16r, a16i)

    x1 = _out0(x2, yg, zb, ya, w_glu[0].astype(BF16), w_out_even[0].astype(BF16),
               norm_post[0].reshape(1, D_MODEL))

    out = _layer1(x1, norm_pre[1].reshape(1, D_MODEL), w_in_odd[0].astype(BF16), conv_short[0],
                  w_out_odd[0].astype(BF16), norm_post[1].reshape(1, D_MODEL))
    return out.reshape(BATCH, SEQ, D_MODEL)
```

```python
import functools
import math

import jax
import jax.numpy as jnp
from jax import lax
from jax.experimental import pallas as pl
from jax.experimental.pallas import tpu as pltpu

F32 = jnp.float32
BF16 = jnp.bfloat16

D_MODEL = 1024
BATCH = 2
SEQ = 8192
TOKENS = BATCH * SEQ
RMS_EPS = 1e-6

GDN_HEADS = 8
GDN_DK = 128
GDN_CONV = 4
GDN_CHUNK = 64
GDN_QKV = 3 * GDN_HEADS * GDN_DK

S5_GROUP = 16
S5_GROUPS = D_MODEL // S5_GROUP
S5_STATE = 64
S5_MIN_NEG = 1e-4
S5_CHUNK = 16
S5_SEGS = 8
S5_CPS = SEQ // (S5_CHUNK * S5_SEGS)
S5_ROWS = BATCH * SEQ // S5_CHUNK
S5_W = S5_CHUNK * S5_GROUP

SC_WIDTH = 2 * D_MODEL
SC_CONV = 3

HALO = 8
CONV_DEC = 4
VMEM_LIMIT = 56 * 1024 * 1024

TM = 512


def _dot(a, b):
    return jnp.dot(a, b, preferred_element_type=F32)


def _dot_nt(a, b):
    return lax.dot_general(a, b, (((1,), (1,)), ((), ())), preferred_element_type=F32)


def _dot_tn(a, b):
    return lax.dot_general(a, b, (((0,), (0,)), ((), ())), preferred_element_type=F32)


def _split3(x):
    x1 = x.astype(BF16)
    r = x - x1.astype(F32)
    x2 = r.astype(BF16)
    x3 = (r - x2.astype(F32)).astype(BF16)
    return x1, x2, x3


def _dot3_nt(a, b):
    a1, a2, _ = _split3(a)
    b1, b2, _ = _split3(b)
    return _dot_nt(a1, b1) + _dot_nt(a1, b2) + _dot_nt(a2, b1)


def _pack_rows(w):
    k, n = w.shape
    pairs = jnp.swapaxes(w.astype(BF16).reshape(k // 2, 2, n), -1, -2)
    return lax.bitcast_convert_type(pairs, jnp.uint32)


def _unpack_rows(w):
    return pltpu.bitcast(w, BF16)


def _sigmoid(x):
    return 1.0 / (1.0 + jnp.exp(-x))


def _silu(x):
    return x * _sigmoid(x)


def _rms_scale(x):
    return x * lax.rsqrt(jnp.mean(x * x, axis=-1, keepdims=True) + RMS_EPS)


S5_LANE_BLOCKS = 128 // S5_GROUP


def _block_transpose(ws):
    n = S5_LANE_BLOCKS
    blk = lax.broadcasted_iota(jnp.int32, ws[0].shape, 1) // S5_GROUP
    step = 1
    while step < n:
        upper = (blk & step) != 0
        new = list(ws)
        for a in range(n):
            if a & step:
                continue
            lo, hi = ws[a], ws[a + step]
            new[a] = jnp.where(upper, pltpu.roll(hi, step * S5_GROUP, 1), lo)
            new[a + step] = jnp.where(upper, hi, pltpu.roll(lo, 128 - step * S5_GROUP, 1))
        ws = new
        step *= 2
    return ws


def _proj0_kernel(x_ref, npre_ref, wqkv_ref, wza_ref, wba_ref, wu_ref, wzb_ref, convw_ref,
                  alog_ref, dtb_ref,
                  q_ref, k_ref, v_ref, za_ref, bg_ref, ug_ref, zb_ref, ext_ref, mix_ref, us_ref):
    tiles_per_seq = SEQ // TM

    @pl.when(pl.program_id(0) % tiles_per_seq == 0)
    def _():
        ext_ref[:, 0:HALO, :] = jnp.zeros((GDN_QKV // 128, HALO, 128), F32)

    h = (_rms_scale(x_ref[...]) * npre_ref[...]).astype(BF16)

    gw = 256
    chunks = TM // S5_CHUNK
    for c in range(D_MODEL // gw):
        u = _dot(h, _unpack_rows(wu_ref[:, c * gw:(c + 1) * gw]))
        for jj in range(gw // 128):
            j = c * (gw // 128) + jj
            us_ref[j] = u[:, jj * 128:(jj + 1) * 128]
            for half in range(S5_CHUNK // S5_LANE_BLOCKS):
                ws = [us_ref[j, pl.ds(half * S5_LANE_BLOCKS + a, chunks, stride=S5_CHUNK), :]
                      for a in range(S5_LANE_BLOCKS)]
                ws = _block_transpose(ws)
                for g8 in range(S5_LANE_BLOCKS):
                    ug_ref[j * S5_LANE_BLOCKS + g8, :, half * 128:(half + 1) * 128] = (
                        ws[g8].astype(ug_ref.dtype))
    for c in range(D_MODEL // gw):
        cols = slice(c * gw, (c + 1) * gw)
        za_ref[:, cols] = _silu(_dot(h, _unpack_rows(wza_ref[:, cols]))).astype(za_ref.dtype)
        zb_ref[:, cols] = _silu(_dot(h, _unpack_rows(wzb_ref[:, cols]))).astype(zb_ref.dtype)

    cw = 512
    slabs_per_chunk = cw // 128
    outs = (q_ref, k_ref, v_ref)
    for c in range(GDN_QKV // cw):
        raw = _dot(h, _unpack_rows(wqkv_ref[:, c * cw:(c + 1) * cw]))
        for sl in range(slabs_per_chunk):
            ext_ref[c * slabs_per_chunk + sl, HALO:HALO + TM, :] = raw[:, sl * 128:(sl + 1) * 128]
        which = (c * cw) // (GDN_HEADS * GDN_DK)
        out_ref = outs[which]
        for sl in range(slabs_per_chunk):
            slab = c * slabs_per_chunk + sl
            w = convw_ref[:, slab * 128:(slab + 1) * 128]
            for r in range(CONV_DEC):
                acc = None
                for j in range(GDN_CONV):
                    start = HALO + r - (GDN_CONV - 1) + j
                    term = ext_ref[slab, pl.ds(start, TM // CONV_DEC, stride=CONV_DEC), :] * w[j:j + 1]
                    acc = term if acc is None else acc + term
                sh = _silu(acc)
                if which < 2:
                    scale = lax.rsqrt(jnp.sum(sh * sh, axis=-1, keepdims=True) + RMS_EPS)
                    if which == 0:
                        scale = scale * (GDN_DK ** -0.5)
                    sh = sh * scale
                mix_ref[sl, pl.ds(r, TM // CONV_DEC, stride=CONV_DEC), :] = sh
            col0 = slab * 128 - which * GDN_HEADS * GDN_DK
            out_ref[:, col0:col0 + 128] = mix_ref[sl].astype(out_ref.dtype)
    ext_ref[:, 0:HALO, :] = ext_ref[:, TM:TM + HALO, :]

    ba = _dot(h, _unpack_rows(wba_ref[...]))
    beta = _sigmoid(ba)
    xs = ba + dtb_ref[...]
    softplus = jnp.maximum(xs, 0.0) + jnp.log(1.0 + jnp.exp(-jnp.abs(xs)))
    g = -jnp.exp(alog_ref[...]) * softplus
    lane = lax.broadcasted_iota(jnp.int32, ba.shape, 1)
    bg_ref[...] = jnp.where(lane < GDN_HEADS, beta, g)


def _const_spec(shape):
    nd = len(shape)
    return pl.BlockSpec(shape, lambda *_: (0,) * nd)


def _proj0(x2, npre, wqkv, wza, wba, wu, wzb, convw, alog, dtb):
    n_tiles = TOKENS // TM
    tok = lambda cols: pl.BlockSpec((TM, cols), lambda i: (i, 0))
    hd = GDN_HEADS * GDN_DK
    out_shape = (
        jax.ShapeDtypeStruct((TOKENS, hd), BF16),
        jax.ShapeDtypeStruct((TOKENS, hd), BF16),
        jax.ShapeDtypeStruct((TOKENS, hd), BF16),
        jax.ShapeDtypeStruct((TOKENS, hd), BF16),
        jax.ShapeDtypeStruct((TOKENS, 128), F32),
        jax.ShapeDtypeStruct((S5_GROUPS, S5_ROWS, S5_W), BF16),
        jax.ShapeDtypeStruct((TOKENS, D_MODEL), BF16),
    )
    ug_spec = pl.BlockSpec((S5_GROUPS, TM // S5_CHUNK, S5_W), lambda i: (0, i, 0))
    return pl.pallas_call(
        _proj0_kernel,
        grid=(n_tiles,),
        in_specs=[tok(D_MODEL), _const_spec((1, D_MODEL)), _const_spec(wqkv.shape),
                  _const_spec(wza.shape), _const_spec(wba.shape), _const_spec(wu.shape),
                  _const_spec(wzb.shape), _const_spec(convw.shape), _const_spec((1, 128)),
                  _const_spec((1, 128))],
        out_specs=(tok(hd), tok(hd), tok(hd), tok(hd), tok(128), ug_spec, tok(D_MODEL)),
        out_shape=out_shape,
        scratch_shapes=[pltpu.VMEM((GDN_QKV // 128, TM + HALO, 128), F32),
                        pltpu.VMEM((4, TM, 128), F32),
                        pltpu.VMEM((D_MODEL // 128, TM, 128), F32)],
        compiler_params=pltpu.CompilerParams(dimension_semantics=("arbitrary",),
                                             vmem_limit_bytes=VMEM_LIMIT),
        name="proj0",
    )(x2, npre, wqkv, wza, wba, wu, wzb, convw, alog, dtb)


GDN_PACK = 4
GDN_NGRP = GDN_HEADS // GDN_PACK
GDN_PW = GDN_PACK * GDN_CHUNK
GDN_GW = GDN_PACK * GDN_DK
GDN_PREP_TILE = 256


def _lane_bcast(x, lane, width):
    return jnp.broadcast_to(x[:, lane:lane + 1], (x.shape[0], width))


def _gdn_prep_kernel(q_ref, k_ref, v_ref, bg_ref,
                     u_ref, w_ref, qd_ref, kd_ref, attn_ref, el_ref):
    c = GDN_CHUNK
    pw = GDN_PW
    n_chunks = GDN_PREP_TILE // c

    row_p = lax.broadcasted_iota(jnp.int32, (c, pw), 0)
    lane_p = lax.broadcasted_iota(jnp.int32, (c, pw), 1)
    col_p = lane_p % c
    causal_p = row_p >= col_p
    strict_p = row_p > col_p
    eye_p = jnp.where(row_p == col_p, 1.0, 0.0).astype(F32)
    r2 = lax.broadcasted_iota(jnp.int32, (pw, pw), 0)
    c2 = lax.broadcasted_iota(jnp.int32, (pw, pw), 1)
    bd_mask = (r2 // c) == (c2 // c)
    r3 = lax.broadcasted_iota(jnp.int32, (pw, GDN_GW), 0)
    c3 = lax.broadcasted_iota(jnp.int32, (pw, GDN_GW), 1)
    kbd_mask = (r3 // c) == (c3 // GDN_DK)
    r1 = lax.broadcasted_iota(jnp.int32, (c, c), 0)
    c1 = lax.broadcasted_iota(jnp.int32, (c, c), 1)
    tri = jnp.where(r1 >= c1, 1.0, 0.0).astype(BF16)
    lane128 = lax.broadcasted_iota(jnp.int32, (c, 128), 1)
    low_half = lane128 < c

    bd_one = jnp.where(bd_mask, 1.0, 0.0).astype(BF16)
    kbd_one = jnp.where(kbd_mask, 1.0, 0.0).astype(BF16)

    def block_diag(x_b):
        return jnp.concatenate([x_b] * GDN_PACK, axis=0) * bd_one

    chunks = []
    for ci in range(n_chunks):
        rows = slice(ci * c, (ci + 1) * c)
        bg = bg_ref[rows, :]
        g_pad = jnp.where((lane128 >= GDN_HEADS) & (lane128 < 2 * GDN_HEADS), bg, 0.0)
        chunks.append(dict(ci=ci, rows=rows, bg=bg, g3=_split3(g_pad)))
    for ch in chunks:
        g1, g2, g3 = ch["g3"]
        gc = _dot(tri, g1) + _dot(tri, g2) + _dot(tri, g3)
        ch["gc"] = gc
        ch["gc_t"] = gc.T
        ch["e_gc"] = jnp.exp(gc)
        ch["e_rev"] = jnp.exp(gc[c - 1:c, :] - gc)

    ctxs = []
    for ch in chunks:
        rows, bg, gc = ch["rows"], ch["bg"], ch["gc"]
        for gh in range(GDN_NGRP):
            heads = range(gh * GDN_PACK, (gh + 1) * GDN_PACK)
            cols = slice(gh * GDN_GW, (gh + 1) * GDN_GW)
            k_b = k_ref[rows, cols]
            q_b = q_ref[rows, cols]
            wide = lambda src, off: jnp.concatenate(
                [_lane_bcast(src, off + hd, GDN_DK) for hd in heads], axis=1)
            beta_f = wide(bg, 0)
            egc_f = wide(ch["e_gc"], GDN_HEADS)
            erev_f = wide(ch["e_rev"], GDN_HEADS)
            kf = k_b.astype(F32)
            kb = kf * beta_f
            kbd = jnp.concatenate([k_b] * GDN_PACK, axis=0) * kbd_one
            kkqk = _dot_nt(jnp.concatenate([kb.astype(BF16), q_b], axis=0), kbd)

            bc = [_lane_bcast(gc, GDN_HEADS + hd, 128) for hd in heads]
            gcol = jnp.concatenate([jnp.where(low_half, bc[0], bc[1]),
                                    jnp.where(low_half, bc[2], bc[3])], axis=1)
            gt = ch["gc_t"]
            rows_t = [gt[GDN_HEADS + hd:GDN_HEADS + hd + 1, :] for hd in heads]
            grow = jnp.concatenate([jnp.concatenate(rows_t[0:2], axis=1),
                                    jnp.concatenate(rows_t[2:4], axis=1)], axis=1)
            decay = jnp.exp(jnp.minimum(gcol - grow, 0.0))
            a = jnp.where(strict_p, kkqk[:c] * decay, 0.0)
            attn = jnp.where(causal_p, kkqk[c:] * decay, 0.0)

            vb = (v_ref[rows, cols].astype(F32) * beta_f).astype(BF16)
            kbe = (kb * egc_f).astype(BF16)
            rhs = jnp.concatenate(
                [jnp.concatenate([vb[:, i * GDN_DK:(i + 1) * GDN_DK],
                                  kbe[:, i * GDN_DK:(i + 1) * GDN_DK]], axis=1)
                 for i in range(GDN_PACK)], axis=0)

            qd_ref[rows, cols] = (q_b.astype(F32) * egc_f).astype(qd_ref.dtype)
            kd_ref[rows, cols] = (kf * erev_f).astype(kd_ref.dtype)
            el_ref[ch["ci"], :, cols] = egc_f[c - 1:c, :]
            halves = [attn[:, 0:128], attn[:, 128:256]]
            for i in range(GDN_PACK):
                part = halves[i // 2]
                if i % 2:
                    part = pltpu.roll(part, c, 1)
                hd = gh * GDN_PACK + i
                attn_ref[rows, hd * GDN_DK:(hd + 1) * GDN_DK] = jnp.where(
                    low_half, part, 0.0).astype(attn_ref.dtype)
            ctxs.append(dict(rows=rows, gh=gh, t=eye_p - a, p=a, rhs=rhs))

    for cx in ctxs:
        p_b = cx["p"].astype(BF16)
        cx["p"] = _dot(p_b, block_diag(p_b))
    for lvl in range(1, 6):
        for cx in ctxs:
            p_b = cx["p"].astype(BF16)
            t_b = cx["t"].astype(BF16)
            if lvl < 5:
                res = _dot(jnp.concatenate([t_b, p_b], axis=0), block_diag(p_b))
                cx["t"] = cx["t"] + res[:c]
                cx["p"] = res[c:]
            else:
                cx["t"] = cx["t"] + _dot(t_b, block_diag(p_b))

    for cx in ctxs:
        sol = _dot(block_diag(cx["t"].astype(BF16)), cx["rhs"])
        for i in range(GDN_PACK):
            hd = cx["gh"] * GDN_PACK + i
            blk = sol[i * c:(i + 1) * c]
            u_ref[cx["rows"], hd * GDN_DK:(hd + 1) * GDN_DK] = blk[:, :GDN_DK].astype(u_ref.dtype)
            w_ref[cx["rows"], hd * GDN_DK:(hd + 1) * GDN_DK] = blk[:, GDN_DK:].astype(w_ref.dtype)


def _gdn_prep(q, k, v, bg):
    hd = GDN_HEADS * GDN_DK
    n_chunks = GDN_PREP_TILE // GDN_CHUNK
    tok = lambda cols: pl.BlockSpec((GDN_PREP_TILE, cols), lambda i: (i, 0))
    wide = jax.ShapeDtypeStruct((TOKENS, hd), BF16)
    return pl.pallas_call(
        _gdn_prep_kernel,
        grid=(TOKENS // GDN_PREP_TILE,),
        in_specs=[tok(hd), tok(hd), tok(hd), tok(128)],
        out_specs=(tok(hd), tok(hd), tok(hd), tok(hd), tok(hd),
                   pl.BlockSpec((n_chunks, 1, hd), lambda i: (i, 0, 0))),
        out_shape=(wide, wide, wide, wide, wide,
                   jax.ShapeDtypeStruct((TOKENS // GDN_CHUNK, 1, hd), F32)),
        compiler_params=pltpu.CompilerParams(dimension_semantics=("arbitrary",),
                                             vmem_limit_bytes=VMEM_LIMIT),
        name="gdn_prep",
    )(q, k, v, bg)


GDN_SCAN_TILE = 512


def _gdn_scan_kernel(u_ref, w_ref, qd_ref, kd_ref, attn_ref, el_ref, za_ref, gnw_ref,
                     ya_ref, state_ref):
    c = GDN_CHUNK

    @pl.when(pl.program_id(0) == 0)
    def _():
        state_ref[...] = jnp.zeros(state_ref.shape, F32)

    gnw = gnw_ref[...]
    zpad = jnp.zeros((c, GDN_DK), BF16)
    chains = [(b, hd) for b in range(BATCH) for hd in range(GDN_HEADS)]

    def chunk_body(ci, carry):
        rows = pl.ds(pl.multiple_of(ci * c, c), c)
        wqs = {}
        for b, hd in chains:
            cols = slice(hd * GDN_DK, (hd + 1) * GDN_DK)
            wq = jnp.concatenate([w_ref[b, rows, cols], qd_ref[b, rows, cols]], axis=0)
            wqs[b, hd] = _dot(wq, state_ref[b, hd].astype(BF16))
        for b, hd in chains:
            cols = slice(hd * GDN_DK, (hd + 1) * GDN_DK)
            v_new = u_ref[b, rows, cols].astype(F32) - wqs[b, hd][:c]
            v_b = v_new.astype(BF16)
            o = wqs[b, hd][c:] + _dot(attn_ref[b, rows, cols], jnp.concatenate([v_b, zpad], axis=0))
            e_last = el_ref[b, ci, :, cols]
            state_ref[b, hd] = state_ref[b, hd] * e_last + _dot_tn(kd_ref[b, rows, cols], v_b)
            y = _rms_scale(o) * gnw * za_ref[b, rows, cols].astype(F32)
            ya_ref[b, rows, cols] = y.astype(ya_ref.dtype)
        return carry

    lax.fori_loop(0, GDN_SCAN_TILE // c, chunk_body, 0, unroll=4)


def _gdn_scan(u, w, qd, kd, attn, el, za, gnw):
    hd = GDN_HEADS * GDN_DK
    n_chunks = GDN_SCAN_TILE // GDN_CHUNK
    as3 = lambda a: a.reshape(BATCH, SEQ, hd)
    tok = pl.BlockSpec((BATCH, GDN_SCAN_TILE, hd), lambda j: (0, j, 0))
    el4 = el.reshape(BATCH, SEQ // GDN_CHUNK, 1, hd)
    out = pl.pallas_call(
        _gdn_scan_kernel,
        grid=(SEQ // GDN_SCAN_TILE,),
        in_specs=[tok, tok, tok, tok, tok,
                  pl.BlockSpec((BATCH, n_chunks, 1, hd), lambda j: (0, j, 0, 0)),
                  tok, _const_spec((1, GDN_DK))],
        out_specs=tok,
        out_shape=jax.ShapeDtypeStruct((BATCH, SEQ, hd), BF16),
        scratch_shapes=[pltpu.VMEM((BATCH, GDN_HEADS, GDN_DK, GDN_DK), F32)],
        compiler_params=pltpu.CompilerParams(dimension_semantics=("arbitrary",),
                                             vmem_limit_bytes=VMEM_LIMIT),
        name="gdn_scan",
    )(as3(u), as3(w), as3(qd), as3(kd), as3(attn), el4, as3(za), gnw)
    return out.reshape(TOKENS, hd)


S5_PREP_GROUPS = 8
S5_PAIR_STATE = 2 * S5_STATE


def _s5_prep_kernel(lre_ref, lim_ref, ldt_ref, bre_ref, bim_ref, cre_ref, cim_ref, d_ref,
                    toep_ref, m1r_ref, m1i_ref, m3r_ref, m3i_ref, a16r_ref, a16i_ref):
    n = S5_CHUNK
    row = lax.broadcasted_iota(jnp.int32, (S5_W, S5_W), 0)
    col = lax.broadcasted_iota(jnp.int32, (S5_W, S5_W), 1)
    colblk = lax.broadcasted_iota(jnp.int32, (S5_W, 128), 1) // S5_GROUP
    for gi in range(S5_PREP_GROUPS):
        lr = jnp.minimum(lre_ref[gi], -S5_MIN_NEG)
        li = lim_ref[gi]
        dt = jnp.exp(ldt_ref[gi])
        mag = jnp.exp(lr * dt)
        ar = mag * jnp.cos(li * dt)
        ai = mag * jnp.sin(li * dt)
        den = lr * lr + li * li
        nr, ni = ar - 1.0, ai
        fr = (nr * lr + ni * li) / den
        fi = (ni * lr - nr * li) / den
        br, bi = bre_ref[gi], bim_ref[gi]
        bbr = fr * br - fi * bi
        bbi = fr * bi + fi * br
        pr = [jnp.ones_like(ar)]
        pi = [jnp.zeros_like(ar)]
        for _ in range(n):
            pr.append(pr[-1] * ar - pi[-1] * ai)
            pi.append(pr[-2] * ai + pi[-1] * ar)
        m1r = jnp.concatenate([pr[n - 1 - s] * bbr - pi[n - 1 - s] * bbi for s in range(n)], axis=0)
        m1i = jnp.concatenate([pr[n - 1 - s] * bbi + pi[n - 1 - s] * bbr for s in range(n)], axis=0)
        cr, ci = cre_ref[gi], cim_ref[gi]
        m3r = jnp.concatenate([cr * pr[t + 1] - ci * pi[t + 1] for t in range(n)], axis=0)
        m3i = jnp.concatenate([-(cr * pi[t + 1] + ci * pr[t + 1]) for t in range(n)], axis=0)
        cer = jnp.concatenate([cr] * n, axis=0)
        cei = jnp.concatenate([ci] * n, axis=0)
        rt = _dot3_nt(m1r, cer) - _dot3_nt(m1i, cei)
        halves = []
        for lt in range(S5_W // 128):
            rt_lt = rt[:, lt * 128:(lt + 1) * 128]
            part = jnp.zeros((S5_W, 128), F32)
            for tb in range(S5_LANE_BLOCKS):
                sh = (n - 1 - (lt * S5_LANE_BLOCKS + tb)) * S5_GROUP
                if sh:
                    shifted = jnp.concatenate([rt_lt[sh:], jnp.zeros((sh, 128), F32)], axis=0)
                else:
                    shifted = rt_lt
                part = jnp.where(colblk == tb, shifted, part)
            halves.append(part)
        toep = jnp.concatenate(halves, axis=1) + jnp.where(row == col, d_ref[gi], 0.0)
        toep_ref[gi] = toep.astype(toep_ref.dtype)
        m1r_ref[gi] = m1r.astype(m1r_ref.dtype)
        m1i_ref[gi] = m1i.astype(m1i_ref.dtype)
        m3r_ref[gi] = m3r.astype(m3r_ref.dtype)
        m3i_ref[gi] = m3i.astype(m3i_ref.dtype)
        own = (lax.broadcasted_iota(jnp.int32, ar.shape, 1) // S5_STATE) == (gi % 2)
        a16r_ref[gi] = jnp.where(own, pr[n], 0.0)
        a16i_ref[gi] = jnp.where(own, pi[n], 0.0)


def _s5_prep(lre, lim, ldt, bre_t, bim_t, cre, cim, drow):
    gp = S5_PREP_GROUPS
    blk = lambda *tail: pl.BlockSpec((gp,) + tail, lambda i: (i,) + (0,) * len(tail))
    p = S5_PAIR_STATE
    out_shape = (
        jax.ShapeDtypeStruct((S5_GROUPS, S5_W, S5_W), BF16),
        jax.ShapeDtypeStruct((S5_GROUPS, S5_W, p), BF16),
        jax.ShapeDtypeStruct((S5_GROUPS, S5_W, p), BF16),
        jax.ShapeDtypeStruct((S5_GROUPS, S5_W, p), BF16),
        jax.ShapeDtypeStruct((S5_GROUPS, S5_W, p), BF16),
        jax.ShapeDtypeStruct((S5_GROUPS, 1, p), F32),
        jax.ShapeDtypeStruct((S5_GROUPS, 1, p), F32),
    )
    return pl.pallas_call(
        _s5_prep_kernel,
        grid=(S5_GROUPS // gp,),
        in_specs=[blk(1, p), blk(1, p), blk(1, 1), blk(S5_GROUP, p), blk(S5_GROUP, p),
                  blk(S5_GROUP, p), blk(S5_GROUP, p), blk(1, S5_W)],
        out_specs=(blk(S5_W, S5_W), blk(S5_W, p), blk(S5_W, p), blk(S5_W, p), blk(S5_W, p),
                   blk(1, p), blk(1, p)),
        out_shape=out_shape,
        compiler_params=pltpu.CompilerParams(dimension_semantics=("arbitrary",),
                                             vmem_limit_bytes=VMEM_LIMIT),
        name="s5_prep",
    )(lre, lim, ldt, bre_t, bim_t, cre, cim, drow)


S5_PAIRS = 2
S5_PITCH = S5_CPS + 8


def _s5_kernel(u_ref, toep_ref, m1r_ref, m1i_ref, m3r_ref, m3i_ref, a16r_ref, a16i_ref,
               y_ref, zr_ref, zi_ref, xr_ref, xi_ref, pr_ref, pi_ref):
    p = S5_PAIR_STATE
    segs = S5_SEGS
    n_blk = BATCH * segs
    blk_rows = lambda blk: slice(blk * S5_PITCH, blk * S5_PITCH + S5_CPS)
    ar, ai = [], []
    for pp in range(S5_PAIRS):
        g0, g1 = 2 * pp, 2 * pp + 1
        m1 = lambda g: jnp.concatenate([m1r_ref[g], m1i_ref[g]], axis=1)
        z = _dot(u_ref[g0], m1(g0)) + _dot(u_ref[g1], m1(g1))
        zr, zi = z[:, :p], z[:, p:]
        for blk in range(n_blk):
            zr_ref[pp, blk_rows(blk), :] = zr[blk * S5_CPS:(blk + 1) * S5_CPS]
            zi_ref[pp, blk_rows(blk), :] = zi[blk * S5_CPS:(blk + 1) * S5_CPS]
        ar.append(jnp.broadcast_to(a16r_ref[g0] + a16r_ref[g1], (segs, p)))
        ai.append(jnp.broadcast_to(a16i_ref[g0] + a16i_ref[g1], (segs, p)))

    def scan_body(i, carry):
        new = []
        for pp in range(S5_PAIRS):
            pwr, pwi, st = carry[pp]
            pr_ref[pp, pl.ds(i, 1), :] = pwr[0:1]
            pi_ref[pp, pl.ds(i, 1), :] = pwi[0:1]
            new_st = []
            for b in range(BATCH):
                sr, si = st[b]
                rb = pl.ds(b * segs * S5_PITCH + i, segs, stride=S5_PITCH)
                xr_ref[pp, rb, :] = sr
                xi_ref[pp, rb, :] = si
                new_st.append((sr * ar[pp] - si * ai[pp] + zr_ref[pp, rb, :],
                               sr * ai[pp] + si * ar[pp] + zi_ref[pp, rb, :]))
            new.append((pwr * ar[pp] - pwi * ai[pp], pwr * ai[pp] + pwi * ar[pp], tuple(new_st)))
        return tuple(new)

    zero = jnp.zeros((segs, p), F32)
    init = tuple((jnp.ones((segs, p), F32), zero, tuple((zero, zero) for _ in range(BATCH)))
                 for _ in range(S5_PAIRS))
    final = lax.fori_loop(0, S5_CPS, scan_body, init)

    sub = lax.broadcasted_iota(jnp.int32, (segs, p), 0)
    for pp in range(S5_PAIRS):
        pwr, pwi, st = final[pp]
        pw_r = pr_ref[pp]
        pw_i = pi_ref[pp]
        xr_parts, xi_parts = [], []
        for b in range(BATCH):
            fr, fi = st[b]
            gr, gi = zero, zero
            for _ in range(segs - 1):
                nr = pwr * gr - pwi * gi + fr
                ni = pwr * gi + pwi * gr + fi
                gr = jnp.where(sub == 0, 0.0, pltpu.roll(nr, 1, 0))
                gi = jnp.where(sub == 0, 0.0, pltpu.roll(ni, 1, 0))
            for s in range(segs):
                rows = blk_rows(b * segs + s)
                g_r, g_i = gr[s:s + 1], gi[s:s + 1]
                xr_parts.append(xr_ref[pp, rows, :] + pw_r * g_r - pw_i * g_i)
                xi_parts.append(xi_ref[pp, rows, :] + pw_r * g_i + pw_i * g_r)
        x_b = jnp.concatenate([jnp.concatenate(xr_parts, axis=0), jnp.concatenate(xi_parts, axis=0)],
                              axis=1).astype(BF16)
        for g in (2 * pp, 2 * pp + 1):
            m3 = jnp.concatenate([m3r_ref[g], m3i_ref[g]], axis=1)
            y = _dot(u_ref[g], toep_ref[g]) + _dot_nt(x_b, m3)
            y_ref[g] = y.astype(y_ref.dtype)


def _s5(ug, toep, m1r, m1i, m3r, m3i, a16r, a16i):
    gb = 2 * S5_PAIRS
    p = S5_PAIR_STATE
    blk = lambda *tail: pl.BlockSpec((gb,) + tail, lambda i: (i,) + (0,) * len(tail))
    seg_rows = BATCH * S5_SEGS * S5_PITCH
    return pl.pallas_call(
        _s5_kernel,
        grid=(S5_GROUPS // gb,),
        in_specs=[blk(S5_ROWS, S5_W), blk(S5_W, S5_W), blk(S5_W, p), blk(S5_W, p), blk(S5_W, p),
                  blk(S5_W, p), blk(1, p), blk(1, p)],
        out_specs=blk(S5_ROWS, S5_W),
        out_shape=jax.ShapeDtypeStruct((S5_GROUPS, S5_ROWS, S5_W), BF16),
        scratch_shapes=[pltpu.VMEM((S5_PAIRS, seg_rows, p), F32) for _ in range(4)]
        + [pltpu.VMEM((S5_PAIRS, S5_CPS, p), F32) for _ in range(2)],
        compiler_params=pltpu.CompilerParams(dimension_semantics=("arbitrary",),
                                             vmem_limit_bytes=VMEM_LIMIT),
        name="s5",
    )(ug, toep, m1r, m1i, m3r, m3i, a16r, a16i)


def _gelu_tanh(x):
    return 0.5 * x * (1.0 + jnp.tanh(math.sqrt(2.0 / math.pi) * (x + 0.044715 * (x * x * x))))


def _out0_kernel(x_ref, yg_ref, zb_ref, ya_ref, wglu_ref, wout_ref, npost_ref, o_ref, ys_ref):
    chunks = TM // S5_CHUNK
    for j in range(D_MODEL // 128):
        for half in range(S5_CHUNK // S5_LANE_BLOCKS):
            ws = [yg_ref[j * S5_LANE_BLOCKS + g8, :, half * 128:(half + 1) * 128].astype(F32)
                  for g8 in range(S5_LANE_BLOCKS)]
            ws = _block_transpose(ws)
            for a in range(S5_LANE_BLOCKS):
                ys_ref[j, pl.ds(half * S5_LANE_BLOCKS + a, chunks, stride=S5_CHUNK), :] = ws[a]
    y = _gelu_tanh(jnp.concatenate([ys_ref[j] for j in range(D_MODEL // 128)], axis=1))
    y = y * _sigmoid(_dot(y.astype(BF16), _unpack_rows(wglu_ref[...])))
    yb = (y * zb_ref[...].astype(F32)).astype(BF16)
    half = GDN_HEADS * GDN_DK
    mix = (_dot(ya_ref[...], _unpack_rows(wout_ref[0:half // 2, :]))
           + _dot(yb, _unpack_rows(wout_ref[half // 2:, :])))
    o_ref[...] = x_ref[...] + _rms_scale(mix) * npost_ref[...]


def _out0(x2, yg, zb, ya, wglu, wout, npost):
    tok = lambda cols: pl.BlockSpec((TM, cols), lambda i: (i, 0))
    yg_spec = pl.BlockSpec((S5_GROUPS, TM // S5_CHUNK, S5_W), lambda i: (0, i, 0))
    return pl.pallas_call(
        _out0_kernel,
        grid=(TOKENS // TM,),
        in_specs=[tok(D_MODEL), yg_spec, tok(D_MODEL), tok(D_MODEL), _const_spec(wglu.shape),
                  _const_spec(wout.shape), _const_spec((1, D_MODEL))],
        out_specs=tok(D_MODEL),
        out_shape=jax.ShapeDtypeStruct((TOKENS, D_MODEL), F32),
        scratch_shapes=[pltpu.VMEM((D_MODEL // 128, TM, 128), F32)],
        compiler_params=pltpu.CompilerParams(dimension_semantics=("arbitrary",),
                                             vmem_limit_bytes=VMEM_LIMIT),
        name="out0",
    )(x2, yg, zb, ya, wglu, wout, npost)


def _layer1_kernel(x_ref, npre_ref, win_ref, convw_ref, wout_ref, npost_ref, o_ref, ext_ref):
    tiles_per_seq = SEQ // TM

    @pl.when(pl.program_id(0) % tiles_per_seq == 0)
    def _():
        ext_ref[0:HALO, :] = jnp.zeros((HALO, SC_WIDTH), F32)

    x = x_ref[...]
    h = (_rms_scale(x) * npre_ref[...]).astype(BF16)
    cw = 512
    acc = jnp.zeros((TM, D_MODEL), F32)
    for c in range(SC_WIDTH // cw):
        cols = slice(c * cw, (c + 1) * cw)
        part = lambda k: _dot(h, _unpack_rows(
            win_ref[:, k * SC_WIDTH + c * cw:k * SC_WIDTH + (c + 1) * cw]))
        prod = part(1) * part(2)
        ext_ref[HALO:HALO + TM, cols] = prod
        w = convw_ref[:, cols]
        conv = prod * w[2:3]
        for j in range(SC_CONV - 1):
            shift = SC_CONV - 1 - j
            conv = conv + ext_ref[HALO - shift:HALO - shift + TM, cols] * w[j:j + 1]
        y = part(0) * conv * _silu(part(3))
        acc = acc + _dot(y.astype(BF16), _unpack_rows(wout_ref[c * cw // 2:(c + 1) * cw // 2, :]))
    ext_ref[0:HALO, :] = ext_ref[TM:TM + HALO, :]
    o_ref[...] = x + _rms_scale(acc) * npost_ref[...]


def _layer1(x1, npre, win, convw, wout, npost):
    tok = lambda cols: pl.BlockSpec((TM, cols), lambda i: (i, 0))
    return pl.pallas_call(
        _layer1_kernel,
        grid=(TOKENS // TM,),
        in_specs=[tok(D_MODEL), _const_spec((1, D_MODEL)), _const_spec(win.shape),
                  _const_spec(convw.shape), _const_spec(wout.shape), _const_spec((1, D_MODEL))],
        out_specs=tok(D_MODEL),
        out_shape=jax.ShapeDtypeStruct((TOKENS, D_MODEL), F32),
        scratch_shapes=[pltpu.VMEM((TM + HALO, SC_WIDTH), F32)],
        compiler_params=pltpu.CompilerParams(dimension_semantics=("arbitrary",),
                                             vmem_limit_bytes=VMEM_LIMIT),
        name="layer1",
    )(x1, npre, win, convw, wout, npost)


def kernel(x, norm_pre, norm_post, w_in_even, conv_qkv, a_log, dt_bias, gdn_norm_w,
           s5_lam_re, s5_lam_im, s5_b_re, s5_b_im, s5_c_re, s5_c_im, s5_log_dt, s5_d,
           w_glu, w_out_even, w_in_odd, conv_short, w_out_odd):
    x2 = x.reshape(TOKENS, D_MODEL)
    hd = GDN_HEADS * GDN_DK

    w0 = w_in_even[0]
    o_za = GDN_QKV
    o_b = o_za + hd
    o_u = o_b + 2 * GDN_HEADS
    o_zb = o_u + D_MODEL
    wqkv = _pack_rows(w0[:, :o_za])
    wza = _pack_rows(w0[:, o_za:o_b])
    wba = _pack_rows(jnp.pad(w0[:, o_b:o_u], ((0, 0), (0, 128 - 2 * GDN_HEADS))))
    wu = _pack_rows(w0[:, o_u:o_zb])
    wzb = _pack_rows(w0[:, o_zb:])
    pad_g = lambda v: jnp.pad(v, (GDN_HEADS, 128 - 2 * GDN_HEADS)).reshape(1, 128)

    q, k, v, za, bg, ug, zb = _proj0(x2, norm_pre[0].reshape(1, D_MODEL), wqkv, wza, wba, wu, wzb,
                                    conv_qkv[0], pad_g(a_log[0]), pad_g(dt_bias[0]))
    u_c, w_c, q_dec, k_dec, attn, e_last = _gdn_prep(q, k, v, bg)
    ya = _gdn_scan(u_c, w_c, q_dec, k_dec, attn, e_last, za, gdn_norm_w[0].reshape(1, GDN_DK))

    p = S5_STATE
    drow = jnp.tile(s5_d[0].reshape(S5_GROUPS, 1, S5_GROUP), (1, 1, S5_CHUNK))
    odd = (jnp.arange(S5_GROUPS) % 2 == 1).reshape(S5_GROUPS, 1, 1)

    def by_parity(a):
        z = jnp.zeros_like(a)
        return jnp.concatenate([jnp.where(odd, z, a), jnp.where(odd, a, z)], axis=-1)

    toep, m1r, m1i, m3r, m3i, a16r, a16i = _s5_prep(
        by_parity(s5_lam_re[0].reshape(S5_GROUPS, 1, p)),
        by_parity(s5_lam_im[0].reshape(S5_GROUPS, 1, p)),
        s5_log_dt[0].reshape(S5_GROUPS, 1, 1),
        by_parity(jnp.swapaxes(s5_b_re[0], 1, 2)), by_parity(jnp.swapaxes(s5_b_im[0], 1, 2)),
        by_parity(s5_c_re[0]), by_parity(s5_c_im[0]), drow)
    yg = _s5(ug, toep, m1r, m1i, m3r, m3i, a16r, a16i)

    x1 = _out0(x2, yg, zb, ya, _pack_rows(w_glu[0]), _pack_rows(w_out_even[0]),
               norm_post[0].reshape(1, D_MODEL))

    out = _layer1(x1, norm_pre[1].reshape(1, D_MODEL), _pack_rows(w_in_odd[0]), conv_short[0],
                  _pack_rows(w_out_odd[0]), norm_post[1].reshape(1, D_MODEL))
    return out.reshape(BATCH, SEQ, D_MODEL)
```

```python
import functools
import math

import jax
import jax.numpy as jnp
from jax import lax
from jax.experimental import pallas as pl
from jax.experimental.pallas import tpu as pltpu

F32 = jnp.float32
BF16 = jnp.bfloat16

D_MODEL = 1024
BATCH = 2
SEQ = 8192
TOKENS = BATCH * SEQ
RMS_EPS = 1e-6

GDN_HEADS = 8
GDN_DK = 128
GDN_CONV = 4
GDN_CHUNK = 64
GDN_QKV = 3 * GDN_HEADS * GDN_DK

S5_GROUP = 16
S5_GROUPS = D_MODEL // S5_GROUP
S5_STATE = 64
S5_MIN_NEG = 1e-4
S5_CHUNK = 16
S5_SEGS = 8
S5_CPS = SEQ // (S5_CHUNK * S5_SEGS)
S5_ROWS = BATCH * SEQ // S5_CHUNK
S5_W = S5_CHUNK * S5_GROUP

SC_WIDTH = 2 * D_MODEL
SC_CONV = 3

HALO = 8
CONV_DEC = 4
VMEM_LIMIT = 56 * 1024 * 1024

TM = 512


def _dot(a, b):
    return jnp.dot(a, b, preferred_element_type=F32)


def _dot_nt(a, b):
    return lax.dot_general(a, b, (((1,), (1,)), ((), ())), preferred_element_type=F32)


def _dot_tn(a, b):
    return lax.dot_general(a, b, (((0,), (0,)), ((), ())), preferred_element_type=F32)


def _split3(x):
    x1 = x.astype(BF16)
    r = x - x1.astype(F32)
    x2 = r.astype(BF16)
    x3 = (r - x2.astype(F32)).astype(BF16)
    return x1, x2, x3


def _dot3_nt(a, b):
    a1, a2, _ = _split3(a)
    b1, b2, _ = _split3(b)
    return _dot_nt(a1, b1) + _dot_nt(a1, b2) + _dot_nt(a2, b1)


STAGE_BYTES = 2 * 1024 * 1024


def _stage_rows(n_cols):
    return STAGE_BYTES // (4 * n_cols)


def _stage_scratch(n_cols):
    return [pltpu.VMEM((2, _stage_rows(n_cols), n_cols), F32), pltpu.SemaphoreType.DMA((2,))]


def _load_weight_bf16(src_hbm, dst_ref, stage_ref, sem):
    n_rows, n_cols = dst_ref.shape
    rows = stage_ref.shape[1]
    n = n_rows // rows

    def copy(i):
        return pltpu.make_async_copy(src_hbm.at[pl.ds(i * rows, rows), pl.ds(0, n_cols)],
                                     stage_ref.at[i % 2], sem.at[i % 2])

    copy(0).start()
    for i in range(n):
        if i + 1 < n:
            copy(i + 1).start()
        copy(i).wait()
        dst_ref[i * rows:(i + 1) * rows, :] = stage_ref[i % 2].astype(BF16)


def _sigmoid(x):
    return 1.0 / (1.0 + jnp.exp(-x))


def _silu(x):
    return x * _sigmoid(x)


def _rms_scale(x):
    return x * lax.rsqrt(jnp.mean(x * x, axis=-1, keepdims=True) + RMS_EPS)


S5_LANE_BLOCKS = 128 // S5_GROUP


def _block_transpose(ws):
    n = S5_LANE_BLOCKS
    blk = lax.broadcasted_iota(jnp.int32, ws[0].shape, 1) // S5_GROUP
    step = 1
    while step < n:
        upper = (blk & step) != 0
        new = list(ws)
        for a in range(n):
            if a & step:
                continue
            lo, hi = ws[a], ws[a + step]
            new[a] = jnp.where(upper, pltpu.roll(hi, step * S5_GROUP, 1), lo)
            new[a + step] = jnp.where(upper, hi, pltpu.roll(lo, 128 - step * S5_GROUP, 1))
        ws = new
        step *= 2
    return ws


def _proj0_kernel(x_ref, npre_ref, win_hbm, wuz_hbm, wba_ref, convw_ref, alog_ref, dtb_ref,
                  q_ref, k_ref, v_ref, za_ref, bg_ref, ug_ref, zb_ref,
                  ext_ref, mix_ref, us_ref, wqz_ref, wuz_ref, stage_a, sem_a, stage_b, sem_b):
    tiles_per_seq = SEQ // TM

    @pl.when(pl.program_id(0) == 0)
    def _():
        _load_weight_bf16(win_hbm, wqz_ref, stage_a, sem_a)
        _load_weight_bf16(wuz_hbm, wuz_ref, stage_b, sem_b)

    @pl.when(pl.program_id(0) % tiles_per_seq == 0)
    def _():
        ext_ref[:, 0:HALO, :] = jnp.zeros((GDN_QKV // 128, HALO, 128), F32)

    h = (_rms_scale(x_ref[...]) * npre_ref[...]).astype(BF16)

    gw = 256
    chunks = TM // S5_CHUNK
    for c in range(D_MODEL // gw):
        u = _dot(h, wuz_ref[:, c * gw:(c + 1) * gw])
        for jj in range(gw // 128):
            j = c * (gw // 128) + jj
            us_ref[j] = u[:, jj * 128:(jj + 1) * 128]
            for half in range(S5_CHUNK // S5_LANE_BLOCKS):
                ws = [us_ref[j, pl.ds(half * S5_LANE_BLOCKS + a, chunks, stride=S5_CHUNK), :]
                      for a in range(S5_LANE_BLOCKS)]
                ws = _block_transpose(ws)
                for g8 in range(S5_LANE_BLOCKS):
                    ug_ref[j * S5_LANE_BLOCKS + g8, :, half * 128:(half + 1) * 128] = (
                        ws[g8].astype(ug_ref.dtype))
    for c in range(D_MODEL // gw):
        cols = slice(c * gw, (c + 1) * gw)
        wza = wqz_ref[:, GDN_QKV + c * gw:GDN_QKV + (c + 1) * gw]
        wzb = wuz_ref[:, D_MODEL + c * gw:D_MODEL + (c + 1) * gw]
        za_ref[:, cols] = _silu(_dot(h, wza)).astype(za_ref.dtype)
        zb_ref[:, cols] = _silu(_dot(h, wzb)).astype(zb_ref.dtype)

    cw = 512
    slabs_per_chunk = cw // 128
    outs = (q_ref, k_ref, v_ref)
    for c in range(GDN_QKV // cw):
        raw = _dot(h, wqz_ref[:, c * cw:(c + 1) * cw])
        for sl in range(slabs_per_chunk):
            ext_ref[c * slabs_per_chunk + sl, HALO:HALO + TM, :] = raw[:, sl * 128:(sl + 1) * 128]
        which = (c * cw) // (GDN_HEADS * GDN_DK)
        out_ref = outs[which]
        for sl in range(slabs_per_chunk):
            slab = c * slabs_per_chunk + sl
            w = convw_ref[:, slab * 128:(slab + 1) * 128]
            for r in range(CONV_DEC):
                acc = None
                for j in range(GDN_CONV):
                    start = HALO + r - (GDN_CONV - 1) + j
                    term = ext_ref[slab, pl.ds(start, TM // CONV_DEC, stride=CONV_DEC), :] * w[j:j + 1]
                    acc = term if acc is None else acc + term
                sh = _silu(acc)
                if which < 2:
                    scale = lax.rsqrt(jnp.sum(sh * sh, axis=-1, keepdims=True) + RMS_EPS)
                    if which == 0:
                        scale = scale * (GDN_DK ** -0.5)
                    sh = sh * scale
                mix_ref[sl, pl.ds(r, TM // CONV_DEC, stride=CONV_DEC), :] = sh
            col0 = slab * 128 - which * GDN_HEADS * GDN_DK
            out_ref[:, col0:col0 + 128] = mix_ref[sl].astype(out_ref.dtype)
    ext_ref[:, 0:HALO, :] = ext_ref[:, TM:TM + HALO, :]

    ba = _dot(h, wba_ref[...].astype(BF16))
    beta = _sigmoid(ba)
    xs = ba + dtb_ref[...]
    softplus = jnp.maximum(xs, 0.0) + jnp.log(1.0 + jnp.exp(-jnp.abs(xs)))
    g = -jnp.exp(alog_ref[...]) * softplus
    lane = lax.broadcasted_iota(jnp.int32, ba.shape, 1)
    bg_ref[...] = jnp.where(lane < GDN_HEADS, beta, g)


def _const_spec(shape):
    nd = len(shape)
    return pl.BlockSpec(shape, lambda *_: (0,) * nd)


def _proj0(x2, npre, w_in, w_uz, wba, convw, alog, dtb):
    n_tiles = TOKENS // TM
    n_qz = GDN_QKV + GDN_HEADS * GDN_DK
    hbm = pl.BlockSpec(memory_space=pl.ANY)
    tok = lambda cols: pl.BlockSpec((TM, cols), lambda i: (i, 0))
    hd = GDN_HEADS * GDN_DK
    out_shape = (
        jax.ShapeDtypeStruct((TOKENS, hd), BF16),
        jax.ShapeDtypeStruct((TOKENS, hd), BF16),
        jax.ShapeDtypeStruct((TOKENS, hd), BF16),
        jax.ShapeDtypeStruct((TOKENS, hd), BF16),
        jax.ShapeDtypeStruct((TOKENS, 128), F32),
        jax.ShapeDtypeStruct((S5_GROUPS, S5_ROWS, S5_W), BF16),
        jax.ShapeDtypeStruct((TOKENS, D_MODEL), BF16),
    )
    ug_spec = pl.BlockSpec((S5_GROUPS, TM // S5_CHUNK, S5_W), lambda i: (0, i, 0))
    return pl.pallas_call(
        _proj0_kernel,
        grid=(n_tiles,),
        in_specs=[tok(D_MODEL), _const_spec((1, D_MODEL)), hbm, hbm, _const_spec(wba.shape),
                  _const_spec(convw.shape), _const_spec((1, 128)), _const_spec((1, 128))],
        out_specs=(tok(hd), tok(hd), tok(hd), tok(hd), tok(128), ug_spec, tok(D_MODEL)),
        out_shape=out_shape,
        scratch_shapes=[pltpu.VMEM((GDN_QKV // 128, TM + HALO, 128), F32),
                        pltpu.VMEM((4, TM, 128), F32),
                        pltpu.VMEM((D_MODEL // 128, TM, 128), F32),
                        pltpu.VMEM((D_MODEL, n_qz), BF16),
                        pltpu.VMEM((D_MODEL, 2 * D_MODEL), BF16)]
        + _stage_scratch(n_qz) + _stage_scratch(2 * D_MODEL),
        compiler_params=pltpu.CompilerParams(dimension_semantics=("arbitrary",),
                                             vmem_limit_bytes=VMEM_LIMIT),
        name="proj0",
    )(x2, npre, w_in, w_uz, wba, convw, alog, dtb)


GDN_PACK = 4
GDN_NGRP = GDN_HEADS // GDN_PACK
GDN_PW = GDN_PACK * GDN_CHUNK
GDN_GW = GDN_PACK * GDN_DK
GDN_PREP_TILE = 256


def _lane_bcast(x, lane, width):
    return jnp.broadcast_to(x[:, lane:lane + 1], (x.shape[0], width))


def _gdn_prep_kernel(q_ref, k_ref, v_ref, bg_ref,
                     u_ref, w_ref, qd_ref, kd_ref, attn_ref, el_ref):
    c = GDN_CHUNK
    pw = GDN_PW
    n_chunks = GDN_PREP_TILE // c

    row_p = lax.broadcasted_iota(jnp.int32, (c, pw), 0)
    lane_p = lax.broadcasted_iota(jnp.int32, (c, pw), 1)
    col_p = lane_p % c
    causal_p = row_p >= col_p
    strict_p = row_p > col_p
    eye_p = jnp.where(row_p == col_p, 1.0, 0.0).astype(F32)
    r2 = lax.broadcasted_iota(jnp.int32, (pw, pw), 0)
    c2 = lax.broadcasted_iota(jnp.int32, (pw, pw), 1)
    bd_mask = (r2 // c) == (c2 // c)
    r3 = lax.broadcasted_iota(jnp.int32, (pw, GDN_GW), 0)
    c3 = lax.broadcasted_iota(jnp.int32, (pw, GDN_GW), 1)
    kbd_mask = (r3 // c) == (c3 // GDN_DK)
    r1 = lax.broadcasted_iota(jnp.int32, (c, c), 0)
    c1 = lax.broadcasted_iota(jnp.int32, (c, c), 1)
    tri = jnp.where(r1 >= c1, 1.0, 0.0).astype(BF16)
    lane128 = lax.broadcasted_iota(jnp.int32, (c, 128), 1)
    low_half = lane128 < c

    bd_one = jnp.where(bd_mask, 1.0, 0.0).astype(BF16)
    kbd_one = jnp.where(kbd_mask, 1.0, 0.0).astype(BF16)

    def block_diag(x_b):
        return jnp.concatenate([x_b] * GDN_PACK, axis=0) * bd_one

    chunks = []
    for ci in range(n_chunks):
        rows = slice(ci * c, (ci + 1) * c)
        bg = bg_ref[rows, :]
        g_pad = jnp.where((lane128 >= GDN_HEADS) & (lane128 < 2 * GDN_HEADS), bg, 0.0)
        chunks.append(dict(ci=ci, rows=rows, bg=bg, g3=_split3(g_pad)))
    for ch in chunks:
        g1, g2, g3 = ch["g3"]
        gc = _dot(tri, g1) + _dot(tri, g2) + _dot(tri, g3)
        ch["gc"] = gc
        ch["gc_t"] = gc.T
        ch["e_gc"] = jnp.exp(gc)
        ch["e_rev"] = jnp.exp(gc[c - 1:c, :] - gc)

    ctxs = []
    for ch in chunks:
        rows, bg, gc = ch["rows"], ch["bg"], ch["gc"]
        for gh in range(GDN_NGRP):
            heads = range(gh * GDN_PACK, (gh + 1) * GDN_PACK)
            cols = slice(gh * GDN_GW, (gh + 1) * GDN_GW)
            k_b = k_ref[rows, cols]
            q_b = q_ref[rows, cols]
            wide = lambda src, off: jnp.concatenate(
                [_lane_bcast(src, off + hd, GDN_DK) for hd in heads], axis=1)
            beta_f = wide(bg, 0)
            egc_f = wide(ch["e_gc"], GDN_HEADS)
            erev_f = wide(ch["e_rev"], GDN_HEADS)
            kf = k_b.astype(F32)
            kb = kf * beta_f
            kbd = jnp.concatenate([k_b] * GDN_PACK, axis=0) * kbd_one
            kkqk = _dot_nt(jnp.concatenate([kb.astype(BF16), q_b], axis=0), kbd)

            bc = [_lane_bcast(gc, GDN_HEADS + hd, 128) for hd in heads]
            gcol = jnp.concatenate([jnp.where(low_half, bc[0], bc[1]),
                                    jnp.where(low_half, bc[2], bc[3])], axis=1)
            gt = ch["gc_t"]
            rows_t = [gt[GDN_HEADS + hd:GDN_HEADS + hd + 1, :] for hd in heads]
            grow = jnp.concatenate([jnp.concatenate(rows_t[0:2], axis=1),
                                    jnp.concatenate(rows_t[2:4], axis=1)], axis=1)
            decay = jnp.exp(jnp.minimum(gcol - grow, 0.0))
            a = jnp.where(strict_p, kkqk[:c] * decay, 0.0)
            attn = jnp.where(causal_p, kkqk[c:] * decay, 0.0)

            vb = (v_ref[rows, cols].astype(F32) * beta_f).astype(BF16)
            kbe = (kb * egc_f).astype(BF16)
            rhs = jnp.concatenate(
                [jnp.concatenate([vb[:, i * GDN_DK:(i + 1) * GDN_DK],
                                  kbe[:, i * GDN_DK:(i + 1) * GDN_DK]], axis=1)
                 for i in range(GDN_PACK)], axis=0)

            qd_ref[rows, cols] = (q_b.astype(F32) * egc_f).astype(qd_ref.dtype)
            kd_ref[rows, cols] = (kf * erev_f).astype(kd_ref.dtype)
            el_ref[ch["ci"], :, cols] = egc_f[c - 1:c, :]
            halves = [attn[:, 0:128], attn[:, 128:256]]
            for i in range(GDN_PACK):
                part = halves[i // 2]
                if i % 2:
                    part = pltpu.roll(part, c, 1)
                hd = gh * GDN_PACK + i
                attn_ref[rows, hd * GDN_DK:(hd + 1) * GDN_DK] = jnp.where(
                    low_half, part, 0.0).astype(attn_ref.dtype)
            ctxs.append(dict(rows=rows, gh=gh, t=eye_p - a, p=a, rhs=rhs))

    for cx in ctxs:
        p_b = cx["p"].astype(BF16)
        cx["p"] = _dot(p_b, block_diag(p_b))
    for lvl in range(1, 6):
        for cx in ctxs:
            p_b = cx["p"].astype(BF16)
            t_b = cx["t"].astype(BF16)
            if lvl < 5:
                res = _dot(jnp.concatenate([t_b, p_b], axis=0), block_diag(p_b))
                cx["t"] = cx["t"] + res[:c]
                cx["p"] = res[c:]
            else:
                cx["t"] = cx["t"] + _dot(t_b, block_diag(p_b))

    for cx in ctxs:
        sol = _dot(block_diag(cx["t"].astype(BF16)), cx["rhs"])
        for i in range(GDN_PACK):
            hd = cx["gh"] * GDN_PACK + i
            blk = sol[i * c:(i + 1) * c]
            u_ref[cx["rows"], hd * GDN_DK:(hd + 1) * GDN_DK] = blk[:, :GDN_DK].astype(u_ref.dtype)
            w_ref[cx["rows"], hd * GDN_DK:(hd + 1) * GDN_DK] = blk[:, GDN_DK:].astype(w_ref.dtype)


def _gdn_prep(q, k, v, bg):
    hd = GDN_HEADS * GDN_DK
    n_chunks = GDN_PREP_TILE // GDN_CHUNK
    tok = lambda cols: pl.BlockSpec((GDN_PREP_TILE, cols), lambda i: (i, 0))
    wide = jax.ShapeDtypeStruct((TOKENS, hd), BF16)
    return pl.pallas_call(
        _gdn_prep_kernel,
        grid=(TOKENS // GDN_PREP_TILE,),
        in_specs=[tok(hd), tok(hd), tok(hd), tok(128)],
        out_specs=(tok(hd), tok(hd), tok(hd), tok(hd), tok(hd),
                   pl.BlockSpec((n_chunks, 1, hd), lambda i: (i, 0, 0))),
        out_shape=(wide, wide, wide, wide, wide,
                   jax.ShapeDtypeStruct((TOKENS // GDN_CHUNK, 1, hd), F32)),
        compiler_params=pltpu.CompilerParams(dimension_semantics=("arbitrary",),
                                             vmem_limit_bytes=VMEM_LIMIT),
        name="gdn_prep",
    )(q, k, v, bg)


GDN_SCAN_TILE = 512


def _gdn_scan_kernel(u_ref, w_ref, qd_ref, kd_ref, attn_ref, el_ref, za_ref, gnw_ref,
                     ya_ref, state_ref):
    c = GDN_CHUNK

    @pl.when(pl.program_id(0) == 0)
    def _():
        state_ref[...] = jnp.zeros(state_ref.shape, F32)

    gnw = gnw_ref[...]
    zpad = jnp.zeros((c, GDN_DK), BF16)
    chains = [(b, hd) for b in range(BATCH) for hd in range(GDN_HEADS)]

    def chunk_body(ci, carry):
        rows = pl.ds(pl.multiple_of(ci * c, c), c)
        wqs = {}
        for b, hd in chains:
            cols = slice(hd * GDN_DK, (hd + 1) * GDN_DK)
            wq = jnp.concatenate([w_ref[b, rows, cols], qd_ref[b, rows, cols]], axis=0)
            wqs[b, hd] = _dot(wq, state_ref[b, hd].astype(BF16))
        for b, hd in chains:
            cols = slice(hd * GDN_DK, (hd + 1) * GDN_DK)
            v_new = u_ref[b, rows, cols].astype(F32) - wqs[b, hd][:c]
            v_b = v_new.astype(BF16)
            o = wqs[b, hd][c:] + _dot(attn_ref[b, rows, cols], jnp.concatenate([v_b, zpad], axis=0))
            e_last = el_ref[b, ci, :, cols]
            state_ref[b, hd] = state_ref[b, hd] * e_last + _dot_tn(kd_ref[b, rows, cols], v_b)
            y = _rms_scale(o) * gnw * za_ref[b, rows, cols].astype(F32)
            ya_ref[b, rows, cols] = y.astype(ya_ref.dtype)
        return carry

    lax.fori_loop(0, GDN_SCAN_TILE // c, chunk_body, 0, unroll=4)


def _gdn_scan(u, w, qd, kd, attn, el, za, gnw):
    hd = GDN_HEADS * GDN_DK
    n_chunks = GDN_SCAN_TILE // GDN_CHUNK
    as3 = lambda a: a.reshape(BATCH, SEQ, hd)
    tok = pl.BlockSpec((BATCH, GDN_SCAN_TILE, hd), lambda j: (0, j, 0))
    el4 = el.reshape(BATCH, SEQ // GDN_CHUNK, 1, hd)
    out = pl.pallas_call(
        _gdn_scan_kernel,
        grid=(SEQ // GDN_SCAN_TILE,),
        in_specs=[tok, tok, tok, tok, tok,
                  pl.BlockSpec((BATCH, n_chunks, 1, hd), lambda j: (0, j, 0, 0)),
                  tok, _const_spec((1, GDN_DK))],
        out_specs=tok,
        out_shape=jax.ShapeDtypeStruct((BATCH, SEQ, hd), BF16),
        scratch_shapes=[pltpu.VMEM((BATCH, GDN_HEADS, GDN_DK, GDN_DK), F32)],
        compiler_params=pltpu.CompilerParams(dimension_semantics=("arbitrary",),
                                             vmem_limit_bytes=VMEM_LIMIT),
        name="gdn_scan",
    )(as3(u), as3(w), as3(qd), as3(kd), as3(attn), el4, as3(za), gnw)
    return out.reshape(TOKENS, hd)


S5_PREP_GROUPS = 8
S5_PAIR_STATE = 2 * S5_STATE


def _s5_prep_kernel(lre_ref, lim_ref, ldt_ref, bre_ref, bim_ref, cre_ref, cim_ref, d_ref,
                    toep_ref, m1r_ref, m1i_ref, m3r_ref, m3i_ref, a16r_ref, a16i_ref):
    n = S5_CHUNK
    row = lax.broadcasted_iota(jnp.int32, (S5_W, S5_W), 0)
    col = lax.broadcasted_iota(jnp.int32, (S5_W, S5_W), 1)
    colblk = lax.broadcasted_iota(jnp.int32, (S5_W, 128), 1) // S5_GROUP
    for gi in range(S5_PREP_GROUPS):
        lr = jnp.minimum(lre_ref[gi], -S5_MIN_NEG)
        li = lim_ref[gi]
        dt = jnp.exp(ldt_ref[gi])
        mag = jnp.exp(lr * dt)
        ar = mag * jnp.cos(li * dt)
        ai = mag * jnp.sin(li * dt)
        den = lr * lr + li * li
        nr, ni = ar - 1.0, ai
        fr = (nr * lr + ni * li) / den
        fi = (ni * lr - nr * li) / den
        br, bi = bre_ref[gi], bim_ref[gi]
        bbr = fr * br - fi * bi
        bbi = fr * bi + fi * br
        pr = [jnp.ones_like(ar)]
        pi = [jnp.zeros_like(ar)]
        for _ in range(n):
            pr.append(pr[-1] * ar - pi[-1] * ai)
            pi.append(pr[-2] * ai + pi[-1] * ar)
        m1r = jnp.concatenate([pr[n - 1 - s] * bbr - pi[n - 1 - s] * bbi for s in range(n)], axis=0)
        m1i = jnp.concatenate([pr[n - 1 - s] * bbi + pi[n - 1 - s] * bbr for s in range(n)], axis=0)
        cr, ci = cre_ref[gi], cim_ref[gi]
        m3r = jnp.concatenate([cr * pr[t + 1] - ci * pi[t + 1] for t in range(n)], axis=0)
        m3i = jnp.concatenate([-(cr * pi[t + 1] + ci * pr[t + 1]) for t in range(n)], axis=0)
        cer = jnp.concatenate([cr] * n, axis=0)
        cei = jnp.concatenate([ci] * n, axis=0)
        rt = _dot3_nt(m1r, cer) - _dot3_nt(m1i, cei)
        halves = []
        for lt in range(S5_W // 128):
            rt_lt = rt[:, lt * 128:(lt + 1) * 128]
            part = jnp.zeros((S5_W, 128), F32)
            for tb in range(S5_LANE_BLOCKS):
                sh = (n - 1 - (lt * S5_LANE_BLOCKS + tb)) * S5_GROUP
                if sh:
                    shifted = jnp.concatenate([rt_lt[sh:], jnp.zeros((sh, 128), F32)], axis=0)
                else:
                    shifted = rt_lt
                part = jnp.where(colblk == tb, shifted, part)
            halves.append(part)
        toep = jnp.concatenate(halves, axis=1) + jnp.where(row == col, d_ref[gi], 0.0)
        toep_ref[gi] = toep.astype(toep_ref.dtype)
        m1r_ref[gi] = m1r.astype(m1r_ref.dtype)
        m1i_ref[gi] = m1i.astype(m1i_ref.dtype)
        m3r_ref[gi] = m3r.astype(m3r_ref.dtype)
        m3i_ref[gi] = m3i.astype(m3i_ref.dtype)
        own = (lax.broadcasted_iota(jnp.int32, ar.shape, 1) // S5_STATE) == (gi % 2)
        a16r_ref[gi] = jnp.where(own, pr[n], 0.0)
        a16i_ref[gi] = jnp.where(own, pi[n], 0.0)


def _s5_prep(lre, lim, ldt, bre_t, bim_t, cre, cim, drow):
    gp = S5_PREP_GROUPS
    blk = lambda *tail: pl.BlockSpec((gp,) + tail, lambda i: (i,) + (0,) * len(tail))
    p = S5_PAIR_STATE
    out_shape = (
        jax.ShapeDtypeStruct((S5_GROUPS, S5_W, S5_W), BF16),
        jax.ShapeDtypeStruct((S5_GROUPS, S5_W, p), BF16),
        jax.ShapeDtypeStruct((S5_GROUPS, S5_W, p), BF16),
        jax.ShapeDtypeStruct((S5_GROUPS, S5_W, p), BF16),
        jax.ShapeDtypeStruct((S5_GROUPS, S5_W, p), BF16),
        jax.ShapeDtypeStruct((S5_GROUPS, 1, p), F32),
        jax.ShapeDtypeStruct((S5_GROUPS, 1, p), F32),
    )
    return pl.pallas_call(
        _s5_prep_kernel,
        grid=(S5_GROUPS // gp,),
        in_specs=[blk(1, p), blk(1, p), blk(1, 1), blk(S5_GROUP, p), blk(S5_GROUP, p),
                  blk(S5_GROUP, p), blk(S5_GROUP, p), blk(1, S5_W)],
        out_specs=(blk(S5_W, S5_W), blk(S5_W, p), blk(S5_W, p), blk(S5_W, p), blk(S5_W, p),
                   blk(1, p), blk(1, p)),
        out_shape=out_shape,
        compiler_params=pltpu.CompilerParams(dimension_semantics=("arbitrary",),
                                             vmem_limit_bytes=VMEM_LIMIT),
        name="s5_prep",
    )(lre, lim, ldt, bre_t, bim_t, cre, cim, drow)


S5_PAIRS = 2
S5_PITCH = S5_CPS + 8


def _s5_kernel(u_ref, toep_ref, m1r_ref, m1i_ref, m3r_ref, m3i_ref, a16r_ref, a16i_ref,
               y_ref, zr_ref, zi_ref, xr_ref, xi_ref, pr_ref, pi_ref):
    p = S5_PAIR_STATE
    segs = S5_SEGS
    n_blk = BATCH * segs
    blk_rows = lambda blk: slice(blk * S5_PITCH, blk * S5_PITCH + S5_CPS)
    ar, ai = [], []
    for pp in range(S5_PAIRS):
        g0, g1 = 2 * pp, 2 * pp + 1
        m1 = lambda g: jnp.concatenate([m1r_ref[g], m1i_ref[g]], axis=1)
        z = _dot(u_ref[g0], m1(g0)) + _dot(u_ref[g1], m1(g1))
        zr, zi = z[:, :p], z[:, p:]
        for blk in range(n_blk):
            zr_ref[pp, blk_rows(blk), :] = zr[blk * S5_CPS:(blk + 1) * S5_CPS]
            zi_ref[pp, blk_rows(blk), :] = zi[blk * S5_CPS:(blk + 1) * S5_CPS]
        ar.append(jnp.broadcast_to(a16r_ref[g0] + a16r_ref[g1], (segs, p)))
        ai.append(jnp.broadcast_to(a16i_ref[g0] + a16i_ref[g1], (segs, p)))

    def scan_body(i, carry):
        new = []
        for pp in range(S5_PAIRS):
            pwr, pwi, st = carry[pp]
            pr_ref[pp, pl.ds(i, 1), :] = pwr[0:1]
            pi_ref[pp, pl.ds(i, 1), :] = pwi[0:1]
            new_st = []
            for b in range(BATCH):
                sr, si = st[b]
                rb = pl.ds(b * segs * S5_PITCH + i, segs, stride=S5_PITCH)
                xr_ref[pp, rb, :] = sr
                xi_ref[pp, rb, :] = si
                new_st.append((sr * ar[pp] - si * ai[pp] + zr_ref[pp, rb, :],
                               sr * ai[pp] + si * ar[pp] + zi_ref[pp, rb, :]))
            new.append((pwr * ar[pp] - pwi * ai[pp], pwr * ai[pp] + pwi * ar[pp], tuple(new_st)))
        return tuple(new)

    zero = jnp.zeros((segs, p), F32)
    init = tuple((jnp.ones((segs, p), F32), zero, tuple((zero, zero) for _ in range(BATCH)))
                 for _ in range(S5_PAIRS))
    final = lax.fori_loop(0, S5_CPS, scan_body, init)

    sub = lax.broadcasted_iota(jnp.int32, (segs, p), 0)
    for pp in range(S5_PAIRS):
        pwr, pwi, st = final[pp]
        pw_r = pr_ref[pp]
        pw_i = pi_ref[pp]
        xr_parts, xi_parts = [], []
        for b in range(BATCH):
            fr, fi = st[b]
            gr, gi = zero, zero
            for _ in range(segs - 1):
                nr = pwr * gr - pwi * gi + fr
                ni = pwr * gi + pwi * gr + fi
                gr = jnp.where(sub == 0, 0.0, pltpu.roll(nr, 1, 0))
                gi = jnp.where(sub == 0, 0.0, pltpu.roll(ni, 1, 0))
            for s in range(segs):
                rows = blk_rows(b * segs + s)
                g_r, g_i = gr[s:s + 1], gi[s:s + 1]
                xr_parts.append(xr_ref[pp, rows, :] + pw_r * g_r - pw_i * g_i)
                xi_parts.append(xi_ref[pp, rows, :] + pw_r * g_i + pw_i * g_r)
        x_b = jnp.concatenate([jnp.concatenate(xr_parts, axis=0), jnp.concatenate(xi_parts, axis=0)],
                              axis=1).astype(BF16)
        for g in (2 * pp, 2 * pp + 1):
            m3 = jnp.concatenate([m3r_ref[g], m3i_ref[g]], axis=1)
            y = _dot(u_ref[g], toep_ref[g]) + _dot_nt(x_b, m3)
            y_ref[g] = y.astype(y_ref.dtype)


def _s5(ug, toep, m1r, m1i, m3r, m3i, a16r, a16i):
    gb = 2 * S5_PAIRS
    p = S5_PAIR_STATE
    blk = lambda *tail: pl.BlockSpec((gb,) + tail, lambda i: (i,) + (0,) * len(tail))
    seg_rows = BATCH * S5_SEGS * S5_PITCH
    return pl.pallas_call(
        _s5_kernel,
        grid=(S5_GROUPS // gb,),
        in_specs=[blk(S5_ROWS, S5_W), blk(S5_W, S5_W), blk(S5_W, p), blk(S5_W, p), blk(S5_W, p),
                  blk(S5_W, p), blk(1, p), blk(1, p)],
        out_specs=blk(S5_ROWS, S5_W),
        out_shape=jax.ShapeDtypeStruct((S5_GROUPS, S5_ROWS, S5_W), BF16),
        scratch_shapes=[pltpu.VMEM((S5_PAIRS, seg_rows, p), F32) for _ in range(4)]
        + [pltpu.VMEM((S5_PAIRS, S5_CPS, p), F32) for _ in range(2)],
        compiler_params=pltpu.CompilerParams(dimension_semantics=("arbitrary",),
                                             vmem_limit_bytes=VMEM_LIMIT),
        name="s5",
    )(ug, toep, m1r, m1i, m3r, m3i, a16r, a16i)


def _gelu_tanh(x):
    return 0.5 * x * (1.0 + jnp.tanh(math.sqrt(2.0 / math.pi) * (x + 0.044715 * (x * x * x))))


def _out0_kernel(x_ref, yg_ref, zb_ref, ya_ref, wglu_hbm, wout_hbm, npost_ref, o_ref,
                 ys_ref, wglu_ref, wout_ref, stage_ref, sem):
    @pl.when(pl.program_id(0) == 0)
    def _():
        _load_weight_bf16(wglu_hbm, wglu_ref, stage_ref, sem)
        _load_weight_bf16(wout_hbm, wout_ref, stage_ref, sem)

    chunks = TM // S5_CHUNK
    for j in range(D_MODEL // 128):
        for half in range(S5_CHUNK // S5_LANE_BLOCKS):
            ws = [yg_ref[j * S5_LANE_BLOCKS + g8, :, half * 128:(half + 1) * 128].astype(F32)
                  for g8 in range(S5_LANE_BLOCKS)]
            ws = _block_transpose(ws)
            for a in range(S5_LANE_BLOCKS):
                ys_ref[j, pl.ds(half * S5_LANE_BLOCKS + a, chunks, stride=S5_CHUNK), :] = ws[a]
    y = _gelu_tanh(jnp.concatenate([ys_ref[j] for j in range(D_MODEL // 128)], axis=1))
    y = y * _sigmoid(_dot(y.astype(BF16), wglu_ref[...]))
    yb = (y * zb_ref[...].astype(F32)).astype(BF16)
    half = GDN_HEADS * GDN_DK
    mix = _dot(ya_ref[...], wout_ref[0:half, :]) + _dot(yb, wout_ref[half:, :])
    o_ref[...] = x_ref[...] + _rms_scale(mix) * npost_ref[...]


def _out0(x2, yg, zb, ya, wglu, wout, npost):
    tok = lambda cols: pl.BlockSpec((TM, cols), lambda i: (i, 0))
    yg_spec = pl.BlockSpec((S5_GROUPS, TM // S5_CHUNK, S5_W), lambda i: (0, i, 0))
    hbm = pl.BlockSpec(memory_space=pl.ANY)
    return pl.pallas_call(
        _out0_kernel,
        grid=(TOKENS // TM,),
        in_specs=[tok(D_MODEL), yg_spec, tok(D_MODEL), tok(D_MODEL), hbm, hbm,
                  _const_spec((1, D_MODEL))],
        out_specs=tok(D_MODEL),
        out_shape=jax.ShapeDtypeStruct((TOKENS, D_MODEL), F32),
        scratch_shapes=[pltpu.VMEM((D_MODEL // 128, TM, 128), F32),
                        pltpu.VMEM(wglu.shape, BF16), pltpu.VMEM(wout.shape, BF16)]
        + _stage_scratch(D_MODEL),
        compiler_params=pltpu.CompilerParams(dimension_semantics=("arbitrary",),
                                             vmem_limit_bytes=VMEM_LIMIT),
        name="out0",
    )(x2, yg, zb, ya, wglu, wout, npost)


def _layer1_kernel(x_ref, npre_ref, win_hbm, convw_ref, wout_hbm, npost_ref, o_ref,
                   ext_ref, win_ref, wout_ref, stage_a, sem_a, stage_b, sem_b):
    tiles_per_seq = SEQ // TM

    @pl.when(pl.program_id(0) == 0)
    def _():
        _load_weight_bf16(win_hbm, win_ref, stage_a, sem_a)
        _load_weight_bf16(wout_hbm, wout_ref, stage_b, sem_b)

    @pl.when(pl.program_id(0) % tiles_per_seq == 0)
    def _():
        ext_ref[0:HALO, :] = jnp.zeros((HALO, SC_WIDTH), F32)

    x = x_ref[...]
    h = (_rms_scale(x) * npre_ref[...]).astype(BF16)
    cw = 512
    acc = jnp.zeros((TM, D_MODEL), F32)
    for c in range(SC_WIDTH // cw):
        cols = slice(c * cw, (c + 1) * cw)
        part = lambda k: _dot(h, win_ref[:, k * SC_WIDTH + c * cw:k * SC_WIDTH + (c + 1) * cw])
        prod = part(1) * part(2)
        ext_ref[HALO:HALO + TM, cols] = prod
        w = convw_ref[:, cols]
        conv = prod * w[2:3]
        for j in range(SC_CONV - 1):
            shift = SC_CONV - 1 - j
            conv = conv + ext_ref[HALO - shift:HALO - shift + TM, cols] * w[j:j + 1]
        y = part(0) * conv * _silu(part(3))
        acc = acc + _dot(y.astype(BF16), wout_ref[cols, :])
    ext_ref[0:HALO, :] = ext_ref[TM:TM + HALO, :]
    o_ref[...] = x + _rms_scale(acc) * npost_ref[...]


def _layer1(x1, npre, win, convw, wout, npost):
    tok = lambda cols: pl.BlockSpec((TM, cols), lambda i: (i, 0))
    hbm = pl.BlockSpec(memory_space=pl.ANY)
    return pl.pallas_call(
        _layer1_kernel,
        grid=(TOKENS // TM,),
        in_specs=[tok(D_MODEL), _const_spec((1, D_MODEL)), hbm, _const_spec(convw.shape), hbm,
                  _const_spec((1, D_MODEL))],
        out_specs=tok(D_MODEL),
        out_shape=jax.ShapeDtypeStruct((TOKENS, D_MODEL), F32),
        scratch_shapes=[pltpu.VMEM((TM + HALO, SC_WIDTH), F32),
                        pltpu.VMEM(win.shape, BF16), pltpu.VMEM(wout.shape, BF16)]
        + _stage_scratch(win.shape[1]) + _stage_scratch(wout.shape[1]),
        compiler_params=pltpu.CompilerParams(dimension_semantics=("arbitrary",),
                                             vmem_limit_bytes=VMEM_LIMIT),
        name="layer1",
    )(x1, npre, win, convw, wout, npost)


def kernel(x, norm_pre, norm_post, w_in_even, conv_qkv, a_log, dt_bias, gdn_norm_w,
           s5_lam_re, s5_lam_im, s5_b_re, s5_b_im, s5_c_re, s5_c_im, s5_log_dt, s5_d,
           w_glu, w_out_even, w_in_odd, conv_short, w_out_odd):
    x2 = x.reshape(TOKENS, D_MODEL)
    hd = GDN_HEADS * GDN_DK

    w0 = w_in_even[0]
    o_b = GDN_QKV + hd
    o_u = o_b + 2 * GDN_HEADS
    w_uz = w0[:, o_u:]
    wba = jnp.pad(w0[:, o_b:o_u], ((0, 0), (0, 128 - 2 * GDN_HEADS)))
    pad_g = lambda v: jnp.pad(v, (GDN_HEADS, 128 - 2 * GDN_HEADS)).reshape(1, 128)

    q, k, v, za, bg, ug, zb = _proj0(x2, norm_pre[0].reshape(1, D_MODEL), w0, w_uz, wba,
                                    conv_qkv[0], pad_g(a_log[0]), pad_g(dt_bias[0]))
    u_c, w_c, q_dec, k_dec, attn, e_last = _gdn_prep(q, k, v, bg)
    ya = _gdn_scan(u_c, w_c, q_dec, k_dec, attn, e_last, za, gdn_norm_w[0].reshape(1, GDN_DK))

    p = S5_STATE
    drow = jnp.tile(s5_d[0].reshape(S5_GROUPS, 1, S5_GROUP), (1, 1, S5_CHUNK))
    odd = (jnp.arange(S5_GROUPS) % 2 == 1).reshape(S5_GROUPS, 1, 1)

    def by_parity(a):
        z = jnp.zeros_like(a)
        return jnp.concatenate([jnp.where(odd, z, a), jnp.where(odd, a, z)], axis=-1)

    toep, m1r, m1i, m3r, m3i, a16r, a16i = _s5_prep(
        by_parity(s5_lam_re[0].reshape(S5_GROUPS, 1, p)),
        by_parity(s5_lam_im[0].reshape(S5_GROUPS, 1, p)),
        s5_log_dt[0].reshape(S5_GROUPS, 1, 1),
        by_parity(jnp.swapaxes(s5_b_re[0], 1, 2)), by_parity(jnp.swapaxes(s5_b_im[0], 1, 2)),
        by_parity(s5_c_re[0]), by_parity(s5_c_im[0]), drow)
    yg = _s5(ug, toep, m1r, m1i, m3r, m3i, a16r, a16i)

    x1 = _out0(x2, yg, zb, ya, w_glu[0], w_out_even[0], norm_post[0].reshape(1, D_MODEL))

    out = _layer1(x1, norm_pre[1].reshape(1, D_MODEL), w_in_odd[0], conv_short[0], w_out_odd[0],
                  norm_post[1].reshape(1, D_MODEL))
    return out.reshape(BATCH, SEQ, D_MODEL)
```

```python
import functools
import math

import jax
import jax.numpy as jnp
from jax import lax
from jax.experimental import pallas as pl
from jax.experimental.pallas import tpu as pltpu

F32 = jnp.float32
BF16 = jnp.bfloat16

D_MODEL = 1024
BATCH = 2
SEQ = 8192
TOKENS = BATCH * SEQ
RMS_EPS = 1e-6

GDN_HEADS = 8
GDN_DK = 128
GDN_CONV = 4
GDN_CHUNK = 64
GDN_QKV = 3 * GDN_HEADS * GDN_DK

S5_GROUP = 16
S5_GROUPS = D_MODEL // S5_GROUP
S5_STATE = 64
S5_MIN_NEG = 1e-4
S5_CHUNK = 16
S5_SEGS = 8
S5_CPS = SEQ // (S5_CHUNK * S5_SEGS)
S5_ROWS = BATCH * SEQ // S5_CHUNK
S5_W = S5_CHUNK * S5_GROUP

SC_WIDTH = 2 * D_MODEL
SC_CONV = 3

HALO = 8
CONV_DEC = 4
VMEM_LIMIT = 56 * 1024 * 1024

TM = 512


def _dot(a, b):
    return jnp.dot(a, b, preferred_element_type=F32)


def _dot_nt(a, b):
    return lax.dot_general(a, b, (((1,), (1,)), ((), ())), preferred_element_type=F32)


def _dot_tn(a, b):
    return lax.dot_general(a, b, (((0,), (0,)), ((), ())), preferred_element_type=F32)


def _split3(x):
    x1 = x.astype(BF16)
    r = x - x1.astype(F32)
    x2 = r.astype(BF16)
    x3 = (r - x2.astype(F32)).astype(BF16)
    return x1, x2, x3


def _dot3_nt(a, b):
    a1, a2, _ = _split3(a)
    b1, b2, _ = _split3(b)
    return _dot_nt(a1, b1) + _dot_nt(a1, b2) + _dot_nt(a2, b1)


STAGE_BYTES = 2 * 1024 * 1024


def _stage_rows(n_cols):
    return STAGE_BYTES // (4 * n_cols)


def _stage_scratch(n_cols):
    return [pltpu.VMEM((2, _stage_rows(n_cols), n_cols), F32), pltpu.SemaphoreType.DMA((2,))]


def _load_weight_bf16(src_hbm, dst_ref, stage_ref, sem):
    n_rows, n_cols = dst_ref.shape
    rows = stage_ref.shape[1]
    n = n_rows // rows

    def copy(i):
        return pltpu.make_async_copy(src_hbm.at[pl.ds(i * rows, rows), pl.ds(0, n_cols)],
                                     stage_ref.at[i % 2], sem.at[i % 2])

    copy(0).start()
    for i in range(n):
        if i + 1 < n:
            copy(i + 1).start()
        copy(i).wait()
        dst_ref[i * rows:(i + 1) * rows, :] = stage_ref[i % 2].astype(BF16)


def _sigmoid(x):
    return 1.0 / (1.0 + jnp.exp(-x))


def _silu(x):
    return x * _sigmoid(x)


def _rms_scale(x):
    return x * lax.rsqrt(jnp.mean(x * x, axis=-1, keepdims=True) + RMS_EPS)


S5_LANE_BLOCKS = 128 // S5_GROUP


def _block_transpose(ws):
    n = S5_LANE_BLOCKS
    blk = lax.broadcasted_iota(jnp.int32, ws[0].shape, 1) // S5_GROUP
    step = 1
    while step < n:
        upper = (blk & step) != 0
        new = list(ws)
        for a in range(n):
            if a & step:
                continue
            lo, hi = ws[a], ws[a + step]
            new[a] = jnp.where(upper, pltpu.roll(hi, step * S5_GROUP, 1), lo)
            new[a + step] = jnp.where(upper, hi, pltpu.roll(lo, 128 - step * S5_GROUP, 1))
        ws = new
        step *= 2
    return ws


def _proj0_kernel(x_ref, npre_ref, win_hbm, wuz_hbm, wba_ref, convw_ref, alog_ref, dtb_ref,
                  q_ref, k_ref, v_ref, za_ref, bg_ref, ug_ref, zb_ref,
                  ext_ref, mix_ref, us_ref, wqz_ref, wuz_ref, stage_a, sem_a, stage_b, sem_b):
    tiles_per_seq = SEQ // TM

    @pl.when(pl.program_id(0) == 0)
    def _():
        _load_weight_bf16(win_hbm.at[0], wqz_ref, stage_a, sem_a)
        _load_weight_bf16(wuz_hbm, wuz_ref, stage_b, sem_b)

    @pl.when(pl.program_id(0) % tiles_per_seq == 0)
    def _():
        ext_ref[:, 0:HALO, :] = jnp.zeros((GDN_QKV // 128, HALO, 128), F32)

    h = (_rms_scale(x_ref[...]) * npre_ref[...]).astype(BF16)

    gw = 256
    chunks = TM // S5_CHUNK
    for c in range(D_MODEL // gw):
        u = _dot(h, wuz_ref[:, c * gw:(c + 1) * gw])
        for jj in range(gw // 128):
            j = c * (gw // 128) + jj
            us_ref[j] = u[:, jj * 128:(jj + 1) * 128]
            for half in range(S5_CHUNK // S5_LANE_BLOCKS):
                ws = [us_ref[j, pl.ds(half * S5_LANE_BLOCKS + a, chunks, stride=S5_CHUNK), :]
                      for a in range(S5_LANE_BLOCKS)]
                ws = _block_transpose(ws)
                for g8 in range(S5_LANE_BLOCKS):
                    ug_ref[j * S5_LANE_BLOCKS + g8, :, half * 128:(half + 1) * 128] = (
                        ws[g8].astype(ug_ref.dtype))
    for c in range(D_MODEL // gw):
        cols = slice(c * gw, (c + 1) * gw)
        wza = wqz_ref[:, GDN_QKV + c * gw:GDN_QKV + (c + 1) * gw]
        wzb = wuz_ref[:, D_MODEL + c * gw:D_MODEL + (c + 1) * gw]
        za_ref[:, cols] = _silu(_dot(h, wza)).astype(za_ref.dtype)
        zb_ref[:, cols] = _silu(_dot(h, wzb)).astype(zb_ref.dtype)

    cw = 512
    slabs_per_chunk = cw // 128
    outs = (q_ref, k_ref, v_ref)
    for c in range(GDN_QKV // cw):
        raw = _dot(h, wqz_ref[:, c * cw:(c + 1) * cw])
        for sl in range(slabs_per_chunk):
            ext_ref[c * slabs_per_chunk + sl, HALO:HALO + TM, :] = raw[:, sl * 128:(sl + 1) * 128]
        which = (c * cw) // (GDN_HEADS * GDN_DK)
        out_ref = outs[which]
        for sl in range(slabs_per_chunk):
            slab = c * slabs_per_chunk + sl
            w = convw_ref[:, slab * 128:(slab + 1) * 128]
            for r in range(CONV_DEC):
                acc = None
                for j in range(GDN_CONV):
                    start = HALO + r - (GDN_CONV - 1) + j
                    term = ext_ref[slab, pl.ds(start, TM // CONV_DEC, stride=CONV_DEC), :] * w[j:j + 1]
                    acc = term if acc is None else acc + term
                sh = _silu(acc)
                if which < 2:
                    scale = lax.rsqrt(jnp.sum(sh * sh, axis=-1, keepdims=True) + RMS_EPS)
                    if which == 0:
                        scale = scale * (GDN_DK ** -0.5)
                    sh = sh * scale
                mix_ref[sl, pl.ds(r, TM // CONV_DEC, stride=CONV_DEC), :] = sh
            col0 = slab * 128 - which * GDN_HEADS * GDN_DK
            out_ref[:, col0:col0 + 128] = mix_ref[sl].astype(out_ref.dtype)
    ext_ref[:, 0:HALO, :] = ext_ref[:, TM:TM + HALO, :]

    ba = _dot(h, wba_ref[...].astype(BF16))
    beta = _sigmoid(ba)
    xs = ba + dtb_ref[...]
    softplus = jnp.maximum(xs, 0.0) + jnp.log(1.0 + jnp.exp(-jnp.abs(xs)))
    g = -jnp.exp(alog_ref[...]) * softplus
    lane = lax.broadcasted_iota(jnp.int32, ba.shape, 1)
    bg_ref[...] = jnp.where(lane < GDN_HEADS, beta, g)


def _const_spec(shape):
    nd = len(shape)
    return pl.BlockSpec(shape, lambda *_: (0,) * nd)


def _proj0(x2, npre, w_in, w_uz, wba, convw, alog, dtb):
    n_tiles = TOKENS // TM
    n_qz = GDN_QKV + GDN_HEADS * GDN_DK
    hbm = pl.BlockSpec(memory_space=pl.ANY)
    tok = lambda cols: pl.BlockSpec((TM, cols), lambda i: (i, 0))
    hd = GDN_HEADS * GDN_DK
    out_shape = (
        jax.ShapeDtypeStruct((TOKENS, hd), BF16),
        jax.ShapeDtypeStruct((TOKENS, hd), BF16),
        jax.ShapeDtypeStruct((TOKENS, hd), BF16),
        jax.ShapeDtypeStruct((TOKENS, hd), BF16),
        jax.ShapeDtypeStruct((TOKENS, 128), F32),
        jax.ShapeDtypeStruct((S5_GROUPS, S5_ROWS, S5_W), BF16),
        jax.ShapeDtypeStruct((TOKENS, D_MODEL), BF16),
    )
    ug_spec = pl.BlockSpec((S5_GROUPS, TM // S5_CHUNK, S5_W), lambda i: (0, i, 0))
    return pl.pallas_call(
        _proj0_kernel,
        grid=(n_tiles,),
        in_specs=[tok(D_MODEL), _const_spec((1, D_MODEL)), hbm, hbm, _const_spec(wba.shape),
                  _const_spec(convw.shape), _const_spec((1, 128)), _const_spec((1, 128))],
        out_specs=(tok(hd), tok(hd), tok(hd), tok(hd), tok(128), ug_spec, tok(D_MODEL)),
        out_shape=out_shape,
        scratch_shapes=[pltpu.VMEM((GDN_QKV // 128, TM + HALO, 128), F32),
                        pltpu.VMEM((4, TM, 128), F32),
                        pltpu.VMEM((D_MODEL // 128, TM, 128), F32),
                        pltpu.VMEM((D_MODEL, n_qz), BF16),
                        pltpu.VMEM((D_MODEL, 2 * D_MODEL), BF16)]
        + _stage_scratch(n_qz) + _stage_scratch(2 * D_MODEL),
        compiler_params=pltpu.CompilerParams(dimension_semantics=("arbitrary",),
                                             vmem_limit_bytes=VMEM_LIMIT),
        name="proj0",
    )(x2, npre, w_in, w_uz, wba, convw, alog, dtb)


GDN_PACK = 4
GDN_NGRP = GDN_HEADS // GDN_PACK
GDN_PW = GDN_PACK * GDN_CHUNK
GDN_GW = GDN_PACK * GDN_DK
GDN_PREP_TILE = 512


def _lane_bcast(x, lane, width):
    return jnp.broadcast_to(x[:, lane:lane + 1], (x.shape[0], width))


def _gdn_prep_kernel(q_ref, k_ref, v_ref, bg_ref,
                     u_ref, w_ref, qd_ref, kd_ref, attn_ref, el_ref):
    c = GDN_CHUNK
    pw = GDN_PW
    n_chunks = GDN_PREP_TILE // c

    row_p = lax.broadcasted_iota(jnp.int32, (c, pw), 0)
    lane_p = lax.broadcasted_iota(jnp.int32, (c, pw), 1)
    col_p = lane_p % c
    causal_p = row_p >= col_p
    strict_p = row_p > col_p
    eye_p = jnp.where(row_p == col_p, 1.0, 0.0).astype(F32)
    r2 = lax.broadcasted_iota(jnp.int32, (pw, pw), 0)
    c2 = lax.broadcasted_iota(jnp.int32, (pw, pw), 1)
    bd_mask = (r2 // c) == (c2 // c)
    r3 = lax.broadcasted_iota(jnp.int32, (pw, GDN_GW), 0)
    c3 = lax.broadcasted_iota(jnp.int32, (pw, GDN_GW), 1)
    kbd_mask = (r3 // c) == (c3 // GDN_DK)
    r1 = lax.broadcasted_iota(jnp.int32, (c, c), 0)
    c1 = lax.broadcasted_iota(jnp.int32, (c, c), 1)
    tri = jnp.where(r1 >= c1, 1.0, 0.0).astype(BF16)
    lane128 = lax.broadcasted_iota(jnp.int32, (c, 128), 1)
    low_half = lane128 < c

    bd_one = jnp.where(bd_mask, 1.0, 0.0).astype(BF16)
    kbd_one = jnp.where(kbd_mask, 1.0, 0.0).astype(BF16)

    def block_diag(x_b):
        return jnp.concatenate([x_b] * GDN_PACK, axis=0) * bd_one

    chunks = []
    for ci in range(n_chunks):
        rows = slice(ci * c, (ci + 1) * c)
        bg = bg_ref[rows, :]
        g_pad = jnp.where((lane128 >= GDN_HEADS) & (lane128 < 2 * GDN_HEADS), bg, 0.0)
        chunks.append(dict(ci=ci, rows=rows, bg=bg, g3=_split3(g_pad)))
    for ch in chunks:
        g1, g2, g3 = ch["g3"]
        gc = _dot(tri, g1) + _dot(tri, g2) + _dot(tri, g3)
        ch["gc"] = gc
        ch["gc_t"] = gc.T
        ch["e_gc"] = jnp.exp(gc)
        ch["e_rev"] = jnp.exp(gc[c - 1:c, :] - gc)

    ctxs = []
    for ch in chunks:
        rows, bg, gc = ch["rows"], ch["bg"], ch["gc"]
        for gh in range(GDN_NGRP):
            heads = range(gh * GDN_PACK, (gh + 1) * GDN_PACK)
            cols = slice(gh * GDN_GW, (gh + 1) * GDN_GW)
            k_b = k_ref[rows, cols]
            q_b = q_ref[rows, cols]
            wide = lambda src, off: jnp.concatenate(
                [_lane_bcast(src, off + hd, GDN_DK) for hd in heads], axis=1)
            beta_f = wide(bg, 0)
            egc_f = wide(ch["e_gc"], GDN_HEADS)
            erev_f = wide(ch["e_rev"], GDN_HEADS)
            kf = k_b.astype(F32)
            kb = kf * beta_f
            kbd = jnp.concatenate([k_b] * GDN_PACK, axis=0) * kbd_one
            kkqk = _dot_nt(jnp.concatenate([kb.astype(BF16), q_b], axis=0), kbd)

            bc = [_lane_bcast(gc, GDN_HEADS + hd, 128) for hd in heads]
            gcol = jnp.concatenate([jnp.where(low_half, bc[0], bc[1]),
                                    jnp.where(low_half, bc[2], bc[3])], axis=1)
            gt = ch["gc_t"]
            rows_t = [gt[GDN_HEADS + hd:GDN_HEADS + hd + 1, :] for hd in heads]
            grow = jnp.concatenate([jnp.concatenate(rows_t[0:2], axis=1),
                                    jnp.concatenate(rows_t[2:4], axis=1)], axis=1)
            decay = jnp.exp(jnp.minimum(gcol - grow, 0.0))
            a = jnp.where(strict_p, kkqk[:c] * decay, 0.0)
            attn = jnp.where(causal_p, kkqk[c:] * decay, 0.0)

            vb = (v_ref[rows, cols].astype(F32) * beta_f).astype(BF16)
            kbe = (kb * egc_f).astype(BF16)
            rhs = jnp.concatenate(
                [jnp.concatenate([vb[:, i * GDN_DK:(i + 1) * GDN_DK],
                                  kbe[:, i * GDN_DK:(i + 1) * GDN_DK]], axis=1)
                 for i in range(GDN_PACK)], axis=0)

            qd_ref[rows, cols] = (q_b.astype(F32) * egc_f).astype(qd_ref.dtype)
            kd_ref[rows, cols] = (kf * erev_f).astype(kd_ref.dtype)
            el_ref[ch["ci"], :, cols] = egc_f[c - 1:c, :]
            halves = [attn[:, 0:128], attn[:, 128:256]]
            for i in range(GDN_PACK):
                part = halves[i // 2]
                if i % 2:
                    part = pltpu.roll(part, c, 1)
                hd = gh * GDN_PACK + i
                attn_ref[rows, hd * GDN_DK:(hd + 1) * GDN_DK] = jnp.where(
                    low_half, part, 0.0).astype(attn_ref.dtype)
            ctxs.append(dict(rows=rows, gh=gh, t=eye_p - a, p=a, rhs=rhs))

    for cx in ctxs:
        p_b = cx["p"].astype(BF16)
        cx["p"] = _dot(p_b, block_diag(p_b))
    for lvl in range(1, 6):
        for cx in ctxs:
            p_b = cx["p"].astype(BF16)
            t_b = cx["t"].astype(BF16)
            if lvl < 5:
                res = _dot(jnp.concatenate([t_b, p_b], axis=0), block_diag(p_b))
                cx["t"] = cx["t"] + res[:c]
                cx["p"] = res[c:]
            else:
                cx["t"] = cx["t"] + _dot(t_b, block_diag(p_b))

    for cx in ctxs:
        sol = _dot(block_diag(cx["t"].astype(BF16)), cx["rhs"])
        for i in range(GDN_PACK):
            hd = cx["gh"] * GDN_PACK + i
            blk = sol[i * c:(i + 1) * c]
            u_ref[cx["rows"], hd * GDN_DK:(hd + 1) * GDN_DK] = blk[:, :GDN_DK].astype(u_ref.dtype)
            w_ref[cx["rows"], hd * GDN_DK:(hd + 1) * GDN_DK] = blk[:, GDN_DK:].astype(w_ref.dtype)


def _gdn_prep(q, k, v, bg):
    hd = GDN_HEADS * GDN_DK
    n_chunks = GDN_PREP_TILE // GDN_CHUNK
    tok = lambda cols: pl.BlockSpec((GDN_PREP_TILE, cols), lambda i: (i, 0))
    wide = jax.ShapeDtypeStruct((TOKENS, hd), BF16)
    return pl.pallas_call(
        _gdn_prep_kernel,
        grid=(TOKENS // GDN_PREP_TILE,),
        in_specs=[tok(hd), tok(hd), tok(hd), tok(128)],
        out_specs=(tok(hd), tok(hd), tok(hd), tok(hd), tok(hd),
                   pl.BlockSpec((n_chunks, 1, hd), lambda i: (i, 0, 0))),
        out_shape=(wide, wide, wide, wide, wide,
                   jax.ShapeDtypeStruct((TOKENS // GDN_CHUNK, 1, hd), F32)),
        compiler_params=pltpu.CompilerParams(dimension_semantics=("arbitrary",),
                                             vmem_limit_bytes=VMEM_LIMIT),
        name="gdn_prep",
    )(q, k, v, bg)


GDN_SCAN_TILE = 512


def _gdn_scan_kernel(u_ref, w_ref, qd_ref, kd_ref, attn_ref, el_ref, za_ref, gnw_ref,
                     ya_ref, state_ref):
    c = GDN_CHUNK

    @pl.when(pl.program_id(0) == 0)
    def _():
        state_ref[...] = jnp.zeros(state_ref.shape, F32)

    gnw = gnw_ref[...]
    zpad = jnp.zeros((c, GDN_DK), BF16)
    chains = [(b, hd) for b in range(BATCH) for hd in range(GDN_HEADS)]

    def chunk_body(ci, carry):
        rows = pl.ds(pl.multiple_of(ci * c, c), c)
        wqs = {}
        for b, hd in chains:
            cols = slice(hd * GDN_DK, (hd + 1) * GDN_DK)
            wq = jnp.concatenate([w_ref[b, rows, cols], qd_ref[b, rows, cols]], axis=0)
            wqs[b, hd] = _dot(wq, state_ref[b, hd].astype(BF16))
        for b, hd in chains:
            cols = slice(hd * GDN_DK, (hd + 1) * GDN_DK)
            v_new = u_ref[b, rows, cols].astype(F32) - wqs[b, hd][:c]
            v_b = v_new.astype(BF16)
            o = wqs[b, hd][c:] + _dot(attn_ref[b, rows, cols], jnp.concatenate([v_b, zpad], axis=0))
            e_last = el_ref[b, ci, :, cols]
            state_ref[b, hd] = state_ref[b, hd] * e_last + _dot_tn(kd_ref[b, rows, cols], v_b)
            y = _rms_scale(o) * gnw * za_ref[b, rows, cols].astype(F32)
            ya_ref[b, rows, cols] = y.astype(ya_ref.dtype)
        return carry

    lax.fori_loop(0, GDN_SCAN_TILE // c, chunk_body, 0, unroll=4)


def _gdn_scan(u, w, qd, kd, attn, el, za, gnw):
    hd = GDN_HEADS * GDN_DK
    n_chunks = GDN_SCAN_TILE // GDN_CHUNK
    as3 = lambda a: a.reshape(BATCH, SEQ, hd)
    tok = pl.BlockSpec((BATCH, GDN_SCAN_TILE, hd), lambda j: (0, j, 0))
    el4 = el.reshape(BATCH, SEQ // GDN_CHUNK, 1, hd)
    out = pl.pallas_call(
        _gdn_scan_kernel,
        grid=(SEQ // GDN_SCAN_TILE,),
        in_specs=[tok, tok, tok, tok, tok,
                  pl.BlockSpec((BATCH, n_chunks, 1, hd), lambda j: (0, j, 0, 0)),
                  tok, _const_spec((1, GDN_DK))],
        out_specs=tok,
        out_shape=jax.ShapeDtypeStruct((BATCH, SEQ, hd), BF16),
        scratch_shapes=[pltpu.VMEM((BATCH, GDN_HEADS, GDN_DK, GDN_DK), F32)],
        compiler_params=pltpu.CompilerParams(dimension_semantics=("arbitrary",),
                                             vmem_limit_bytes=VMEM_LIMIT),
        name="gdn_scan",
    )(as3(u), as3(w), as3(qd), as3(kd), as3(attn), el4, as3(za), gnw)
    return out.reshape(TOKENS, hd)


S5_PREP_GROUPS = 8
S5_PAIR_STATE = 2 * S5_STATE


def _s5_prep_kernel(lre_ref, lim_ref, ldt_ref, bre_ref, bim_ref, cre_ref, cim_ref, d_ref,
                    toep_ref, m1r_ref, m1i_ref, m3r_ref, m3i_ref, a16r_ref, a16i_ref):
    n = S5_CHUNK
    row = lax.broadcasted_iota(jnp.int32, (S5_W, S5_W), 0)
    col = lax.broadcasted_iota(jnp.int32, (S5_W, S5_W), 1)
    colblk = lax.broadcasted_iota(jnp.int32, (S5_W, 128), 1) // S5_GROUP
    for gi in range(S5_PREP_GROUPS):
        lr = jnp.minimum(lre_ref[gi], -S5_MIN_NEG)
        li = lim_ref[gi]
        dt = jnp.exp(ldt_ref[gi])
        mag = jnp.exp(lr * dt)
        ar = mag * jnp.cos(li * dt)
        ai = mag * jnp.sin(li * dt)
        den = lr * lr + li * li
        nr, ni = ar - 1.0, ai
        fr = (nr * lr + ni * li) / den
        fi = (ni * lr - nr * li) / den
        br, bi = bre_ref[gi], bim_ref[gi]
        bbr = fr * br - fi * bi
        bbi = fr * bi + fi * br
        pr = [jnp.ones_like(ar)]
        pi = [jnp.zeros_like(ar)]
        for _ in range(n):
            pr.append(pr[-1] * ar - pi[-1] * ai)
            pi.append(pr[-2] * ai + pi[-1] * ar)
        m1r = jnp.concatenate([pr[n - 1 - s] * bbr - pi[n - 1 - s] * bbi for s in range(n)], axis=0)
        m1i = jnp.concatenate([pr[n - 1 - s] * bbi + pi[n - 1 - s] * bbr for s in range(n)], axis=0)
        cr, ci = cre_ref[gi], cim_ref[gi]
        m3r = jnp.concatenate([cr * pr[t + 1] - ci * pi[t + 1] for t in range(n)], axis=0)
        m3i = jnp.concatenate([-(cr * pi[t + 1] + ci * pr[t + 1]) for t in range(n)], axis=0)
        cer = jnp.concatenate([cr] * n, axis=0)
        cei = jnp.concatenate([ci] * n, axis=0)
        rt = _dot3_nt(m1r, cer) - _dot3_nt(m1i, cei)
        halves = []
        for lt in range(S5_W // 128):
            rt_lt = rt[:, lt * 128:(lt + 1) * 128]
            part = jnp.zeros((S5_W, 128), F32)
            for tb in range(S5_LANE_BLOCKS):
                sh = (n - 1 - (lt * S5_LANE_BLOCKS + tb)) * S5_GROUP
                if sh:
                    shifted = jnp.concatenate([rt_lt[sh:], jnp.zeros((sh, 128), F32)], axis=0)
                else:
                    shifted = rt_lt
                part = jnp.where(colblk == tb, shifted, part)
            halves.append(part)
        toep = jnp.concatenate(halves, axis=1) + jnp.where(row == col, d_ref[gi], 0.0)
        toep_ref[gi] = toep.astype(toep_ref.dtype)
        m1r_ref[gi] = m1r.astype(m1r_ref.dtype)
        m1i_ref[gi] = m1i.astype(m1i_ref.dtype)
        m3r_ref[gi] = m3r.astype(m3r_ref.dtype)
        m3i_ref[gi] = m3i.astype(m3i_ref.dtype)
        own = (lax.broadcasted_iota(jnp.int32, ar.shape, 1) // S5_STATE) == (gi % 2)
        a16r_ref[gi] = jnp.where(own, pr[n], 0.0)
        a16i_ref[gi] = jnp.where(own, pi[n], 0.0)


def _s5_prep(lre, lim, ldt, bre_t, bim_t, cre, cim, drow):
    gp = S5_PREP_GROUPS
    blk = lambda *tail: pl.BlockSpec((gp,) + tail, lambda i: (i,) + (0,) * len(tail))
    p = S5_PAIR_STATE
    out_shape = (
        jax.ShapeDtypeStruct((S5_GROUPS, S5_W, S5_W), BF16),
        jax.ShapeDtypeStruct((S5_GROUPS, S5_W, p), BF16),
        jax.ShapeDtypeStruct((S5_GROUPS, S5_W, p), BF16),
        jax.ShapeDtypeStruct((S5_GROUPS, S5_W, p), BF16),
        jax.ShapeDtypeStruct((S5_GROUPS, S5_W, p), BF16),
        jax.ShapeDtypeStruct((S5_GROUPS, 1, p), F32),
        jax.ShapeDtypeStruct((S5_GROUPS, 1, p), F32),
    )
    return pl.pallas_call(
        _s5_prep_kernel,
        grid=(S5_GROUPS // gp,),
        in_specs=[blk(1, p), blk(1, p), blk(1, 1), blk(S5_GROUP, p), blk(S5_GROUP, p),
                  blk(S5_GROUP, p), blk(S5_GROUP, p), blk(1, S5_W)],
        out_specs=(blk(S5_W, S5_W), blk(S5_W, p), blk(S5_W, p), blk(S5_W, p), blk(S5_W, p),
                   blk(1, p), blk(1, p)),
        out_shape=out_shape,
        compiler_params=pltpu.CompilerParams(dimension_semantics=("arbitrary",),
                                             vmem_limit_bytes=VMEM_LIMIT),
        name="s5_prep",
    )(lre, lim, ldt, bre_t, bim_t, cre, cim, drow)


S5_PAIRS = 2
S5_PITCH = S5_CPS + 8


def _s5_kernel(u_ref, toep_ref, m1r_ref, m1i_ref, m3r_ref, m3i_ref, a16r_ref, a16i_ref,
               y_ref, zr_ref, zi_ref, xr_ref, xi_ref, pr_ref, pi_ref):
    p = S5_PAIR_STATE
    segs = S5_SEGS
    n_blk = BATCH * segs
    blk_rows = lambda blk: slice(blk * S5_PITCH, blk * S5_PITCH + S5_CPS)
    ar, ai = [], []
    for pp in range(S5_PAIRS):
        g0, g1 = 2 * pp, 2 * pp + 1
        m1 = lambda g: jnp.concatenate([m1r_ref[g], m1i_ref[g]], axis=1)
        z = _dot(u_ref[g0], m1(g0)) + _dot(u_ref[g1], m1(g1))
        zr, zi = z[:, :p], z[:, p:]
        for blk in range(n_blk):
            zr_ref[pp, blk_rows(blk), :] = zr[blk * S5_CPS:(blk + 1) * S5_CPS]
            zi_ref[pp, blk_rows(blk), :] = zi[blk * S5_CPS:(blk + 1) * S5_CPS]
        ar.append(jnp.broadcast_to(a16r_ref[g0] + a16r_ref[g1], (segs, p)))
        ai.append(jnp.broadcast_to(a16i_ref[g0] + a16i_ref[g1], (segs, p)))

    def scan_body(i, carry):
        new = []
        for pp in range(S5_PAIRS):
            pwr, pwi, st = carry[pp]
            pr_ref[pp, pl.ds(i, 1), :] = pwr[0:1]
            pi_ref[pp, pl.ds(i, 1), :] = pwi[0:1]
            new_st = []
            for b in range(BATCH):
                sr, si = st[b]
                rb = pl.ds(b * segs * S5_PITCH + i, segs, stride=S5_PITCH)
                xr_ref[pp, rb, :] = sr
                xi_ref[pp, rb, :] = si
                new_st.append((sr * ar[pp] - si * ai[pp] + zr_ref[pp, rb, :],
                               sr * ai[pp] + si * ar[pp] + zi_ref[pp, rb, :]))
            new.append((pwr * ar[pp] - pwi * ai[pp], pwr * ai[pp] + pwi * ar[pp], tuple(new_st)))
        return tuple(new)

    zero = jnp.zeros((segs, p), F32)
    init = tuple((jnp.ones((segs, p), F32), zero, tuple((zero, zero) for _ in range(BATCH)))
                 for _ in range(S5_PAIRS))
    final = lax.fori_loop(0, S5_CPS, scan_body, init)

    sub = lax.broadcasted_iota(jnp.int32, (segs, p), 0)
    for pp in range(S5_PAIRS):
        pwr, pwi, st = final[pp]
        pw_r = pr_ref[pp]
        pw_i = pi_ref[pp]
        xr_parts, xi_parts = [], []
        for b in range(BATCH):
            fr, fi = st[b]
            gr, gi = zero, zero
            for _ in range(segs - 1):
                nr = pwr * gr - pwi * gi + fr
                ni = pwr * gi + pwi * gr + fi
                gr = jnp.where(sub == 0, 0.0, pltpu.roll(nr, 1, 0))
                gi = jnp.where(sub == 0, 0.0, pltpu.roll(ni, 1, 0))
            for s in range(segs):
                rows = blk_rows(b * segs + s)
                g_r, g_i = gr[s:s + 1], gi[s:s + 1]
                xr_parts.append(xr_ref[pp, rows, :] + pw_r * g_r - pw_i * g_i)
                xi_parts.append(xi_ref[pp, rows, :] + pw_r * g_i + pw_i * g_r)
        x_b = jnp.concatenate([jnp.concatenate(xr_parts, axis=0), jnp.concatenate(xi_parts, axis=0)],
                              axis=1).astype(BF16)
        for g in (2 * pp, 2 * pp + 1):
            m3 = jnp.concatenate([m3r_ref[g], m3i_ref[g]], axis=1)
            y = _dot(u_ref[g], toep_ref[g]) + _dot_nt(x_b, m3)
            y_ref[g] = y.astype(y_ref.dtype)


def _s5(ug, toep, m1r, m1i, m3r, m3i, a16r, a16i):
    gb = 2 * S5_PAIRS
    p = S5_PAIR_STATE
    blk = lambda *tail: pl.BlockSpec((gb,) + tail, lambda i: (i,) + (0,) * len(tail))
    seg_rows = BATCH * S5_SEGS * S5_PITCH
    return pl.pallas_call(
        _s5_kernel,
        grid=(S5_GROUPS // gb,),
        in_specs=[blk(S5_ROWS, S5_W), blk(S5_W, S5_W), blk(S5_W, p), blk(S5_W, p), blk(S5_W, p),
                  blk(S5_W, p), blk(1, p), blk(1, p)],
        out_specs=blk(S5_ROWS, S5_W),
        out_shape=jax.ShapeDtypeStruct((S5_GROUPS, S5_ROWS, S5_W), BF16),
        scratch_shapes=[pltpu.VMEM((S5_PAIRS, seg_rows, p), F32) for _ in range(4)]
        + [pltpu.VMEM((S5_PAIRS, S5_CPS, p), F32) for _ in range(2)],
        compiler_params=pltpu.CompilerParams(dimension_semantics=("arbitrary",),
                                             vmem_limit_bytes=VMEM_LIMIT),
        name="s5",
    )(ug, toep, m1r, m1i, m3r, m3i, a16r, a16i)


def _gelu_tanh(x):
    return 0.5 * x * (1.0 + jnp.tanh(math.sqrt(2.0 / math.pi) * (x + 0.044715 * (x * x * x))))


OUT0_PARTS = 2


def _out0_kernel(x_ref, yg_ref, zb_ref, ya_ref, wglu_hbm, wout_hbm, npost_ref, o_ref,
                 ys_ref, wglu_ref, wout_ref, stage_ref, sem):
    @pl.when(pl.program_id(0) == 0)
    def _():
        _load_weight_bf16(wglu_hbm, wglu_ref, stage_ref, sem)
        _load_weight_bf16(wout_hbm, wout_ref, stage_ref, sem)

    hd = GDN_HEADS * GDN_DK
    part_rows = TM // OUT0_PARTS
    chunks = part_rows // S5_CHUNK
    for part in range(OUT0_PARTS):
        r0 = part * part_rows
        rows = slice(r0, r0 + part_rows)
        crows = slice(part * chunks, (part + 1) * chunks)
        for j in range(D_MODEL // 128):
            for half in range(S5_CHUNK // S5_LANE_BLOCKS):
                ws = [yg_ref[j * S5_LANE_BLOCKS + g8, crows, half * 128:(half + 1) * 128].astype(F32)
                      for g8 in range(S5_LANE_BLOCKS)]
                ws = _block_transpose(ws)
                for a in range(S5_LANE_BLOCKS):
                    ys_ref[j, pl.ds(r0 + half * S5_LANE_BLOCKS + a, chunks, stride=S5_CHUNK), :] = ws[a]
        y = _gelu_tanh(jnp.concatenate([ys_ref[j, rows, :] for j in range(D_MODEL // 128)], axis=1))
        y = y * _sigmoid(_dot(y.astype(BF16), wglu_ref[...]))
        yb = (y * zb_ref[rows, :].astype(F32)).astype(BF16)
        mix = _dot(ya_ref[rows, :], wout_ref[0:hd, :]) + _dot(yb, wout_ref[hd:, :])
        o_ref[rows, :] = x_ref[rows, :] + _rms_scale(mix) * npost_ref[...]


def _out0(x2, yg, zb, ya, wglu, wout, npost):
    tok = lambda cols: pl.BlockSpec((TM, cols), lambda i: (i, 0))
    yg_spec = pl.BlockSpec((S5_GROUPS, TM // S5_CHUNK, S5_W), lambda i: (0, i, 0))
    hbm = pl.BlockSpec(memory_space=pl.ANY)
    return pl.pallas_call(
        _out0_kernel,
        grid=(TOKENS // TM,),
        in_specs=[tok(D_MODEL), yg_spec, tok(D_MODEL), tok(D_MODEL), hbm, hbm,
                  _const_spec((1, D_MODEL))],
        out_specs=tok(D_MODEL),
        out_shape=jax.ShapeDtypeStruct((TOKENS, D_MODEL), F32),
        scratch_shapes=[pltpu.VMEM((D_MODEL // 128, TM, 128), F32),
                        pltpu.VMEM(wglu.shape, BF16), pltpu.VMEM(wout.shape, BF16)]
        + _stage_scratch(D_MODEL),
        compiler_params=pltpu.CompilerParams(dimension_semantics=("arbitrary",),
                                             vmem_limit_bytes=VMEM_LIMIT),
        name="out0",
    )(x2, yg, zb, ya, wglu, wout, npost)


def _layer1_kernel(x_ref, npre_ref, win_hbm, convw_ref, wout_hbm, npost_ref, o_ref,
                   ext_ref, win_ref, wout_ref, stage_a, sem_a, stage_b, sem_b):
    tiles_per_seq = SEQ // TM

    @pl.when(pl.program_id(0) == 0)
    def _():
        _load_weight_bf16(win_hbm, win_ref, stage_a, sem_a)
        _load_weight_bf16(wout_hbm, wout_ref, stage_b, sem_b)

    @pl.when(pl.program_id(0) % tiles_per_seq == 0)
    def _():
        ext_ref[0:HALO, :] = jnp.zeros((HALO, SC_WIDTH), F32)

    x = x_ref[...]
    h = (_rms_scale(x) * npre_ref[...]).astype(BF16)
    cw = 512
    acc = jnp.zeros((TM, D_MODEL), F32)
    for c in range(SC_WIDTH // cw):
        cols = slice(c * cw, (c + 1) * cw)
        part = lambda k: _dot(h, win_ref[:, k * SC_WIDTH + c * cw:k * SC_WIDTH + (c + 1) * cw])
        prod = part(1) * part(2)
        ext_ref[HALO:HALO + TM, cols] = prod
        w = convw_ref[:, cols]
        conv = prod * w[2:3]
        for j in range(SC_CONV - 1):
            shift = SC_CONV - 1 - j
            conv = conv + ext_ref[HALO - shift:HALO - shift + TM, cols] * w[j:j + 1]
        y = part(0) * conv * _silu(part(3))
        acc = acc + _dot(y.astype(BF16), wout_ref[cols, :])
    ext_ref[0:HALO, :] = ext_ref[TM:TM + HALO, :]
    o_ref[...] = x + _rms_scale(acc) * npost_ref[...]


def _layer1(x1, npre, win, convw, wout, npost):
    tok = lambda cols: pl.BlockSpec((TM, cols), lambda i: (i, 0))
    hbm = pl.BlockSpec(memory_space=pl.ANY)
    return pl.pallas_call(
        _layer1_kernel,
        grid=(TOKENS // TM,),
        in_specs=[tok(D_MODEL), _const_spec((1, D_MODEL)), hbm, _const_spec(convw.shape), hbm,
                  _const_spec((1, D_MODEL))],
        out_specs=tok(D_MODEL),
        out_shape=jax.ShapeDtypeStruct((TOKENS, D_MODEL), F32),
        scratch_shapes=[pltpu.VMEM((TM + HALO, SC_WIDTH), F32),
                        pltpu.VMEM(win.shape, BF16), pltpu.VMEM(wout.shape, BF16)]
        + _stage_scratch(win.shape[1]) + _stage_scratch(wout.shape[1]),
        compiler_params=pltpu.CompilerParams(dimension_semantics=("arbitrary",),
                                             vmem_limit_bytes=VMEM_LIMIT),
        name="layer1",
    )(x1, npre, win, convw, wout, npost)


def kernel(x, norm_pre, norm_post, w_in_even, conv_qkv, a_log, dt_bias, gdn_norm_w,
           s5_lam_re, s5_lam_im, s5_b_re, s5_b_im, s5_c_re, s5_c_im, s5_log_dt, s5_d,
           w_glu, w_out_even, w_in_odd, conv_short, w_out_odd):
    x2 = x.reshape(TOKENS, D_MODEL)
    hd = GDN_HEADS * GDN_DK

    w0 = w_in_even[0]
    o_b = GDN_QKV + hd
    o_u = o_b + 2 * GDN_HEADS
    w_uz = w0[:, o_u:]
    wba = jnp.pad(w0[:, o_b:o_u], ((0, 0), (0, 128 - 2 * GDN_HEADS)))
    pad_g = lambda v: jnp.pad(v, (GDN_HEADS, 128 - 2 * GDN_HEADS)).reshape(1, 128)

    q, k, v, za, bg, ug, zb = _proj0(x2, norm_pre[0].reshape(1, D_MODEL), w_in_even, w_uz, wba,
                                    conv_qkv[0], pad_g(a_log[0]), pad_g(dt_bias[0]))
    u_c, w_c, q_dec, k_dec, attn, e_last = _gdn_prep(q, k, v, bg)
    ya = _gdn_scan(u_c, w_c, q_dec, k_dec, attn, e_last, za, gdn_norm_w[0].reshape(1, GDN_DK))

    p = S5_STATE
    drow = jnp.tile(s5_d[0].reshape(S5_GROUPS, 1, S5_GROUP), (1, 1, S5_CHUNK))
    odd = (jnp.arange(S5_GROUPS) % 2 == 1).reshape(S5_GROUPS, 1, 1)

    def by_parity(a):
        z = jnp.zeros_like(a)
        return jnp.concatenate([jnp.where(odd, z, a), jnp.where(odd, a, z)], axis=-1)

    toep, m1r, m1i, m3r, m3i, a16r, a16i = _s5_prep(
        by_parity(s5_lam_re[0].reshape(S5_GROUPS, 1, p)),
        by_parity(s5_lam_im[0].reshape(S5_GROUPS, 1, p)),
        s5_log_dt[0].reshape(S5_GROUPS, 1, 1),
        by_parity(jnp.swapaxes(s5_b_re[0], 1, 2)), by_parity(jnp.swapaxes(s5_b_im[0], 1, 2)),
        by_parity(s5_c_re[0]), by_parity(s5_c_im[0]), drow)
    yg = _s5(ug, toep, m1r, m1i, m3r, m3i, a16r, a16i)

    x1 = _out0(x2, yg, zb, ya, w_glu[0], w_out_even[0], norm_post[0].reshape(1, D_MODEL))

    out = _layer1(x1, norm_pre[1].reshape(1, D_MODEL), w_in_odd[0], conv_short[0], w_out_odd[0],
                  norm_post[1].reshape(1, D_MODEL))
    return out.reshape(BATCH, SEQ, D_MODEL)
```

```python
import functools
import math

import jax
import jax.numpy as jnp
from jax import lax
from jax.experimental import pallas as pl
from jax.experimental.pallas import tpu as pltpu

F32 = jnp.float32
BF16 = jnp.bfloat16

D_MODEL = 1024
BATCH = 2
SEQ = 8192
TOKENS = BATCH * SEQ
RMS_EPS = 1e-6

GDN_HEADS = 8
GDN_DK = 128
GDN_CONV = 4
GDN_CHUNK = 64
GDN_QKV = 3 * GDN_HEADS * GDN_DK

S5_GROUP = 16
S5_GROUPS = D_MODEL // S5_GROUP
S5_STATE = 64
S5_MIN_NEG = 1e-4
S5_CHUNK = 16
S5_SEGS = 8
S5_CPS = SEQ // (S5_CHUNK * S5_SEGS)
S5_ROWS = BATCH * SEQ // S5_CHUNK
S5_W = S5_CHUNK * S5_GROUP

SC_WIDTH = 2 * D_MODEL
SC_CONV = 3

HALO = 8
CONV_DEC = 4
VMEM_LIMIT = 56 * 1024 * 1024

TM = 512


def _dot(a, b):
    return jnp.dot(a, b, preferred_element_type=F32)


def _dot_nt(a, b):
    return lax.dot_general(a, b, (((1,), (1,)), ((), ())), preferred_element_type=F32)


def _dot_tn(a, b):
    return lax.dot_general(a, b, (((0,), (0,)), ((), ())), preferred_element_type=F32)


def _split3(x):
    x1 = x.astype(BF16)
    r = x - x1.astype(F32)
    x2 = r.astype(BF16)
    x3 = (r - x2.astype(F32)).astype(BF16)
    return x1, x2, x3


def _dot3_nt(a, b):
    a1, a2, _ = _split3(a)
    b1, b2, _ = _split3(b)
    return _dot_nt(a1, b1) + _dot_nt(a1, b2) + _dot_nt(a2, b1)


STAGE_BYTES = 2 * 1024 * 1024


def _stage_rows(n_cols):
    return STAGE_BYTES // (4 * n_cols)


def _stage_scratch(n_cols):
    return [pltpu.VMEM((2, _stage_rows(n_cols), n_cols), F32), pltpu.SemaphoreType.DMA((2,))]


def _load_weight_bf16(src_hbm, dst_ref, stage_ref, sem):
    n_rows, n_cols = dst_ref.shape
    rows = stage_ref.shape[1]
    n = n_rows // rows

    def copy(i):
        return pltpu.make_async_copy(src_hbm.at[pl.ds(i * rows, rows), pl.ds(0, n_cols)],
                                     stage_ref.at[i % 2], sem.at[i % 2])

    copy(0).start()
    for i in range(n):
        if i + 1 < n:
            copy(i + 1).start()
        copy(i).wait()
        dst_ref[i * rows:(i + 1) * rows, :] = stage_ref[i % 2].astype(BF16)


def _sigmoid(x):
    return 1.0 / (1.0 + jnp.exp(-x))


def _silu(x):
    return x * _sigmoid(x)


def _rms_scale(x):
    return x * lax.rsqrt(jnp.mean(x * x, axis=-1, keepdims=True) + RMS_EPS)


S5_LANE_BLOCKS = 128 // S5_GROUP


def _block_transpose(ws):
    n = S5_LANE_BLOCKS
    blk = lax.broadcasted_iota(jnp.int32, ws[0].shape, 1) // S5_GROUP
    step = 1
    while step < n:
        upper = (blk & step) != 0
        new = list(ws)
        for a in range(n):
            if a & step:
                continue
            lo, hi = ws[a], ws[a + step]
            new[a] = jnp.where(upper, pltpu.roll(hi, step * S5_GROUP, 1), lo)
            new[a + step] = jnp.where(upper, hi, pltpu.roll(lo, 128 - step * S5_GROUP, 1))
        ws = new
        step *= 2
    return ws


def _proj0_kernel(x_ref, npre_ref, win_hbm, wuz_hbm, wba_ref, convw_ref, alog_ref, dtb_ref,
                  q_ref, k_ref, v_ref, za_ref, bg_ref, ug_ref, zb_ref,
                  ext_ref, mix_ref, us_ref, wqz_ref, wuz_ref, stage_a, sem_a, stage_b, sem_b):
    tiles_per_seq = SEQ // TM

    @pl.when(pl.program_id(0) == 0)
    def _():
        _load_weight_bf16(win_hbm.at[0], wqz_ref, stage_a, sem_a)
        _load_weight_bf16(wuz_hbm, wuz_ref, stage_b, sem_b)

    @pl.when(pl.program_id(0) % tiles_per_seq == 0)
    def _():
        ext_ref[:, 0:HALO, :] = jnp.zeros((GDN_QKV // 128, HALO, 128), F32)

    h = (_rms_scale(x_ref[...]) * npre_ref[...]).astype(BF16)

    gw = 256
    chunks = TM // S5_CHUNK
    for c in range(D_MODEL // gw):
        u = _dot(h, wuz_ref[:, c * gw:(c + 1) * gw])
        for jj in range(gw // 128):
            j = c * (gw // 128) + jj
            us_ref[j] = u[:, jj * 128:(jj + 1) * 128]
            for half in range(S5_CHUNK // S5_LANE_BLOCKS):
                ws = [us_ref[j, pl.ds(half * S5_LANE_BLOCKS + a, chunks, stride=S5_CHUNK), :]
                      for a in range(S5_LANE_BLOCKS)]
                ws = _block_transpose(ws)
                for g8 in range(S5_LANE_BLOCKS):
                    ug_ref[j * S5_LANE_BLOCKS + g8, :, half * 128:(half + 1) * 128] = (
                        ws[g8].astype(ug_ref.dtype))
    for c in range(D_MODEL // gw):
        cols = slice(c * gw, (c + 1) * gw)
        wza = wqz_ref[:, GDN_QKV + c * gw:GDN_QKV + (c + 1) * gw]
        wzb = wuz_ref[:, D_MODEL + c * gw:D_MODEL + (c + 1) * gw]
        za_ref[:, cols] = _silu(_dot(h, wza)).astype(za_ref.dtype)
        zb_ref[:, cols] = _silu(_dot(h, wzb)).astype(zb_ref.dtype)

    cw = 512
    slabs_per_chunk = cw // 128
    outs = (q_ref, k_ref, v_ref)
    for c in range(GDN_QKV // cw):
        raw = _dot(h, wqz_ref[:, c * cw:(c + 1) * cw])
        for sl in range(slabs_per_chunk):
            ext_ref[c * slabs_per_chunk + sl, HALO:HALO + TM, :] = raw[:, sl * 128:(sl + 1) * 128]
        which = (c * cw) // (GDN_HEADS * GDN_DK)
        out_ref = outs[which]
        for sl in range(slabs_per_chunk):
            slab = c * slabs_per_chunk + sl
            w = convw_ref[:, slab * 128:(slab + 1) * 128]
            for r in range(CONV_DEC):
                acc = None
                for j in range(GDN_CONV):
                    start = HALO + r - (GDN_CONV - 1) + j
                    term = ext_ref[slab, pl.ds(start, TM // CONV_DEC, stride=CONV_DEC), :] * w[j:j + 1]
                    acc = term if acc is None else acc + term
                sh = _silu(acc)
                if which < 2:
                    scale = lax.rsqrt(jnp.sum(sh * sh, axis=-1, keepdims=True) + RMS_EPS)
                    if which == 0:
                        scale = scale * (GDN_DK ** -0.5)
                    sh = sh * scale
                mix_ref[sl, pl.ds(r, TM // CONV_DEC, stride=CONV_DEC), :] = sh
            col0 = slab * 128 - which * GDN_HEADS * GDN_DK
            out_ref[:, col0:col0 + 128] = mix_ref[sl].astype(out_ref.dtype)
    ext_ref[:, 0:HALO, :] = ext_ref[:, TM:TM + HALO, :]

    ba = _dot(h, wba_ref[...].astype(BF16))
    beta = _sigmoid(ba)
    xs = ba + dtb_ref[...]
    softplus = jnp.maximum(xs, 0.0) + jnp.log(1.0 + jnp.exp(-jnp.abs(xs)))
    g = -jnp.exp(alog_ref[...]) * softplus
    lane = lax.broadcasted_iota(jnp.int32, ba.shape, 1)
    bg_ref[...] = jnp.where(lane < GDN_HEADS, beta, g)


def _const_spec(shape):
    nd = len(shape)
    return pl.BlockSpec(shape, lambda *_: (0,) * nd)


def _proj0(x2, npre, w_in, w_uz, wba, convw, alog, dtb):
    n_tiles = TOKENS // TM
    n_qz = GDN_QKV + GDN_HEADS * GDN_DK
    hbm = pl.BlockSpec(memory_space=pl.ANY)
    tok = lambda cols: pl.BlockSpec((TM, cols), lambda i: (i, 0))
    hd = GDN_HEADS * GDN_DK
    out_shape = (
        jax.ShapeDtypeStruct((TOKENS, hd), BF16),
        jax.ShapeDtypeStruct((TOKENS, hd), BF16),
        jax.ShapeDtypeStruct((TOKENS, hd), BF16),
        jax.ShapeDtypeStruct((TOKENS, hd), BF16),
        jax.ShapeDtypeStruct((TOKENS, 128), F32),
        jax.ShapeDtypeStruct((S5_GROUPS, S5_ROWS, S5_W), BF16),
        jax.ShapeDtypeStruct((TOKENS, D_MODEL), BF16),
    )
    ug_spec = pl.BlockSpec((S5_GROUPS, TM // S5_CHUNK, S5_W), lambda i: (0, i, 0))
    return pl.pallas_call(
        _proj0_kernel,
        grid=(n_tiles,),
        in_specs=[tok(D_MODEL), _const_spec((1, D_MODEL)), hbm, hbm, _const_spec(wba.shape),
                  _const_spec(convw.shape), _const_spec((1, 128)), _const_spec((1, 128))],
        out_specs=(tok(hd), tok(hd), tok(hd), tok(hd), tok(128), ug_spec, tok(D_MODEL)),
        out_shape=out_shape,
        scratch_shapes=[pltpu.VMEM((GDN_QKV // 128, TM + HALO, 128), F32),
                        pltpu.VMEM((4, TM, 128), F32),
                        pltpu.VMEM((D_MODEL // 128, TM, 128), F32),
                        pltpu.VMEM((D_MODEL, n_qz), BF16),
                        pltpu.VMEM((D_MODEL, 2 * D_MODEL), BF16)]
        + _stage_scratch(n_qz) + _stage_scratch(2 * D_MODEL),
        compiler_params=pltpu.CompilerParams(dimension_semantics=("arbitrary",),
                                             vmem_limit_bytes=VMEM_LIMIT),
        name="proj0",
    )(x2, npre, w_in, w_uz, wba, convw, alog, dtb)


GDN_PACK = 4
GDN_NGRP = GDN_HEADS // GDN_PACK
GDN_PW = GDN_PACK * GDN_CHUNK
GDN_GW = GDN_PACK * GDN_DK
GDN_PREP_TILE = 512


def _lane_bcast(x, lane, width):
    return jnp.broadcast_to(x[:, lane:lane + 1], (x.shape[0], width))


def _gdn_prep_kernel(q_ref, k_ref, v_ref, bg_ref,
                     u_ref, w_ref, qd_ref, kd_ref, attn_ref, el_ref):
    c = GDN_CHUNK
    pw = GDN_PW
    n_chunks = GDN_PREP_TILE // c

    row_p = lax.broadcasted_iota(jnp.int32, (c, pw), 0)
    lane_p = lax.broadcasted_iota(jnp.int32, (c, pw), 1)
    col_p = lane_p % c
    causal_p = row_p >= col_p
    strict_p = row_p > col_p
    eye_p = jnp.where(row_p == col_p, 1.0, 0.0).astype(F32)
    r2 = lax.broadcasted_iota(jnp.int32, (pw, pw), 0)
    c2 = lax.broadcasted_iota(jnp.int32, (pw, pw), 1)
    bd_mask = (r2 // c) == (c2 // c)
    r3 = lax.broadcasted_iota(jnp.int32, (pw, GDN_GW), 0)
    c3 = lax.broadcasted_iota(jnp.int32, (pw, GDN_GW), 1)
    kbd_mask = (r3 // c) == (c3 // GDN_DK)
    r1 = lax.broadcasted_iota(jnp.int32, (c, c), 0)
    c1 = lax.broadcasted_iota(jnp.int32, (c, c), 1)
    tri = jnp.where(r1 >= c1, 1.0, 0.0).astype(BF16)
    lane128 = lax.broadcasted_iota(jnp.int32, (c, 128), 1)
    low_half = lane128 < c

    bd_one = jnp.where(bd_mask, 1.0, 0.0).astype(BF16)
    kbd_one = jnp.where(kbd_mask, 1.0, 0.0).astype(BF16)

    def block_diag(x_b):
        return jnp.concatenate([x_b] * GDN_PACK, axis=0) * bd_one

    chunks = []
    for ci in range(n_chunks):
        rows = slice(ci * c, (ci + 1) * c)
        bg = bg_ref[rows, :]
        g_pad = jnp.where((lane128 >= GDN_HEADS) & (lane128 < 2 * GDN_HEADS), bg, 0.0)
        chunks.append(dict(ci=ci, rows=rows, bg=bg, g3=_split3(g_pad)))
    for ch in chunks:
        g1, g2, g3 = ch["g3"]
        gc = _dot(tri, g1) + _dot(tri, g2) + _dot(tri, g3)
        ch["gc"] = gc
        ch["gc_t"] = gc.T
        ch["e_gc"] = jnp.exp(gc)
        ch["e_rev"] = jnp.exp(gc[c - 1:c, :] - gc)

    ctxs = []
    for ch in chunks:
        rows, bg, gc = ch["rows"], ch["bg"], ch["gc"]
        for gh in range(GDN_NGRP):
            heads = range(gh * GDN_PACK, (gh + 1) * GDN_PACK)
            cols = slice(gh * GDN_GW, (gh + 1) * GDN_GW)
            k_b = k_ref[rows, cols]
            q_b = q_ref[rows, cols]
            wide = lambda src, off: jnp.concatenate(
                [_lane_bcast(src, off + hd, GDN_DK) for hd in heads], axis=1)
            beta_f = wide(bg, 0)
            egc_f = wide(ch["e_gc"], GDN_HEADS)
            erev_f = wide(ch["e_rev"], GDN_HEADS)
            kf = k_b.astype(F32)
            kb = kf * beta_f
            kbd = jnp.concatenate([k_b] * GDN_PACK, axis=0) * kbd_one
            kkqk = _dot_nt(jnp.concatenate([kb.astype(BF16), q_b], axis=0), kbd)

            bc = [_lane_bcast(gc, GDN_HEADS + hd, 128) for hd in heads]
            gcol = jnp.concatenate([jnp.where(low_half, bc[0], bc[1]),
                                    jnp.where(low_half, bc[2], bc[3])], axis=1)
            gt = ch["gc_t"]
            rows_t = [gt[GDN_HEADS + hd:GDN_HEADS + hd + 1, :] for hd in heads]
            grow = jnp.concatenate([jnp.concatenate(rows_t[0:2], axis=1),
                                    jnp.concatenate(rows_t[2:4], axis=1)], axis=1)
            decay = jnp.exp(jnp.minimum(gcol - grow, 0.0))
            a = jnp.where(strict_p, kkqk[:c] * decay, 0.0)
            attn = jnp.where(causal_p, kkqk[c:] * decay, 0.0)

            vb = (v_ref[rows, cols].astype(F32) * beta_f).astype(BF16)
            kbe = (kb * egc_f).astype(BF16)
            rhs = jnp.concatenate(
                [jnp.concatenate([vb[:, i * GDN_DK:(i + 1) * GDN_DK],
                                  kbe[:, i * GDN_DK:(i + 1) * GDN_DK]], axis=1)
                 for i in range(GDN_PACK)], axis=0)

            qd_ref[rows, cols] = (q_b.astype(F32) * egc_f).astype(qd_ref.dtype)
            kd_ref[rows, cols] = (kf * erev_f).astype(kd_ref.dtype)
            el_ref[ch["ci"], :, cols] = egc_f[c - 1:c, :]
            attn_ref[rows, gh * pw:(gh + 1) * pw] = attn.astype(attn_ref.dtype)
            ctxs.append(dict(rows=rows, gh=gh, t=eye_p - a, p=a, rhs=rhs))

    for cx in ctxs:
        p_b = cx["p"].astype(BF16)
        cx["p"] = _dot(p_b, block_diag(p_b))
    for lvl in range(1, 6):
        for cx in ctxs:
            p_b = cx["p"].astype(BF16)
            t_b = cx["t"].astype(BF16)
            if lvl < 5:
                res = _dot(jnp.concatenate([t_b, p_b], axis=0), block_diag(p_b))
                cx["t"] = cx["t"] + res[:c]
                cx["p"] = res[c:]
            else:
                cx["t"] = cx["t"] + _dot(t_b, block_diag(p_b))

    for cx in ctxs:
        sol = _dot(block_diag(cx["t"].astype(BF16)), cx["rhs"])
        for i in range(GDN_PACK):
            hd = cx["gh"] * GDN_PACK + i
            blk = sol[i * c:(i + 1) * c]
            u_ref[cx["rows"], hd * GDN_DK:(hd + 1) * GDN_DK] = blk[:, :GDN_DK].astype(u_ref.dtype)
            w_ref[cx["rows"], hd * GDN_DK:(hd + 1) * GDN_DK] = blk[:, GDN_DK:].astype(w_ref.dtype)


def _gdn_prep(q, k, v, bg):
    hd = GDN_HEADS * GDN_DK
    n_chunks = GDN_PREP_TILE // GDN_CHUNK
    tok = lambda cols: pl.BlockSpec((GDN_PREP_TILE, cols), lambda i: (i, 0))
    wide = jax.ShapeDtypeStruct((TOKENS, hd), BF16)
    return pl.pallas_call(
        _gdn_prep_kernel,
        grid=(TOKENS // GDN_PREP_TILE,),
        in_specs=[tok(hd), tok(hd), tok(hd), tok(128)],
        out_specs=(tok(hd), tok(hd), tok(hd), tok(hd), tok(GDN_HEADS * GDN_CHUNK),
                   pl.BlockSpec((n_chunks, 1, hd), lambda i: (i, 0, 0))),
        out_shape=(wide, wide, wide, wide,
                   jax.ShapeDtypeStruct((TOKENS, GDN_HEADS * GDN_CHUNK), BF16),
                   jax.ShapeDtypeStruct((TOKENS // GDN_CHUNK, 1, hd), F32)),
        compiler_params=pltpu.CompilerParams(dimension_semantics=("arbitrary",),
                                             vmem_limit_bytes=VMEM_LIMIT),
        name="gdn_prep",
    )(q, k, v, bg)


GDN_SCAN_TILE = 512


def _gdn_scan_kernel(u_ref, w_ref, qd_ref, kd_ref, attn_ref, el_ref, gnw_ref, o_ref, state_ref):
    c = GDN_CHUNK

    @pl.when(pl.program_id(0) == 0)
    def _():
        state_ref[...] = jnp.zeros(state_ref.shape, F32)

    gnw = gnw_ref[...]
    zblk = jnp.zeros((c, GDN_DK), BF16)
    chains = [(b, hd) for b in range(BATCH) for hd in range(GDN_HEADS)]
    head_cols = lambda hd: slice(hd * GDN_DK, (hd + 1) * GDN_DK)

    def chunk_body(ci, carry):
        rows = pl.ds(pl.multiple_of(ci * c, c), c)
        wqs = {}
        for b, hd in chains:
            cols = head_cols(hd)
            wq = jnp.concatenate([w_ref[b, rows, cols], qd_ref[b, rows, cols]], axis=0)
            wqs[b, hd] = _dot(wq, state_ref[b, hd].astype(BF16))
        for b in range(BATCH):
            for hp in range(GDN_HEADS // 2):
                pair = (2 * hp, 2 * hp + 1)
                v_b = [(u_ref[b, rows, head_cols(hd)].astype(F32) - wqs[b, hd][:c]).astype(BF16)
                       for hd in pair]
                v_bd = jnp.concatenate([jnp.concatenate([v_b[0], zblk], axis=1),
                                        jnp.concatenate([zblk, v_b[1]], axis=1)], axis=0)
                av = _dot(attn_ref[b, rows, hp * 2 * c:(hp + 1) * 2 * c], v_bd)
                for i, hd in enumerate(pair):
                    cols = head_cols(hd)
                    o = wqs[b, hd][c:] + av[:, i * GDN_DK:(i + 1) * GDN_DK]
                    e_last = el_ref[b, ci, :, cols]
                    state_ref[b, hd] = (state_ref[b, hd] * e_last
                                        + _dot_tn(kd_ref[b, rows, cols], v_b[i]))
                    o_ref[b, rows, cols] = (_rms_scale(o) * gnw).astype(o_ref.dtype)
        return carry

    lax.fori_loop(0, GDN_SCAN_TILE // c, chunk_body, 0, unroll=4)


def _gdn_scan(u, w, qd, kd, attn, el, gnw):
    hd = GDN_HEADS * GDN_DK
    aw = GDN_HEADS * GDN_CHUNK
    n_chunks = GDN_SCAN_TILE // GDN_CHUNK
    as3 = lambda a: a.reshape(BATCH, SEQ, a.shape[-1])
    tok = pl.BlockSpec((BATCH, GDN_SCAN_TILE, hd), lambda j: (0, j, 0))
    el4 = el.reshape(BATCH, SEQ // GDN_CHUNK, 1, hd)
    out = pl.pallas_call(
        _gdn_scan_kernel,
        grid=(SEQ // GDN_SCAN_TILE,),
        in_specs=[tok, tok, tok, tok, pl.BlockSpec((BATCH, GDN_SCAN_TILE, aw), lambda j: (0, j, 0)),
                  pl.BlockSpec((BATCH, n_chunks, 1, hd), lambda j: (0, j, 0, 0)),
                  _const_spec((1, GDN_DK))],
        out_specs=tok,
        out_shape=jax.ShapeDtypeStruct((BATCH, SEQ, hd), BF16),
        scratch_shapes=[pltpu.VMEM((BATCH, GDN_HEADS, GDN_DK, GDN_DK), F32)],
        compiler_params=pltpu.CompilerParams(dimension_semantics=("arbitrary",),
                                             vmem_limit_bytes=VMEM_LIMIT),
        name="gdn_scan",
    )(as3(u), as3(w), as3(qd), as3(kd), as3(attn), el4, gnw)
    return out.reshape(TOKENS, hd)


S5_PREP_GROUPS = 8
S5_PAIR_STATE = 2 * S5_STATE


def _s5_prep_kernel(lre_ref, lim_ref, ldt_ref, bre_ref, bim_ref, cre_ref, cim_ref, d_ref,
                    toep_ref, m1r_ref, m1i_ref, m3r_ref, m3i_ref, a16r_ref, a16i_ref):
    n = S5_CHUNK
    row = lax.broadcasted_iota(jnp.int32, (S5_W, S5_W), 0)
    col = lax.broadcasted_iota(jnp.int32, (S5_W, S5_W), 1)
    colblk = lax.broadcasted_iota(jnp.int32, (S5_W, 128), 1) // S5_GROUP
    for gi in range(S5_PREP_GROUPS):
        lr = jnp.minimum(lre_ref[gi], -S5_MIN_NEG)
        li = lim_ref[gi]
        dt = jnp.exp(ldt_ref[gi])
        mag = jnp.exp(lr * dt)
        ar = mag * jnp.cos(li * dt)
        ai = mag * jnp.sin(li * dt)
        den = lr * lr + li * li
        nr, ni = ar - 1.0, ai
        fr = (nr * lr + ni * li) / den
        fi = (ni * lr - nr * li) / den
        br, bi = bre_ref[gi], bim_ref[gi]
        bbr = fr * br - fi * bi
        bbi = fr * bi + fi * br
        pr = [jnp.ones_like(ar)]
        pi = [jnp.zeros_like(ar)]
        for _ in range(n):
            pr.append(pr[-1] * ar - pi[-1] * ai)
            pi.append(pr[-2] * ai + pi[-1] * ar)
        m1r = jnp.concatenate([pr[n - 1 - s] * bbr - pi[n - 1 - s] * bbi for s in range(n)], axis=0)
        m1i = jnp.concatenate([pr[n - 1 - s] * bbi + pi[n - 1 - s] * bbr for s in range(n)], axis=0)
        cr, ci = cre_ref[gi], cim_ref[gi]
        m3r = jnp.concatenate([cr * pr[t + 1] - ci * pi[t + 1] for t in range(n)], axis=0)
        m3i = jnp.concatenate([-(cr * pi[t + 1] + ci * pr[t + 1]) for t in range(n)], axis=0)
        cer = jnp.concatenate([cr] * n, axis=0)
        cei = jnp.concatenate([ci] * n, axis=0)
        rt = _dot3_nt(m1r, cer) - _dot3_nt(m1i, cei)
        halves = []
        for lt in range(S5_W // 128):
            rt_lt = rt[:, lt * 128:(lt + 1) * 128]
            part = jnp.zeros((S5_W, 128), F32)
            for tb in range(S5_LANE_BLOCKS):
                sh = (n - 1 - (lt * S5_LANE_BLOCKS + tb)) * S5_GROUP
                if sh:
                    shifted = jnp.concatenate([rt_lt[sh:], jnp.zeros((sh, 128), F32)], axis=0)
                else:
                    shifted = rt_lt
                part = jnp.where(colblk == tb, shifted, part)
            halves.append(part)
        toep = jnp.concatenate(halves, axis=1) + jnp.where(row == col, d_ref[gi], 0.0)
        toep_ref[gi] = toep.astype(toep_ref.dtype)
        m1r_ref[gi] = m1r.astype(m1r_ref.dtype)
        m1i_ref[gi] = m1i.astype(m1i_ref.dtype)
        m3r_ref[gi] = m3r.astype(m3r_ref.dtype)
        m3i_ref[gi] = m3i.astype(m3i_ref.dtype)
        own = (lax.broadcasted_iota(jnp.int32, ar.shape, 1) // S5_STATE) == (gi % 2)
        a16r_ref[gi] = jnp.where(own, pr[n], 0.0)
        a16i_ref[gi] = jnp.where(own, pi[n], 0.0)


def _s5_prep(lre, lim, ldt, bre_t, bim_t, cre, cim, drow):
    gp = S5_PREP_GROUPS
    blk = lambda *tail: pl.BlockSpec((gp,) + tail, lambda i: (i,) + (0,) * len(tail))
    p = S5_PAIR_STATE
    out_shape = (
        jax.ShapeDtypeStruct((S5_GROUPS, S5_W, S5_W), BF16),
        jax.ShapeDtypeStruct((S5_GROUPS, S5_W, p), BF16),
        jax.ShapeDtypeStruct((S5_GROUPS, S5_W, p), BF16),
        jax.ShapeDtypeStruct((S5_GROUPS, S5_W, p), BF16),
        jax.ShapeDtypeStruct((S5_GROUPS, S5_W, p), BF16),
        jax.ShapeDtypeStruct((S5_GROUPS, 1, p), F32),
        jax.ShapeDtypeStruct((S5_GROUPS, 1, p), F32),
    )
    return pl.pallas_call(
        _s5_prep_kernel,
        grid=(S5_GROUPS // gp,),
        in_specs=[blk(1, p), blk(1, p), blk(1, 1), blk(S5_GROUP, p), blk(S5_GROUP, p),
                  blk(S5_GROUP, p), blk(S5_GROUP, p), blk(1, S5_W)],
        out_specs=(blk(S5_W, S5_W), blk(S5_W, p), blk(S5_W, p), blk(S5_W, p), blk(S5_W, p),
                   blk(1, p), blk(1, p)),
        out_shape=out_shape,
        compiler_params=pltpu.CompilerParams(dimension_semantics=("arbitrary",),
                                             vmem_limit_bytes=VMEM_LIMIT),
        name="s5_prep",
    )(lre, lim, ldt, bre_t, bim_t, cre, cim, drow)


S5_PAIRS = 2
S5_PITCH = S5_CPS + 8


def _s5_kernel(u_ref, toep_ref, m1r_ref, m1i_ref, m3r_ref, m3i_ref, a16r_ref, a16i_ref,
               y_ref, zr_ref, zi_ref, xr_ref, xi_ref, pr_ref, pi_ref):
    p = S5_PAIR_STATE
    segs = S5_SEGS
    n_blk = BATCH * segs
    blk_rows = lambda blk: slice(blk * S5_PITCH, blk * S5_PITCH + S5_CPS)
    ar, ai = [], []
    for pp in range(S5_PAIRS):
        g0, g1 = 2 * pp, 2 * pp + 1
        m1 = lambda g: jnp.concatenate([m1r_ref[g], m1i_ref[g]], axis=1)
        z = _dot(u_ref[g0], m1(g0)) + _dot(u_ref[g1], m1(g1))
        zr, zi = z[:, :p], z[:, p:]
        for blk in range(n_blk):
            zr_ref[pp, blk_rows(blk), :] = zr[blk * S5_CPS:(blk + 1) * S5_CPS]
            zi_ref[pp, blk_rows(blk), :] = zi[blk * S5_CPS:(blk + 1) * S5_CPS]
        ar.append(jnp.broadcast_to(a16r_ref[g0] + a16r_ref[g1], (segs, p)))
        ai.append(jnp.broadcast_to(a16i_ref[g0] + a16i_ref[g1], (segs, p)))

    def scan_body(i, carry):
        new = []
        for pp in range(S5_PAIRS):
            pwr, pwi, st = carry[pp]
            pr_ref[pp, pl.ds(i, 1), :] = pwr[0:1]
            pi_ref[pp, pl.ds(i, 1), :] = pwi[0:1]
            new_st = []
            for b in range(BATCH):
                sr, si = st[b]
                rb = pl.ds(b * segs * S5_PITCH + i, segs, stride=S5_PITCH)
                xr_ref[pp, rb, :] = sr
                xi_ref[pp, rb, :] = si
                new_st.append((sr * ar[pp] - si * ai[pp] + zr_ref[pp, rb, :],
                               sr * ai[pp] + si * ar[pp] + zi_ref[pp, rb, :]))
            new.append((pwr * ar[pp] - pwi * ai[pp], pwr * ai[pp] + pwi * ar[pp], tuple(new_st)))
        return tuple(new)

    zero = jnp.zeros((segs, p), F32)
    init = tuple((jnp.ones((segs, p), F32), zero, tuple((zero, zero) for _ in range(BATCH)))
                 for _ in range(S5_PAIRS))
    final = lax.fori_loop(0, S5_CPS, scan_body, init)

    sub = lax.broadcasted_iota(jnp.int32, (segs, p), 0)
    for pp in range(S5_PAIRS):
        pwr, pwi, st = final[pp]
        pw_r = pr_ref[pp]
        pw_i = pi_ref[pp]
        xr_parts, xi_parts = [], []
        for b in range(BATCH):
            fr, fi = st[b]
            gr, gi = zero, zero
            for _ in range(segs - 1):
                nr = pwr * gr - pwi * gi + fr
                ni = pwr * gi + pwi * gr + fi
                gr = jnp.where(sub == 0, 0.0, pltpu.roll(nr, 1, 0))
                gi = jnp.where(sub == 0, 0.0, pltpu.roll(ni, 1, 0))
            for s in range(segs):
                rows = blk_rows(b * segs + s)
                g_r, g_i = gr[s:s + 1], gi[s:s + 1]
                xr_parts.append(xr_ref[pp, rows, :] + pw_r * g_r - pw_i * g_i)
                xi_parts.append(xi_ref[pp, rows, :] + pw_r * g_i + pw_i * g_r)
        x_b = jnp.concatenate([jnp.concatenate(xr_parts, axis=0), jnp.concatenate(xi_parts, axis=0)],
                              axis=1).astype(BF16)
        for g in (2 * pp, 2 * pp + 1):
            m3 = jnp.concatenate([m3r_ref[g], m3i_ref[g]], axis=1)
            y = _dot(u_ref[g], toep_ref[g]) + _dot_nt(x_b, m3)
            y_ref[g] = y.astype(y_ref.dtype)


def _s5(ug, toep, m1r, m1i, m3r, m3i, a16r, a16i):
    gb = 2 * S5_PAIRS
    p = S5_PAIR_STATE
    blk = lambda *tail: pl.BlockSpec((gb,) + tail, lambda i: (i,) + (0,) * len(tail))
    seg_rows = BATCH * S5_SEGS * S5_PITCH
    return pl.pallas_call(
        _s5_kernel,
        grid=(S5_GROUPS // gb,),
        in_specs=[blk(S5_ROWS, S5_W), blk(S5_W, S5_W), blk(S5_W, p), blk(S5_W, p), blk(S5_W, p),
                  blk(S5_W, p), blk(1, p), blk(1, p)],
        out_specs=blk(S5_ROWS, S5_W),
        out_shape=jax.ShapeDtypeStruct((S5_GROUPS, S5_ROWS, S5_W), BF16),
        scratch_shapes=[pltpu.VMEM((S5_PAIRS, seg_rows, p), F32) for _ in range(4)]
        + [pltpu.VMEM((S5_PAIRS, S5_CPS, p), F32) for _ in range(2)],
        compiler_params=pltpu.CompilerParams(dimension_semantics=("arbitrary",),
                                             vmem_limit_bytes=VMEM_LIMIT),
        name="s5",
    )(ug, toep, m1r, m1i, m3r, m3i, a16r, a16i)


def _gelu_tanh(x):
    return 0.5 * x * (1.0 + jnp.tanh(math.sqrt(2.0 / math.pi) * (x + 0.044715 * (x * x * x))))


OUT0_PARTS = 2


def _out0_kernel(x_ref, yg_ref, zb_ref, oa_ref, za_ref, wglu_hbm, wout_hbm, npost_ref, o_ref,
                 ys_ref, wglu_ref, wout_ref, stage_ref, sem):
    @pl.when(pl.program_id(0) == 0)
    def _():
        _load_weight_bf16(wglu_hbm, wglu_ref, stage_ref, sem)
        _load_weight_bf16(wout_hbm, wout_ref, stage_ref, sem)

    hd = GDN_HEADS * GDN_DK
    part_rows = TM // OUT0_PARTS
    chunks = part_rows // S5_CHUNK
    for part in range(OUT0_PARTS):
        r0 = part * part_rows
        rows = slice(r0, r0 + part_rows)
        crows = slice(part * chunks, (part + 1) * chunks)
        for j in range(D_MODEL // 128):
            for half in range(S5_CHUNK // S5_LANE_BLOCKS):
                ws = [yg_ref[j * S5_LANE_BLOCKS + g8, crows, half * 128:(half + 1) * 128].astype(F32)
                      for g8 in range(S5_LANE_BLOCKS)]
                ws = _block_transpose(ws)
                for a in range(S5_LANE_BLOCKS):
                    ys_ref[j, pl.ds(r0 + half * S5_LANE_BLOCKS + a, chunks, stride=S5_CHUNK), :] = ws[a]
        y = _gelu_tanh(jnp.concatenate([ys_ref[j, rows, :] for j in range(D_MODEL // 128)], axis=1))
        y = y * _sigmoid(_dot(y.astype(BF16), wglu_ref[...]))
        yb = (y * zb_ref[rows, :].astype(F32)).astype(BF16)
        ya = oa_ref[rows, :] * za_ref[rows, :]
        mix = _dot(ya, wout_ref[0:hd, :]) + _dot(yb, wout_ref[hd:, :])
        o_ref[rows, :] = x_ref[rows, :] + _rms_scale(mix) * npost_ref[...]


def _out0(x2, yg, zb, oa, za, wglu, wout, npost):
    tok = lambda cols: pl.BlockSpec((TM, cols), lambda i: (i, 0))
    yg_spec = pl.BlockSpec((S5_GROUPS, TM // S5_CHUNK, S5_W), lambda i: (0, i, 0))
    hbm = pl.BlockSpec(memory_space=pl.ANY)
    return pl.pallas_call(
        _out0_kernel,
        grid=(TOKENS // TM,),
        in_specs=[tok(D_MODEL), yg_spec, tok(D_MODEL), tok(D_MODEL), tok(D_MODEL), hbm, hbm,
                  _const_spec((1, D_MODEL))],
        out_specs=tok(D_MODEL),
        out_shape=jax.ShapeDtypeStruct((TOKENS, D_MODEL), F32),
        scratch_shapes=[pltpu.VMEM((D_MODEL // 128, TM, 128), F32),
                        pltpu.VMEM(wglu.shape, BF16), pltpu.VMEM(wout.shape, BF16)]
        + _stage_scratch(D_MODEL),
        compiler_params=pltpu.CompilerParams(dimension_semantics=("arbitrary",),
                                             vmem_limit_bytes=VMEM_LIMIT),
        name="out0",
    )(x2, yg, zb, oa, za, wglu, wout, npost)


def _layer1_kernel(x_ref, npre_ref, win_hbm, convw_ref, wout_hbm, npost_ref, o_ref,
                   ext_ref, win_ref, wout_ref, stage_a, sem_a, stage_b, sem_b):
    tiles_per_seq = SEQ // TM

    @pl.when(pl.program_id(0) == 0)
    def _():
        _load_weight_bf16(win_hbm, win_ref, stage_a, sem_a)
        _load_weight_bf16(wout_hbm, wout_ref, stage_b, sem_b)

    @pl.when(pl.program_id(0) % tiles_per_seq == 0)
    def _():
        ext_ref[0:HALO, :] = jnp.zeros((HALO, SC_WIDTH), F32)

    x = x_ref[...]
    h = (_rms_scale(x) * npre_ref[...]).astype(BF16)
    cw = 512
    acc = jnp.zeros((TM, D_MODEL), F32)
    for c in range(SC_WIDTH // cw):
        cols = slice(c * cw, (c + 1) * cw)
        part = lambda k: _dot(h, win_ref[:, k * SC_WIDTH + c * cw:k * SC_WIDTH + (c + 1) * cw])
        prod = part(1) * part(2)
        ext_ref[HALO:HALO + TM, cols] = prod
        w = convw_ref[:, cols]
        conv = prod * w[2:3]
        for j in range(SC_CONV - 1):
            shift = SC_CONV - 1 - j
            conv = conv + ext_ref[HALO - shift:HALO - shift + TM, cols] * w[j:j + 1]
        y = part(0) * conv * _silu(part(3))
        acc = acc + _dot(y.astype(BF16), wout_ref[cols, :])
    ext_ref[0:HALO, :] = ext_ref[TM:TM + HALO, :]
    o_ref[...] = x + _rms_scale(acc) * npost_ref[...]


def _layer1(x1, npre, win, convw, wout, npost):
    tok = lambda cols: pl.BlockSpec((TM, cols), lambda i: (i, 0))
    hbm = pl.BlockSpec(memory_space=pl.ANY)
    return pl.pallas_call(
        _layer1_kernel,
        grid=(TOKENS // TM,),
        in_specs=[tok(D_MODEL), _const_spec((1, D_MODEL)), hbm, _const_spec(convw.shape), hbm,
                  _const_spec((1, D_MODEL))],
        out_specs=tok(D_MODEL),
        out_shape=jax.ShapeDtypeStruct((TOKENS, D_MODEL), F32),
        scratch_shapes=[pltpu.VMEM((TM + HALO, SC_WIDTH), F32),
                        pltpu.VMEM(win.shape, BF16), pltpu.VMEM(wout.shape, BF16)]
        + _stage_scratch(win.shape[1]) + _stage_scratch(wout.shape[1]),
        compiler_params=pltpu.CompilerParams(dimension_semantics=("arbitrary",),
                                             vmem_limit_bytes=VMEM_LIMIT),
        name="layer1",
    )(x1, npre, win, convw, wout, npost)


def kernel(x, norm_pre, norm_post, w_in_even, conv_qkv, a_log, dt_bias, gdn_norm_w,
           s5_lam_re, s5_lam_im, s5_b_re, s5_b_im, s5_c_re, s5_c_im, s5_log_dt, s5_d,
           w_glu, w_out_even, w_in_odd, conv_short, w_out_odd):
    x2 = x.reshape(TOKENS, D_MODEL)
    hd = GDN_HEADS * GDN_DK

    w0 = w_in_even[0]
    o_b = GDN_QKV + hd
    o_u = o_b + 2 * GDN_HEADS
    w_uz = w0[:, o_u:]
    wba = jnp.pad(w0[:, o_b:o_u], ((0, 0), (0, 128 - 2 * GDN_HEADS)))
    pad_g = lambda v: jnp.pad(v, (GDN_HEADS, 128 - 2 * GDN_HEADS)).reshape(1, 128)

    q, k, v, za, bg, ug, zb = _proj0(x2, norm_pre[0].reshape(1, D_MODEL), w_in_even, w_uz, wba,
                                    conv_qkv[0], pad_g(a_log[0]), pad_g(dt_bias[0]))
    u_c, w_c, q_dec, k_dec, attn, e_last = _gdn_prep(q, k, v, bg)
    oa = _gdn_scan(u_c, w_c, q_dec, k_dec, attn, e_last, gdn_norm_w[0].reshape(1, GDN_DK))

    p = S5_STATE
    drow = jnp.tile(s5_d[0].reshape(S5_GROUPS, 1, S5_GROUP), (1, 1, S5_CHUNK))
    odd = (jnp.arange(S5_GROUPS) % 2 == 1).reshape(S5_GROUPS, 1, 1)

    def by_parity(a):
        z = jnp.zeros_like(a)
        return jnp.concatenate([jnp.where(odd, z, a), jnp.where(odd, a, z)], axis=-1)

    toep, m1r, m1i, m3r, m3i, a16r, a16i = _s5_prep(
        by_parity(s5_lam_re[0].reshape(S5_GROUPS, 1, p)),
        by_parity(s5_lam_im[0].reshape(S5_GROUPS, 1, p)),
        s5_log_dt[0].reshape(S5_GROUPS, 1, 1),
        by_parity(jnp.swapaxes(s5_b_re[0], 1, 2)), by_parity(jnp.swapaxes(s5_b_im[0], 1, 2)),
        by_parity(s5_c_re[0]), by_parity(s5_c_im[0]), drow)
    yg = _s5(ug, toep, m1r, m1i, m3r, m3i, a16r, a16i)

    x1 = _out0(x2, yg, zb, oa, za, w_glu[0], w_out_even[0], norm_post[0].reshape(1, D_MODEL))

    out = _layer1(x1, norm_pre[1].reshape(1, D_MODEL), w_in_odd[0], conv_short[0], w_out_odd[0],
                  norm_post[1].reshape(1, D_MODEL))
    return out.reshape(BATCH, SEQ, D_MODEL)
```

```python
import functools
import math

import jax
import jax.numpy as jnp
from jax import lax
from jax.experimental import pallas as pl
from jax.experimental.pallas import tpu as pltpu

F32 = jnp.float32
BF16 = jnp.bfloat16

D_MODEL = 1024
BATCH = 2
SEQ = 8192
TOKENS = BATCH * SEQ
RMS_EPS = 1e-6

GDN_HEADS = 8
GDN_DK = 128
GDN_CONV = 4
GDN_CHUNK = 64
GDN_QKV = 3 * GDN_HEADS * GDN_DK

S5_GROUP = 16
S5_GROUPS = D_MODEL // S5_GROUP
S5_STATE = 64
S5_MIN_NEG = 1e-4
S5_CHUNK = 16
S5_SEGS = 8
S5_CPS = SEQ // (S5_CHUNK * S5_SEGS)
S5_ROWS = BATCH * SEQ // S5_CHUNK
S5_W = S5_CHUNK * S5_GROUP

SC_WIDTH = 2 * D_MODEL
SC_CONV = 3

HALO = 8
CONV_DEC = 4
VMEM_LIMIT = 56 * 1024 * 1024

TM = 512


def _dot(a, b):
    return jnp.dot(a, b, preferred_element_type=F32)


def _dot_nt(a, b):
    return lax.dot_general(a, b, (((1,), (1,)), ((), ())), preferred_element_type=F32)


def _dot_tn(a, b):
    return lax.dot_general(a, b, (((0,), (0,)), ((), ())), preferred_element_type=F32)


def _split3(x):
    x1 = x.astype(BF16)
    r = x - x1.astype(F32)
    x2 = r.astype(BF16)
    x3 = (r - x2.astype(F32)).astype(BF16)
    return x1, x2, x3


def _dot3_nt(a, b):
    a1, a2, _ = _split3(a)
    b1, b2, _ = _split3(b)
    return _dot_nt(a1, b1) + _dot_nt(a1, b2) + _dot_nt(a2, b1)


STAGE_BYTES = 2 * 1024 * 1024


def _stage_rows(n_cols):
    return STAGE_BYTES // (4 * n_cols)


def _stage_scratch(n_cols):
    return [pltpu.VMEM((2, _stage_rows(n_cols), n_cols), F32), pltpu.SemaphoreType.DMA((2,))]


def _load_weight_bf16(src_hbm, dst_ref, stage_ref, sem, src_row0=0, n_rows=None):
    n_rows = dst_ref.shape[0] if n_rows is None else n_rows
    n_cols = dst_ref.shape[1]
    rows = min(stage_ref.shape[1], n_rows)
    n = n_rows // rows

    def copy(i):
        return pltpu.make_async_copy(
            src_hbm.at[pl.ds(src_row0 + i * rows, rows), pl.ds(0, n_cols)],
            stage_ref.at[i % 2, pl.ds(0, rows)], sem.at[i % 2])

    copy(0).start()
    for i in range(n):
        if i + 1 < n:
            copy(i + 1).start()
        copy(i).wait()
        dst_ref[i * rows:(i + 1) * rows, :] = stage_ref[i % 2, 0:rows].astype(BF16)


def _sigmoid(x):
    return 1.0 / (1.0 + jnp.exp(-x))


def _silu(x):
    return x * _sigmoid(x)


def _rms_scale(x):
    return x * lax.rsqrt(jnp.mean(x * x, axis=-1, keepdims=True) + RMS_EPS)


S5_LANE_BLOCKS = 128 // S5_GROUP


def _block_transpose(ws):
    n = S5_LANE_BLOCKS
    blk = lax.broadcasted_iota(jnp.int32, ws[0].shape, 1) // S5_GROUP
    step = 1
    while step < n:
        upper = (blk & step) != 0
        new = list(ws)
        for a in range(n):
            if a & step:
                continue
            lo, hi = ws[a], ws[a + step]
            new[a] = jnp.where(upper, pltpu.roll(hi, step * S5_GROUP, 1), lo)
            new[a + step] = jnp.where(upper, hi, pltpu.roll(lo, 128 - step * S5_GROUP, 1))
        ws = new
        step *= 2
    return ws


def _proj0_kernel(x_ref, npre_ref, wt_hbm, convw_ref, alog_ref, dtb_ref,
                  q_ref, k_ref, v_ref, za_ref, bg_ref, ug_ref, zb_ref,
                  ext_ref, mix_ref, us_ref, wqz_ref, wuz_ref, wba_ref, stage_ref, sem):
    tiles_per_seq = SEQ // TM

    @pl.when(pl.program_id(0) == 0)
    def _():
        o_b = GDN_QKV + GDN_HEADS * GDN_DK
        o_u = o_b + 2 * GDN_HEADS
        _load_weight_bf16(wt_hbm, wqz_ref, stage_ref, sem)
        _load_weight_bf16(wt_hbm, wuz_ref, stage_ref, sem, src_row0=o_u)
        wba_ref[...] = jnp.zeros(wba_ref.shape, BF16)
        _load_weight_bf16(wt_hbm, wba_ref, stage_ref, sem, src_row0=o_b, n_rows=2 * GDN_HEADS)

    @pl.when(pl.program_id(0) % tiles_per_seq == 0)
    def _():
        ext_ref[:, 0:HALO, :] = jnp.zeros((GDN_QKV // 128, HALO, 128), F32)

    h = (_rms_scale(x_ref[...]) * npre_ref[...]).astype(BF16)

    gw = 256
    chunks = TM // S5_CHUNK
    for c in range(D_MODEL // gw):
        u = _dot_nt(h, wuz_ref[c * gw:(c + 1) * gw, :])
        for jj in range(gw // 128):
            j = c * (gw // 128) + jj
            us_ref[j] = u[:, jj * 128:(jj + 1) * 128]
            for half in range(S5_CHUNK // S5_LANE_BLOCKS):
                ws = [us_ref[j, pl.ds(half * S5_LANE_BLOCKS + a, chunks, stride=S5_CHUNK), :]
                      for a in range(S5_LANE_BLOCKS)]
                ws = _block_transpose(ws)
                for g8 in range(S5_LANE_BLOCKS):
                    ug_ref[j * S5_LANE_BLOCKS + g8, :, half * 128:(half + 1) * 128] = (
                        ws[g8].astype(ug_ref.dtype))
    for c in range(D_MODEL // gw):
        cols = slice(c * gw, (c + 1) * gw)
        wza = wqz_ref[GDN_QKV + c * gw:GDN_QKV + (c + 1) * gw, :]
        wzb = wuz_ref[D_MODEL + c * gw:D_MODEL + (c + 1) * gw, :]
        za_ref[:, cols] = _silu(_dot_nt(h, wza)).astype(za_ref.dtype)
        zb_ref[:, cols] = _silu(_dot_nt(h, wzb)).astype(zb_ref.dtype)

    cw = 512
    slabs_per_chunk = cw // 128
    outs = (q_ref, k_ref, v_ref)
    for c in range(GDN_QKV // cw):
        raw = _dot_nt(h, wqz_ref[c * cw:(c + 1) * cw, :])
        for sl in range(slabs_per_chunk):
            ext_ref[c * slabs_per_chunk + sl, HALO:HALO + TM, :] = raw[:, sl * 128:(sl + 1) * 128]
        which = (c * cw) // (GDN_HEADS * GDN_DK)
        out_ref = outs[which]
        for sl in range(slabs_per_chunk):
            slab = c * slabs_per_chunk + sl
            w = convw_ref[:, slab * 128:(slab + 1) * 128]
            for r in range(CONV_DEC):
                acc = None
                for j in range(GDN_CONV):
                    start = HALO + r - (GDN_CONV - 1) + j
                    term = ext_ref[slab, pl.ds(start, TM // CONV_DEC, stride=CONV_DEC), :] * w[j:j + 1]
                    acc = term if acc is None else acc + term
                sh = _silu(acc)
                if which < 2:
                    scale = lax.rsqrt(jnp.sum(sh * sh, axis=-1, keepdims=True) + RMS_EPS)
                    if which == 0:
                        scale = scale * (GDN_DK ** -0.5)
                    sh = sh * scale
                mix_ref[sl, pl.ds(r, TM // CONV_DEC, stride=CONV_DEC), :] = sh
            col0 = slab * 128 - which * GDN_HEADS * GDN_DK
            out_ref[:, col0:col0 + 128] = mix_ref[sl].astype(out_ref.dtype)
    ext_ref[:, 0:HALO, :] = ext_ref[:, TM:TM + HALO, :]

    ba = _dot_nt(h, wba_ref[...])
    beta = _sigmoid(ba)
    xs = ba + dtb_ref[...]
    softplus = jnp.maximum(xs, 0.0) + jnp.log(1.0 + jnp.exp(-jnp.abs(xs)))
    g = -jnp.exp(alog_ref[...]) * softplus
    lane = lax.broadcasted_iota(jnp.int32, ba.shape, 1)
    bg_ref[...] = jnp.where(lane < GDN_HEADS, beta, g)


def _const_spec(shape):
    nd = len(shape)
    return pl.BlockSpec(shape, lambda *_: (0,) * nd)


def _proj0(x2, npre, w_t, convw, alog, dtb):
    n_tiles = TOKENS // TM
    n_qz = GDN_QKV + GDN_HEADS * GDN_DK
    hbm = pl.BlockSpec(memory_space=pl.ANY)
    tok = lambda cols: pl.BlockSpec((TM, cols), lambda i: (i, 0))
    hd = GDN_HEADS * GDN_DK
    out_shape = (
        jax.ShapeDtypeStruct((TOKENS, hd), BF16),
        jax.ShapeDtypeStruct((TOKENS, hd), BF16),
        jax.ShapeDtypeStruct((TOKENS, hd), BF16),
        jax.ShapeDtypeStruct((TOKENS, hd), BF16),
        jax.ShapeDtypeStruct((TOKENS, 128), F32),
        jax.ShapeDtypeStruct((S5_GROUPS, S5_ROWS, S5_W), BF16),
        jax.ShapeDtypeStruct((TOKENS, D_MODEL), BF16),
    )
    ug_spec = pl.BlockSpec((S5_GROUPS, TM // S5_CHUNK, S5_W), lambda i: (0, i, 0))
    return pl.pallas_call(
        _proj0_kernel,
        grid=(n_tiles,),
        in_specs=[tok(D_MODEL), _const_spec((1, D_MODEL)), hbm,
                  _const_spec(convw.shape), _const_spec((1, 128)), _const_spec((1, 128))],
        out_specs=(tok(hd), tok(hd), tok(hd), tok(hd), tok(128), ug_spec, tok(D_MODEL)),
        out_shape=out_shape,
        scratch_shapes=[pltpu.VMEM((GDN_QKV // 128, TM + HALO, 128), F32),
                        pltpu.VMEM((4, TM, 128), F32),
                        pltpu.VMEM((D_MODEL // 128, TM, 128), F32),
                        pltpu.VMEM((n_qz, D_MODEL), BF16),
                        pltpu.VMEM((2 * D_MODEL, D_MODEL), BF16),
                        pltpu.VMEM((128, D_MODEL), BF16)]
        + _stage_scratch(D_MODEL),
        compiler_params=pltpu.CompilerParams(dimension_semantics=("arbitrary",),
                                             vmem_limit_bytes=VMEM_LIMIT),
        name="proj0",
    )(x2, npre, w_t, convw, alog, dtb)


GDN_PACK = 4
GDN_NGRP = GDN_HEADS // GDN_PACK
GDN_PW = GDN_PACK * GDN_CHUNK
GDN_GW = GDN_PACK * GDN_DK
GDN_PREP_TILE = 512


def _lane_bcast(x, lane, width):
    return jnp.broadcast_to(x[:, lane:lane + 1], (x.shape[0], width))


def _gdn_prep_kernel(q_ref, k_ref, v_ref, bg_ref,
                     u_ref, w_ref, qd_ref, kd_ref, attn_ref, el_ref):
    c = GDN_CHUNK
    pw = GDN_PW
    n_chunks = GDN_PREP_TILE // c

    row_p = lax.broadcasted_iota(jnp.int32, (c, pw), 0)
    lane_p = lax.broadcasted_iota(jnp.int32, (c, pw), 1)
    col_p = lane_p % c
    causal_p = row_p >= col_p
    strict_p = row_p > col_p
    eye_p = jnp.where(row_p == col_p, 1.0, 0.0).astype(F32)
    r2 = lax.broadcasted_iota(jnp.int32, (pw, pw), 0)
    c2 = lax.broadcasted_iota(jnp.int32, (pw, pw), 1)
    bd_mask = (r2 // c) == (c2 // c)
    r3 = lax.broadcasted_iota(jnp.int32, (pw, GDN_GW), 0)
    c3 = lax.broadcasted_iota(jnp.int32, (pw, GDN_GW), 1)
    kbd_mask = (r3 // c) == (c3 // GDN_DK)
    r1 = lax.broadcasted_iota(jnp.int32, (c, c), 0)
    c1 = lax.broadcasted_iota(jnp.int32, (c, c), 1)
    tri = jnp.where(r1 >= c1, 1.0, 0.0).astype(BF16)
    lane128 = lax.broadcasted_iota(jnp.int32, (c, 128), 1)
    low_half = lane128 < c

    bd_one = jnp.where(bd_mask, 1.0, 0.0).astype(BF16)
    kbd_one = jnp.where(kbd_mask, 1.0, 0.0).astype(BF16)

    def block_diag(x_b):
        return jnp.concatenate([x_b] * GDN_PACK, axis=0) * bd_one

    chunks = []
    for ci in range(n_chunks):
        rows = slice(ci * c, (ci + 1) * c)
        bg = bg_ref[rows, :]
        g_pad = jnp.where((lane128 >= GDN_HEADS) & (lane128 < 2 * GDN_HEADS), bg, 0.0)
        chunks.append(dict(ci=ci, rows=rows, bg=bg, g3=_split3(g_pad)))
    for ch in chunks:
        g1, g2, g3 = ch["g3"]
        gc = _dot(tri, g1) + _dot(tri, g2) + _dot(tri, g3)
        ch["gc"] = gc
        ch["gc_t"] = gc.T
        ch["e_gc"] = jnp.exp(gc)
        ch["e_rev"] = jnp.exp(gc[c - 1:c, :] - gc)

    ctxs = []
    for ch in chunks:
        rows, bg, gc = ch["rows"], ch["bg"], ch["gc"]
        for gh in range(GDN_NGRP):
            heads = range(gh * GDN_PACK, (gh + 1) * GDN_PACK)
            cols = slice(gh * GDN_GW, (gh + 1) * GDN_GW)
            k_b = k_ref[rows, cols]
            q_b = q_ref[rows, cols]
            wide = lambda src, off: jnp.concatenate(
                [_lane_bcast(src, off + hd, GDN_DK) for hd in heads], axis=1)
            beta_f = wide(bg, 0)
            egc_f = wide(ch["e_gc"], GDN_HEADS)
            erev_f = wide(ch["e_rev"], GDN_HEADS)
            kf = k_b.astype(F32)
            kb = kf * beta_f
            kbd = jnp.concatenate([k_b] * GDN_PACK, axis=0) * kbd_one
            kkqk = _dot_nt(jnp.concatenate([kb.astype(BF16), q_b], axis=0), kbd)

            bc = [_lane_bcast(gc, GDN_HEADS + hd, 128) for hd in heads]
            gcol = jnp.concatenate([jnp.where(low_half, bc[0], bc[1]),
                                    jnp.where(low_half, bc[2], bc[3])], axis=1)
            gt = ch["gc_t"]
            rows_t = [gt[GDN_HEADS + hd:GDN_HEADS + hd + 1, :] for hd in heads]
            grow = jnp.concatenate([jnp.concatenate(rows_t[0:2], axis=1),
                                    jnp.concatenate(rows_t[2:4], axis=1)], axis=1)
            decay = jnp.exp(jnp.minimum(gcol - grow, 0.0))
            a = jnp.where(strict_p, kkqk[:c] * decay, 0.0)
            attn = jnp.where(causal_p, kkqk[c:] * decay, 0.0)

            vb = (v_ref[rows, cols].astype(F32) * beta_f).astype(BF16)
            kbe = (kb * egc_f).astype(BF16)
            rhs = jnp.concatenate(
                [jnp.concatenate([vb[:, i * GDN_DK:(i + 1) * GDN_DK],
                                  kbe[:, i * GDN_DK:(i + 1) * GDN_DK]], axis=1)
                 for i in range(GDN_PACK)], axis=0)

            qd_ref[rows, cols] = (q_b.astype(F32) * egc_f).astype(qd_ref.dtype)
            kd_ref[rows, cols] = (kf * erev_f).astype(kd_ref.dtype)
            el_ref[ch["ci"], :, cols] = egc_f[c - 1:c, :]
            attn_ref[rows, gh * pw:(gh + 1) * pw] = attn.astype(attn_ref.dtype)
            ctxs.append(dict(rows=rows, gh=gh, t=eye_p - a, p=a, rhs=rhs))

    for cx in ctxs:
        p_b = cx["p"].astype(BF16)
        cx["p"] = _dot(p_b, block_diag(p_b))
    for lvl in range(1, 6):
        for cx in ctxs:
            p_b = cx["p"].astype(BF16)
            t_b = cx["t"].astype(BF16)
            if lvl < 5:
                res = _dot(jnp.concatenate([t_b, p_b], axis=0), block_diag(p_b))
                cx["t"] = cx["t"] + res[:c]
                cx["p"] = res[c:]
            else:
                cx["t"] = cx["t"] + _dot(t_b, block_diag(p_b))

    for cx in ctxs:
        sol = _dot(block_diag(cx["t"].astype(BF16)), cx["rhs"])
        for i in range(GDN_PACK):
            hd = cx["gh"] * GDN_PACK + i
            blk = sol[i * c:(i + 1) * c]
            u_ref[cx["rows"], hd * GDN_DK:(hd + 1) * GDN_DK] = blk[:, :GDN_DK].astype(u_ref.dtype)
            w_ref[cx["rows"], hd * GDN_DK:(hd + 1) * GDN_DK] = blk[:, GDN_DK:].astype(w_ref.dtype)


def _gdn_prep(q, k, v, bg):
    hd = GDN_HEADS * GDN_DK
    n_chunks = GDN_PREP_TILE // GDN_CHUNK
    tok = lambda cols: pl.BlockSpec((GDN_PREP_TILE, cols), lambda i: (i, 0))
    wide = jax.ShapeDtypeStruct((TOKENS, hd), BF16)
    return pl.pallas_call(
        _gdn_prep_kernel,
        grid=(TOKENS // GDN_PREP_TILE,),
        in_specs=[tok(hd), tok(hd), tok(hd), tok(128)],
        out_specs=(tok(hd), tok(hd), tok(hd), tok(hd), tok(GDN_HEADS * GDN_CHUNK),
                   pl.BlockSpec((n_chunks, 1, hd), lambda i: (i, 0, 0))),
        out_shape=(wide, wide, wide, wide,
                   jax.ShapeDtypeStruct((TOKENS, GDN_HEADS * GDN_CHUNK), BF16),
                   jax.ShapeDtypeStruct((TOKENS // GDN_CHUNK, 1, hd), F32)),
        compiler_params=pltpu.CompilerParams(dimension_semantics=("arbitrary",),
                                             vmem_limit_bytes=VMEM_LIMIT),
        name="gdn_prep",
    )(q, k, v, bg)


GDN_SCAN_TILE = 512


def _gdn_scan_kernel(u_ref, w_ref, qd_ref, kd_ref, attn_ref, el_ref, gnw_ref, o_ref, state_ref):
    c = GDN_CHUNK

    @pl.when(pl.program_id(0) == 0)
    def _():
        state_ref[...] = jnp.zeros(state_ref.shape, F32)

    gnw = gnw_ref[...]
    zblk = jnp.zeros((c, GDN_DK), BF16)
    chains = [(b, hd) for b in range(BATCH) for hd in range(GDN_HEADS)]
    head_cols = lambda hd: slice(hd * GDN_DK, (hd + 1) * GDN_DK)

    def chunk_body(ci, carry):
        rows = pl.ds(pl.multiple_of(ci * c, c), c)
        wqs = {}
        for b, hd in chains:
            cols = head_cols(hd)
            wq = jnp.concatenate([w_ref[b, rows, cols], qd_ref[b, rows, cols]], axis=0)
            wqs[b, hd] = _dot(wq, state_ref[b, hd].astype(BF16))
        for b in range(BATCH):
            for hp in range(GDN_HEADS // 2):
                pair = (2 * hp, 2 * hp + 1)
                v_b = [(u_ref[b, rows, head_cols(hd)].astype(F32) - wqs[b, hd][:c]).astype(BF16)
                       for hd in pair]
                v_bd = jnp.concatenate([jnp.concatenate([v_b[0], zblk], axis=1),
                                        jnp.concatenate([zblk, v_b[1]], axis=1)], axis=0)
                av = _dot(attn_ref[b, rows, hp * 2 * c:(hp + 1) * 2 * c], v_bd)
                for i, hd in enumerate(pair):
                    cols = head_cols(hd)
                    o = wqs[b, hd][c:] + av[:, i * GDN_DK:(i + 1) * GDN_DK]
                    e_last = el_ref[b, ci, :, cols]
                    state_ref[b, hd] = (state_ref[b, hd] * e_last
                                        + _dot_tn(kd_ref[b, rows, cols], v_b[i]))
                    o_ref[b, rows, cols] = (_rms_scale(o) * gnw).astype(o_ref.dtype)
        return carry

    lax.fori_loop(0, GDN_SCAN_TILE // c, chunk_body, 0, unroll=4)


def _gdn_scan(u, w, qd, kd, attn, el, gnw):
    hd = GDN_HEADS * GDN_DK
    aw = GDN_HEADS * GDN_CHUNK
    n_chunks = GDN_SCAN_TILE // GDN_CHUNK
    as3 = lambda a: a.reshape(BATCH, SEQ, a.shape[-1])
    tok = pl.BlockSpec((BATCH, GDN_SCAN_TILE, hd), lambda j: (0, j, 0))
    el4 = el.reshape(BATCH, SEQ // GDN_CHUNK, 1, hd)
    out = pl.pallas_call(
        _gdn_scan_kernel,
        grid=(SEQ // GDN_SCAN_TILE,),
        in_specs=[tok, tok, tok, tok, pl.BlockSpec((BATCH, GDN_SCAN_TILE, aw), lambda j: (0, j, 0)),
                  pl.BlockSpec((BATCH, n_chunks, 1, hd), lambda j: (0, j, 0, 0)),
                  _const_spec((1, GDN_DK))],
        out_specs=tok,
        out_shape=jax.ShapeDtypeStruct((BATCH, SEQ, hd), BF16),
        scratch_shapes=[pltpu.VMEM((BATCH, GDN_HEADS, GDN_DK, GDN_DK), F32)],
        compiler_params=pltpu.CompilerParams(dimension_semantics=("arbitrary",),
                                             vmem_limit_bytes=VMEM_LIMIT),
        name="gdn_scan",
    )(as3(u), as3(w), as3(qd), as3(kd), as3(attn), el4, gnw)
    return out.reshape(TOKENS, hd)


S5_PREP_GROUPS = 8
S5_PAIR_STATE = 2 * S5_STATE


def _s5_prep_kernel(lre_ref, lim_ref, ldt_ref, bre_ref, bim_ref, cre_ref, cim_ref, d_ref,
                    toep_ref, m1r_ref, m1i_ref, m3r_ref, m3i_ref, a16r_ref, a16i_ref):
    n = S5_CHUNK
    row = lax.broadcasted_iota(jnp.int32, (S5_W, S5_W), 0)
    col = lax.broadcasted_iota(jnp.int32, (S5_W, S5_W), 1)
    colblk = lax.broadcasted_iota(jnp.int32, (S5_W, 128), 1) // S5_GROUP
    for gi in range(S5_PREP_GROUPS):
        lr = jnp.minimum(lre_ref[gi], -S5_MIN_NEG)
        li = lim_ref[gi]
        dt = jnp.exp(ldt_ref[gi])
        mag = jnp.exp(lr * dt)
        ar = mag * jnp.cos(li * dt)
        ai = mag * jnp.sin(li * dt)
        den = lr * lr + li * li
        nr, ni = ar - 1.0, ai
        fr = (nr * lr + ni * li) / den
        fi = (ni * lr - nr * li) / den
        br, bi = bre_ref[gi], bim_ref[gi]
        bbr = fr * br - fi * bi
        bbi = fr * bi + fi * br
        pr = [jnp.ones_like(ar)]
        pi = [jnp.zeros_like(ar)]
        for _ in range(n):
            pr.append(pr[-1] * ar - pi[-1] * ai)
            pi.append(pr[-2] * ai + pi[-1] * ar)
        m1r = jnp.concatenate([pr[n - 1 - s] * bbr - pi[n - 1 - s] * bbi for s in range(n)], axis=0)
        m1i = jnp.concatenate([pr[n - 1 - s] * bbi + pi[n - 1 - s] * bbr for s in range(n)], axis=0)
        cr, ci = cre_ref[gi], cim_ref[gi]
        m3r = jnp.concatenate([cr * pr[t + 1] - ci * pi[t + 1] for t in range(n)], axis=0)
        m3i = jnp.concatenate([-(cr * pi[t + 1] + ci * pr[t + 1]) for t in range(n)], axis=0)
        cer = jnp.concatenate([cr] * n, axis=0)
        cei = jnp.concatenate([ci] * n, axis=0)
        rt = _dot3_nt(m1r, cer) - _dot3_nt(m1i, cei)
        halves = []
        for lt in range(S5_W // 128):
            rt_lt = rt[:, lt * 128:(lt + 1) * 128]
            part = jnp.zeros((S5_W, 128), F32)
            for tb in range(S5_LANE_BLOCKS):
                sh = (n - 1 - (lt * S5_LANE_BLOCKS + tb)) * S5_GROUP
                if sh:
                    shifted = jnp.concatenate([rt_lt[sh:], jnp.zeros((sh, 128), F32)], axis=0)
                else:
                    shifted = rt_lt
                part = jnp.where(colblk == tb, shifted, part)
            halves.append(part)
        toep = jnp.concatenate(halves, axis=1) + jnp.where(row == col, d_ref[gi], 0.0)
        toep_ref[gi] = toep.astype(toep_ref.dtype)
        m1r_ref[gi] = m1r.astype(m1r_ref.dtype)
        m1i_ref[gi] = m1i.astype(m1i_ref.dtype)
        m3r_ref[gi] = m3r.astype(m3r_ref.dtype)
        m3i_ref[gi] = m3i.astype(m3i_ref.dtype)
        own = (lax.broadcasted_iota(jnp.int32, ar.shape, 1) // S5_STATE) == (gi % 2)
        a16r_ref[gi] = jnp.where(own, pr[n], 0.0)
        a16i_ref[gi] = jnp.where(own, pi[n], 0.0)


def _s5_prep(lre, lim, ldt, bre_t, bim_t, cre, cim, drow):
    gp = S5_PREP_GROUPS
    blk = lambda *tail: pl.BlockSpec((gp,) + tail, lambda i: (i,) + (0,) * len(tail))
    p = S5_PAIR_STATE
    out_shape = (
        jax.ShapeDtypeStruct((S5_GROUPS, S5_W, S5_W), BF16),
        jax.ShapeDtypeStruct((S5_GROUPS, S5_W, p), BF16),
        jax.ShapeDtypeStruct((S5_GROUPS, S5_W, p), BF16),
        jax.ShapeDtypeStruct((S5_GROUPS, S5_W, p), BF16),
        jax.ShapeDtypeStruct((S5_GROUPS, S5_W, p), BF16),
        jax.ShapeDtypeStruct((S5_GROUPS, 1, p), F32),
        jax.ShapeDtypeStruct((S5_GROUPS, 1, p), F32),
    )
    return pl.pallas_call(
        _s5_prep_kernel,
        grid=(S5_GROUPS // gp,),
        in_specs=[blk(1, p), blk(1, p), blk(1, 1), blk(S5_GROUP, p), blk(S5_GROUP, p),
                  blk(S5_GROUP, p), blk(S5_GROUP, p), blk(1, S5_W)],
        out_specs=(blk(S5_W, S5_W), blk(S5_W, p), blk(S5_W, p), blk(S5_W, p), blk(S5_W, p),
                   blk(1, p), blk(1, p)),
        out_shape=out_shape,
        compiler_params=pltpu.CompilerParams(dimension_semantics=("arbitrary",),
                                             vmem_limit_bytes=VMEM_LIMIT),
        name="s5_prep",
    )(lre, lim, ldt, bre_t, bim_t, cre, cim, drow)


S5_PAIRS = 4
S5_PITCH = S5_CPS + 8


def _s5_kernel(u_ref, toep_ref, m1r_ref, m1i_ref, m3r_ref, m3i_ref, a16r_ref, a16i_ref,
               y_ref, zr_ref, zi_ref, xr_ref, xi_ref, pr_ref, pi_ref):
    p = S5_PAIR_STATE
    segs = S5_SEGS
    n_blk = BATCH * segs
    blk_rows = lambda blk: slice(blk * S5_PITCH, blk * S5_PITCH + S5_CPS)
    ar, ai = [], []
    for pp in range(S5_PAIRS):
        g0, g1 = 2 * pp, 2 * pp + 1
        m1 = lambda g: jnp.concatenate([m1r_ref[g], m1i_ref[g]], axis=1)
        z = _dot(u_ref[g0], m1(g0)) + _dot(u_ref[g1], m1(g1))
        zr, zi = z[:, :p], z[:, p:]
        for blk in range(n_blk):
            zr_ref[pp, blk_rows(blk), :] = zr[blk * S5_CPS:(blk + 1) * S5_CPS]
            zi_ref[pp, blk_rows(blk), :] = zi[blk * S5_CPS:(blk + 1) * S5_CPS]
        ar.append(jnp.broadcast_to(a16r_ref[g0] + a16r_ref[g1], (segs, p)))
        ai.append(jnp.broadcast_to(a16i_ref[g0] + a16i_ref[g1], (segs, p)))

    def scan_body(i, carry):
        new = []
        for pp in range(S5_PAIRS):
            pwr, pwi, st = carry[pp]
            pr_ref[pp, pl.ds(i, 1), :] = pwr[0:1]
            pi_ref[pp, pl.ds(i, 1), :] = pwi[0:1]
            new_st = []
            for b in range(BATCH):
                sr, si = st[b]
                rb = pl.ds(b * segs * S5_PITCH + i, segs, stride=S5_PITCH)
                xr_ref[pp, rb, :] = sr
                xi_ref[pp, rb, :] = si
                new_st.append((sr * ar[pp] - si * ai[pp] + zr_ref[pp, rb, :],
                               sr * ai[pp] + si * ar[pp] + zi_ref[pp, rb, :]))
            new.append((pwr * ar[pp] - pwi * ai[pp], pwr * ai[pp] + pwi * ar[pp], tuple(new_st)))
        return tuple(new)

    zero = jnp.zeros((segs, p), F32)
    init = tuple((jnp.ones((segs, p), F32), zero, tuple((zero, zero) for _ in range(BATCH)))
                 for _ in range(S5_PAIRS))
    final = lax.fori_loop(0, S5_CPS, scan_body, init)

    sub = lax.broadcasted_iota(jnp.int32, (segs, p), 0)
    for pp in range(S5_PAIRS):
        pwr, pwi, st = final[pp]
        pw_r = pr_ref[pp]
        pw_i = pi_ref[pp]
        xr_parts, xi_parts = [], []
        for b in range(BATCH):
            fr, fi = st[b]
            gr, gi = zero, zero
            for _ in range(segs - 1):
                nr = pwr * gr - pwi * gi + fr
                ni = pwr * gi + pwi * gr + fi
                gr = jnp.where(sub == 0, 0.0, pltpu.roll(nr, 1, 0))
                gi = jnp.where(sub == 0, 0.0, pltpu.roll(ni, 1, 0))
            for s in range(segs):
                rows = blk_rows(b * segs + s)
                g_r, g_i = gr[s:s + 1], gi[s:s + 1]
                xr_parts.append(xr_ref[pp, rows, :] + pw_r * g_r - pw_i * g_i)
                xi_parts.append(xi_ref[pp, rows, :] + pw_r * g_i + pw_i * g_r)
        x_b = jnp.concatenate([jnp.concatenate(xr_parts, axis=0), jnp.concatenate(xi_parts, axis=0)],
                              axis=1).astype(BF16)
        for g in (2 * pp, 2 * pp + 1):
            m3 = jnp.concatenate([m3r_ref[g], m3i_ref[g]], axis=1)
            y = _dot(u_ref[g], toep_ref[g]) + _dot_nt(x_b, m3)
            y_ref[g] = y.astype(y_ref.dtype)


def _s5(ug, toep, m1r, m1i, m3r, m3i, a16r, a16i):
    gb = 2 * S5_PAIRS
    p = S5_PAIR_STATE
    blk = lambda *tail: pl.BlockSpec((gb,) + tail, lambda i: (i,) + (0,) * len(tail))
    seg_rows = BATCH * S5_SEGS * S5_PITCH
    return pl.pallas_call(
        _s5_kernel,
        grid=(S5_GROUPS // gb,),
        in_specs=[blk(S5_ROWS, S5_W), blk(S5_W, S5_W), blk(S5_W, p), blk(S5_W, p), blk(S5_W, p),
                  blk(S5_W, p), blk(1, p), blk(1, p)],
        out_specs=blk(S5_ROWS, S5_W),
        out_shape=jax.ShapeDtypeStruct((S5_GROUPS, S5_ROWS, S5_W), BF16),
        scratch_shapes=[pltpu.VMEM((S5_PAIRS, seg_rows, p), F32) for _ in range(4)]
        + [pltpu.VMEM((S5_PAIRS, S5_CPS, p), F32) for _ in range(2)],
        compiler_params=pltpu.CompilerParams(dimension_semantics=("arbitrary",),
                                             vmem_limit_bytes=VMEM_LIMIT),
        name="s5",
    )(ug, toep, m1r, m1i, m3r, m3i, a16r, a16i)


def _gelu_tanh(x):
    return 0.5 * x * (1.0 + jnp.tanh(math.sqrt(2.0 / math.pi) * (x + 0.044715 * (x * x * x))))


OUT0_PARTS = 2


def _out0_kernel(x_ref, yg_ref, zb_ref, oa_ref, za_ref, wglu_hbm, wout_hbm, npost_ref, o_ref,
                 ys_ref, wglu_ref, wout_ref, stage_ref, sem):
    @pl.when(pl.program_id(0) == 0)
    def _():
        _load_weight_bf16(wglu_hbm, wglu_ref, stage_ref, sem)
        _load_weight_bf16(wout_hbm, wout_ref, stage_ref, sem)

    hd = GDN_HEADS * GDN_DK
    part_rows = TM // OUT0_PARTS
    chunks = part_rows // S5_CHUNK
    for part in range(OUT0_PARTS):
        r0 = part * part_rows
        rows = slice(r0, r0 + part_rows)
        crows = slice(part * chunks, (part + 1) * chunks)
        for j in range(D_MODEL // 128):
            for half in range(S5_CHUNK // S5_LANE_BLOCKS):
                ws = [yg_ref[j * S5_LANE_BLOCKS + g8, crows, half * 128:(half + 1) * 128].astype(F32)
                      for g8 in range(S5_LANE_BLOCKS)]
                ws = _block_transpose(ws)
                for a in range(S5_LANE_BLOCKS):
                    ys_ref[j, pl.ds(r0 + half * S5_LANE_BLOCKS + a, chunks, stride=S5_CHUNK), :] = ws[a]
        y = _gelu_tanh(jnp.concatenate([ys_ref[j, rows, :] for j in range(D_MODEL // 128)], axis=1))
        y = y * _sigmoid(_dot(y.astype(BF16), wglu_ref[...]))
        yb = (y * zb_ref[rows, :].astype(F32)).astype(BF16)
        ya = oa_ref[rows, :] * za_ref[rows, :]
        mix = _dot(ya, wout_ref[0:hd, :]) + _dot(yb, wout_ref[hd:, :])
        o_ref[rows, :] = x_ref[rows, :] + _rms_scale(mix) * npost_ref[...]


def _out0(x2, yg, zb, oa, za, wglu, wout, npost):
    tok = lambda cols: pl.BlockSpec((TM, cols), lambda i: (i, 0))
    yg_spec = pl.BlockSpec((S5_GROUPS, TM // S5_CHUNK, S5_W), lambda i: (0, i, 0))
    hbm = pl.BlockSpec(memory_space=pl.ANY)
    return pl.pallas_call(
        _out0_kernel,
        grid=(TOKENS // TM,),
        in_specs=[tok(D_MODEL), yg_spec, tok(D_MODEL), tok(D_MODEL), tok(D_MODEL), hbm, hbm,
                  _const_spec((1, D_MODEL))],
        out_specs=tok(D_MODEL),
        out_shape=jax.ShapeDtypeStruct((TOKENS, D_MODEL), F32),
        scratch_shapes=[pltpu.VMEM((D_MODEL // 128, TM, 128), F32),
                        pltpu.VMEM(wglu.shape, BF16), pltpu.VMEM(wout.shape, BF16)]
        + _stage_scratch(D_MODEL),
        compiler_params=pltpu.CompilerParams(dimension_semantics=("arbitrary",),
                                             vmem_limit_bytes=VMEM_LIMIT),
        name="out0",
    )(x2, yg, zb, oa, za, wglu, wout, npost)


def _layer1_kernel(x_ref, npre_ref, win_hbm, convw_ref, wout_hbm, npost_ref, o_ref,
                   ext_ref, win_ref, wout_ref, stage_a, sem_a, stage_b, sem_b):
    tiles_per_seq = SEQ // TM

    @pl.when(pl.program_id(0) == 0)
    def _():
        _load_weight_bf16(win_hbm, win_ref, stage_a, sem_a)
        _load_weight_bf16(wout_hbm, wout_ref, stage_b, sem_b)

    @pl.when(pl.program_id(0) % tiles_per_seq == 0)
    def _():
        ext_ref[0:HALO, :] = jnp.zeros((HALO, SC_WIDTH), F32)

    x = x_ref[...]
    h = (_rms_scale(x) * npre_ref[...]).astype(BF16)
    cw = 512
    acc = jnp.zeros((TM, D_MODEL), F32)
    for c in range(SC_WIDTH // cw):
        cols = slice(c * cw, (c + 1) * cw)
        part = lambda k: _dot(h, win_ref[:, k * SC_WIDTH + c * cw:k * SC_WIDTH + (c + 1) * cw])
        prod = part(1) * part(2)
        ext_ref[HALO:HALO + TM, cols] = prod
        w = convw_ref[:, cols]
        conv = prod * w[2:3]
        for j in range(SC_CONV - 1):
            shift = SC_CONV - 1 - j
            conv = conv + ext_ref[HALO - shift:HALO - shift + TM, cols] * w[j:j + 1]
        y = part(0) * conv * _silu(part(3))
        acc = acc + _dot(y.astype(BF16), wout_ref[cols, :])
    ext_ref[0:HALO, :] = ext_ref[TM:TM + HALO, :]
    o_ref[...] = x + _rms_scale(acc) * npost_ref[...]


def _layer1(x1, npre, win, convw, wout, npost):
    tok = lambda cols: pl.BlockSpec((TM, cols), lambda i: (i, 0))
    hbm = pl.BlockSpec(memory_space=pl.ANY)
    return pl.pallas_call(
        _layer1_kernel,
        grid=(TOKENS // TM,),
        in_specs=[tok(D_MODEL), _const_spec((1, D_MODEL)), hbm, _const_spec(convw.shape), hbm,
                  _const_spec((1, D_MODEL))],
        out_specs=tok(D_MODEL),
        out_shape=jax.ShapeDtypeStruct((TOKENS, D_MODEL), F32),
        scratch_shapes=[pltpu.VMEM((TM + HALO, SC_WIDTH), F32),
                        pltpu.VMEM(win.shape, BF16), pltpu.VMEM(wout.shape, BF16)]
        + _stage_scratch(win.shape[1]) + _stage_scratch(wout.shape[1]),
        compiler_params=pltpu.CompilerParams(dimension_semantics=("arbitrary",),
                                             vmem_limit_bytes=VMEM_LIMIT),
        name="layer1",
    )(x1, npre, win, convw, wout, npost)


def kernel(x, norm_pre, norm_post, w_in_even, conv_qkv, a_log, dt_bias, gdn_norm_w,
           s5_lam_re, s5_lam_im, s5_b_re, s5_b_im, s5_c_re, s5_c_im, s5_log_dt, s5_d,
           w_glu, w_out_even, w_in_odd, conv_short, w_out_odd):
    x2 = x.reshape(TOKENS, D_MODEL)
    hd = GDN_HEADS * GDN_DK

    w_t = jnp.swapaxes(w_in_even[0], 0, 1)
    pad_g = lambda v: jnp.pad(v, (GDN_HEADS, 128 - 2 * GDN_HEADS)).reshape(1, 128)

    q, k, v, za, bg, ug, zb = _proj0(x2, norm_pre[0].reshape(1, D_MODEL), w_t,
                                    conv_qkv[0], pad_g(a_log[0]), pad_g(dt_bias[0]))
    u_c, w_c, q_dec, k_dec, attn, e_last = _gdn_prep(q, k, v, bg)
    oa = _gdn_scan(u_c, w_c, q_dec, k_dec, attn, e_last, gdn_norm_w[0].reshape(1, GDN_DK))

    p = S5_STATE
    drow = jnp.tile(s5_d[0].reshape(S5_GROUPS, 1, S5_GROUP), (1, 1, S5_CHUNK))
    odd = (jnp.arange(S5_GROUPS) % 2 == 1).reshape(S5_GROUPS, 1, 1)

    def by_parity(a):
        z = jnp.zeros_like(a)
        return jnp.concatenate([jnp.where(odd, z, a), jnp.where(odd, a, z)], axis=-1)

    toep, m1r, m1i, m3r, m3i, a16r, a16i = _s5_prep(
        by_parity(s5_lam_re[0].reshape(S5_GROUPS, 1, p)),
        by_parity(s5_lam_im[0].reshape(S5_GROUPS, 1, p)),
        s5_log_dt[0].reshape(S5_GROUPS, 1, 1),
        by_parity(jnp.swapaxes(s5_b_re[0], 1, 2)), by_parity(jnp.swapaxes(s5_b_im[0], 1, 2)),
        by_parity(s5_c_re[0]), by_parity(s5_c_im[0]), drow)
    yg = _s5(ug, toep, m1r, m1i, m3r, m3i, a16r, a16i)

    x1 = _out0(x2, yg, zb, oa, za, w_glu[0], w_out_even[0], norm_post[0].reshape(1, D_MODEL))

    out = _layer1(x1, norm_pre[1].reshape(1, D_MODEL), w_in_odd[0], conv_short[0], w_out_odd[0],
                  norm_post[1].reshape(1, D_MODEL))
    return out.reshape(BATCH, SEQ, D_MODEL)
```

```python
import functools
import math

import jax
import jax.numpy as jnp
from jax import lax
from jax.experimental import pallas as pl
from jax.experimental.pallas import tpu as pltpu

F32 = jnp.float32
BF16 = jnp.bfloat16

D_MODEL = 1024
BATCH = 2
SEQ = 8192
TOKENS = BATCH * SEQ
RMS_EPS = 1e-6

GDN_HEADS = 8
GDN_DK = 128
GDN_CONV = 4
GDN_CHUNK = 64
GDN_QKV = 3 * GDN_HEADS * GDN_DK

S5_GROUP = 16
S5_GROUPS = D_MODEL // S5_GROUP
S5_STATE = 64
S5_MIN_NEG = 1e-4
S5_CHUNK = 16
S5_SEGS = 8
S5_CPS = SEQ // (S5_CHUNK * S5_SEGS)
S5_ROWS = BATCH * SEQ // S5_CHUNK
S5_W = S5_CHUNK * S5_GROUP

SC_WIDTH = 2 * D_MODEL
SC_CONV = 3

HALO = 8
CONV_DEC = 4
VMEM_LIMIT = 56 * 1024 * 1024

TM = 512


def _dot(a, b):
    return jnp.dot(a, b, preferred_element_type=F32)


def _dot_nt(a, b):
    return lax.dot_general(a, b, (((1,), (1,)), ((), ())), preferred_element_type=F32)


def _dot_tn(a, b):
    return lax.dot_general(a, b, (((0,), (0,)), ((), ())), preferred_element_type=F32)


def _split3(x):
    x1 = x.astype(BF16)
    r = x - x1.astype(F32)
    x2 = r.astype(BF16)
    x3 = (r - x2.astype(F32)).astype(BF16)
    return x1, x2, x3


def _dot3_nt(a, b):
    a1, a2, _ = _split3(a)
    b1, b2, _ = _split3(b)
    return _dot_nt(a1, b1) + _dot_nt(a1, b2) + _dot_nt(a2, b1)


STAGE_BYTES = 2 * 1024 * 1024


def _stage_rows(n_cols):
    return STAGE_BYTES // (4 * n_cols)


def _stage_scratch(n_cols):
    return [pltpu.VMEM((2, _stage_rows(n_cols), n_cols), F32), pltpu.SemaphoreType.DMA((2,))]


def _load_weight_bf16(src_hbm, dst_ref, stage_ref, sem, src_row0=0, n_rows=None):
    n_rows = dst_ref.shape[0] if n_rows is None else n_rows
    n_cols = dst_ref.shape[1]
    rows = min(stage_ref.shape[1], n_rows)
    n = n_rows // rows

    def copy(i):
        return pltpu.make_async_copy(
            src_hbm.at[pl.ds(src_row0 + i * rows, rows), pl.ds(0, n_cols)],
            stage_ref.at[i % 2, pl.ds(0, rows)], sem.at[i % 2])

    copy(0).start()
    for i in range(n):
        if i + 1 < n:
            copy(i + 1).start()
        copy(i).wait()
        dst_ref[i * rows:(i + 1) * rows, :] = stage_ref[i % 2, 0:rows].astype(BF16)


def _sigmoid(x):
    return 1.0 / (1.0 + jnp.exp(-x))


def _silu(x):
    return x * _sigmoid(x)


def _rms_scale(x):
    return x * lax.rsqrt(jnp.mean(x * x, axis=-1, keepdims=True) + RMS_EPS)


S5_LANE_BLOCKS = 128 // S5_GROUP


def _block_transpose(ws):
    n = S5_LANE_BLOCKS
    blk = lax.broadcasted_iota(jnp.int32, ws[0].shape, 1) // S5_GROUP
    step = 1
    while step < n:
        upper = (blk & step) != 0
        new = list(ws)
        for a in range(n):
            if a & step:
                continue
            lo, hi = ws[a], ws[a + step]
            new[a] = jnp.where(upper, pltpu.roll(hi, step * S5_GROUP, 1), lo)
            new[a + step] = jnp.where(upper, hi, pltpu.roll(lo, 128 - step * S5_GROUP, 1))
        ws = new
        step *= 2
    return ws


def _proj0_kernel(x_ref, npre_ref, wt_hbm, convw_ref, alog_ref, dtb_ref,
                  q_ref, k_ref, v_ref, za_ref, bg_ref, ug_ref, zb_ref,
                  ext_ref, mix_ref, us_ref, wqz_ref, wuz_ref, wba_ref, stage_ref, sem):
    tiles_per_seq = SEQ // TM

    @pl.when(pl.program_id(0) == 0)
    def _():
        o_b = GDN_QKV + GDN_HEADS * GDN_DK
        o_u = o_b + 2 * GDN_HEADS
        _load_weight_bf16(wt_hbm, wqz_ref, stage_ref, sem)
        _load_weight_bf16(wt_hbm, wuz_ref, stage_ref, sem, src_row0=o_u)
        wba_ref[...] = jnp.zeros(wba_ref.shape, BF16)
        _load_weight_bf16(wt_hbm, wba_ref, stage_ref, sem, src_row0=o_b, n_rows=2 * GDN_HEADS)

    @pl.when(pl.program_id(0) % tiles_per_seq == 0)
    def _():
        ext_ref[:, 0:HALO, :] = jnp.zeros((GDN_QKV // 128, HALO, 128), F32)

    h = (_rms_scale(x_ref[...]) * npre_ref[...]).astype(BF16)

    gw = 256
    chunks = TM // S5_CHUNK
    for c in range(D_MODEL // gw):
        u = _dot_nt(h, wuz_ref[c * gw:(c + 1) * gw, :])
        for jj in range(gw // 128):
            j = c * (gw // 128) + jj
            us_ref[j] = u[:, jj * 128:(jj + 1) * 128]
            for half in range(S5_CHUNK // S5_LANE_BLOCKS):
                ws = [us_ref[j, pl.ds(half * S5_LANE_BLOCKS + a, chunks, stride=S5_CHUNK), :]
                      for a in range(S5_LANE_BLOCKS)]
                ws = _block_transpose(ws)
                for g8 in range(S5_LANE_BLOCKS):
                    ug_ref[j * S5_LANE_BLOCKS + g8, :, half * 128:(half + 1) * 128] = (
                        ws[g8].astype(ug_ref.dtype))
    for c in range(D_MODEL // gw):
        cols = slice(c * gw, (c + 1) * gw)
        wza = wqz_ref[GDN_QKV + c * gw:GDN_QKV + (c + 1) * gw, :]
        wzb = wuz_ref[D_MODEL + c * gw:D_MODEL + (c + 1) * gw, :]
        za_ref[:, cols] = _silu(_dot_nt(h, wza)).astype(za_ref.dtype)
        zb_ref[:, cols] = _silu(_dot_nt(h, wzb)).astype(zb_ref.dtype)

    cw = 512
    slabs_per_chunk = cw // 128
    outs = (q_ref, k_ref, v_ref)
    for c in range(GDN_QKV // cw):
        raw = _dot_nt(h, wqz_ref[c * cw:(c + 1) * cw, :])
        for sl in range(slabs_per_chunk):
            ext_ref[c * slabs_per_chunk + sl, HALO:HALO + TM, :] = raw[:, sl * 128:(sl + 1) * 128]
        which = (c * cw) // (GDN_HEADS * GDN_DK)
        out_ref = outs[which]
        for sl in range(slabs_per_chunk):
            slab = c * slabs_per_chunk + sl
            w = convw_ref[:, slab * 128:(slab + 1) * 128]
            for r in range(CONV_DEC):
                acc = None
                for j in range(GDN_CONV):
                    start = HALO + r - (GDN_CONV - 1) + j
                    term = ext_ref[slab, pl.ds(start, TM // CONV_DEC, stride=CONV_DEC), :] * w[j:j + 1]
                    acc = term if acc is None else acc + term
                sh = _silu(acc)
                if which < 2:
                    scale = lax.rsqrt(jnp.sum(sh * sh, axis=-1, keepdims=True) + RMS_EPS)
                    if which == 0:
                        scale = scale * (GDN_DK ** -0.5)
                    sh = sh * scale
                mix_ref[sl, pl.ds(r, TM // CONV_DEC, stride=CONV_DEC), :] = sh
            col0 = slab * 128 - which * GDN_HEADS * GDN_DK
            out_ref[:, col0:col0 + 128] = mix_ref[sl].astype(out_ref.dtype)
    ext_ref[:, 0:HALO, :] = ext_ref[:, TM:TM + HALO, :]

    ba = _dot_nt(h, wba_ref[...])
    beta = _sigmoid(ba)
    xs = ba + dtb_ref[...]
    softplus = jnp.maximum(xs, 0.0) + jnp.log(1.0 + jnp.exp(-jnp.abs(xs)))
    g = -jnp.exp(alog_ref[...]) * softplus
    lane = lax.broadcasted_iota(jnp.int32, ba.shape, 1)
    bg_ref[...] = jnp.where(lane < GDN_HEADS, beta, g)


def _const_spec(shape):
    nd = len(shape)
    return pl.BlockSpec(shape, lambda *_: (0,) * nd)


def _proj0(x2, npre, w_t, convw, alog, dtb):
    n_tiles = TOKENS // TM
    n_qz = GDN_QKV + GDN_HEADS * GDN_DK
    hbm = pl.BlockSpec(memory_space=pl.ANY)
    tok = lambda cols: pl.BlockSpec((TM, cols), lambda i: (i, 0))
    hd = GDN_HEADS * GDN_DK
    out_shape = (
        jax.ShapeDtypeStruct((TOKENS, hd), BF16),
        jax.ShapeDtypeStruct((TOKENS, hd), BF16),
        jax.ShapeDtypeStruct((TOKENS, hd), BF16),
        jax.ShapeDtypeStruct((TOKENS, hd), BF16),
        jax.ShapeDtypeStruct((TOKENS, 128), F32),
        jax.ShapeDtypeStruct((S5_GROUPS, S5_ROWS, S5_W), BF16),
        jax.ShapeDtypeStruct((TOKENS, D_MODEL), BF16),
    )
    ug_spec = pl.BlockSpec((S5_GROUPS, TM // S5_CHUNK, S5_W), lambda i: (0, i, 0))
    return pl.pallas_call(
        _proj0_kernel,
        grid=(n_tiles,),
        in_specs=[tok(D_MODEL), _const_spec((1, D_MODEL)), hbm,
                  _const_spec(convw.shape), _const_spec((1, 128)), _const_spec((1, 128))],
        out_specs=(tok(hd), tok(hd), tok(hd), tok(hd), tok(128), ug_spec, tok(D_MODEL)),
        out_shape=out_shape,
        scratch_shapes=[pltpu.VMEM((GDN_QKV // 128, TM + HALO, 128), F32),
                        pltpu.VMEM((4, TM, 128), F32),
                        pltpu.VMEM((D_MODEL // 128, TM, 128), F32),
                        pltpu.VMEM((n_qz, D_MODEL), BF16),
                        pltpu.VMEM((2 * D_MODEL, D_MODEL), BF16),
                        pltpu.VMEM((128, D_MODEL), BF16)]
        + _stage_scratch(D_MODEL),
        compiler_params=pltpu.CompilerParams(dimension_semantics=("arbitrary",),
                                             vmem_limit_bytes=VMEM_LIMIT),
        name="proj0",
    )(x2, npre, w_t, convw, alog, dtb)


GDN_PACK = 4
GDN_NGRP = GDN_HEADS // GDN_PACK
GDN_PW = GDN_PACK * GDN_CHUNK
GDN_GW = GDN_PACK * GDN_DK
GDN_PREP_TILE = 512


def _lane_bcast(x, lane, width):
    return jnp.broadcast_to(x[:, lane:lane + 1], (x.shape[0], width))


def _gdn_prep_kernel(q_ref, k_ref, v_ref, bg_ref,
                     u_ref, w_ref, qd_ref, kd_ref, attn_ref, el_ref):
    c = GDN_CHUNK
    pw = GDN_PW
    n_chunks = GDN_PREP_TILE // c

    row_p = lax.broadcasted_iota(jnp.int32, (c, pw), 0)
    lane_p = lax.broadcasted_iota(jnp.int32, (c, pw), 1)
    col_p = lane_p % c
    causal_p = row_p >= col_p
    strict_p = row_p > col_p
    eye_p = jnp.where(row_p == col_p, 1.0, 0.0).astype(F32)
    r2 = lax.broadcasted_iota(jnp.int32, (pw, pw), 0)
    c2 = lax.broadcasted_iota(jnp.int32, (pw, pw), 1)
    bd_mask = (r2 // c) == (c2 // c)
    r3 = lax.broadcasted_iota(jnp.int32, (pw, GDN_GW), 0)
    c3 = lax.broadcasted_iota(jnp.int32, (pw, GDN_GW), 1)
    kbd_mask = (r3 // c) == (c3 // GDN_DK)
    r1 = lax.broadcasted_iota(jnp.int32, (c, c), 0)
    c1 = lax.broadcasted_iota(jnp.int32, (c, c), 1)
    tri = jnp.where(r1 >= c1, 1.0, 0.0).astype(BF16)
    lane128 = lax.broadcasted_iota(jnp.int32, (c, 128), 1)
    low_half = lane128 < c

    bd_one = jnp.where(bd_mask, 1.0, 0.0).astype(BF16)
    kbd_one = jnp.where(kbd_mask, 1.0, 0.0).astype(BF16)

    def block_diag(x_b):
        return jnp.concatenate([x_b] * GDN_PACK, axis=0) * bd_one

    chunks = []
    for ci in range(n_chunks):
        rows = slice(ci * c, (ci + 1) * c)
        bg = bg_ref[rows, :]
        g_pad = jnp.where((lane128 >= GDN_HEADS) & (lane128 < 2 * GDN_HEADS), bg, 0.0)
        chunks.append(dict(ci=ci, rows=rows, bg=bg, g3=_split3(g_pad)))
    for ch in chunks:
        g1, g2, g3 = ch["g3"]
        gc = _dot(tri, g1) + _dot(tri, g2) + _dot(tri, g3)
        ch["gc"] = gc
        ch["gc_t"] = gc.T
        ch["e_gc"] = jnp.exp(gc)
        ch["e_rev"] = jnp.exp(gc[c - 1:c, :] - gc)

    ctxs = []
    for ch in chunks:
        rows, bg, gc = ch["rows"], ch["bg"], ch["gc"]
        for gh in range(GDN_NGRP):
            heads = range(gh * GDN_PACK, (gh + 1) * GDN_PACK)
            cols = slice(gh * GDN_GW, (gh + 1) * GDN_GW)
            k_b = k_ref[rows, cols]
            q_b = q_ref[rows, cols]
            wide = lambda src, off: jnp.concatenate(
                [_lane_bcast(src, off + hd, GDN_DK) for hd in heads], axis=1)
            beta_f = wide(bg, 0)
            egc_f = wide(ch["e_gc"], GDN_HEADS)
            erev_f = wide(ch["e_rev"], GDN_HEADS)
            kf = k_b.astype(F32)
            kb = kf * beta_f
            kbd = jnp.concatenate([k_b] * GDN_PACK, axis=0) * kbd_one
            kkqk = _dot_nt(jnp.concatenate([kb.astype(BF16), q_b], axis=0), kbd)

            bc = [_lane_bcast(gc, GDN_HEADS + hd, 128) for hd in heads]
            gcol = jnp.concatenate([jnp.where(low_half, bc[0], bc[1]),
                                    jnp.where(low_half, bc[2], bc[3])], axis=1)
            gt = ch["gc_t"]
            rows_t = [gt[GDN_HEADS + hd:GDN_HEADS + hd + 1, :] for hd in heads]
            grow = jnp.concatenate([jnp.concatenate(rows_t[0:2], axis=1),
                                    jnp.concatenate(rows_t[2:4], axis=1)], axis=1)
            decay = jnp.exp(jnp.minimum(gcol - grow, 0.0))
            a = jnp.where(strict_p, kkqk[:c] * decay, 0.0)
            attn = jnp.where(causal_p, kkqk[c:] * decay, 0.0)

            vb = (v_ref[rows, cols].astype(F32) * beta_f).astype(BF16)
            kbe = (kb * egc_f).astype(BF16)
            rhs = jnp.concatenate(
                [jnp.concatenate([vb[:, i * GDN_DK:(i + 1) * GDN_DK],
                                  kbe[:, i * GDN_DK:(i + 1) * GDN_DK]], axis=1)
                 for i in range(GDN_PACK)], axis=0)

            qd_ref[rows, cols] = (q_b.astype(F32) * egc_f).astype(qd_ref.dtype)
            kd_ref[rows, cols] = (kf * erev_f).astype(kd_ref.dtype)
            el_ref[ch["ci"], :, cols] = egc_f[c - 1:c, :]
            attn_ref[rows, gh * pw:(gh + 1) * pw] = attn.astype(attn_ref.dtype)
            ctxs.append(dict(rows=rows, gh=gh, t=eye_p - a, p=a, rhs=rhs))

    for cx in ctxs:
        p_b = cx["p"].astype(BF16)
        cx["p"] = _dot(p_b, block_diag(p_b))
    for lvl in range(1, 6):
        for cx in ctxs:
            p_b = cx["p"].astype(BF16)
            t_b = cx["t"].astype(BF16)
            if lvl < 5:
                res = _dot(jnp.concatenate([t_b, p_b], axis=0), block_diag(p_b))
                cx["t"] = cx["t"] + res[:c]
                cx["p"] = res[c:]
            else:
                cx["t"] = cx["t"] + _dot(t_b, block_diag(p_b))

    for cx in ctxs:
        sol = _dot(block_diag(cx["t"].astype(BF16)), cx["rhs"])
        for i in range(GDN_PACK):
            hd = cx["gh"] * GDN_PACK + i
            blk = sol[i * c:(i + 1) * c]
            u_ref[cx["rows"], hd * GDN_DK:(hd + 1) * GDN_DK] = blk[:, :GDN_DK].astype(u_ref.dtype)
            w_ref[cx["rows"], hd * GDN_DK:(hd + 1) * GDN_DK] = blk[:, GDN_DK:].astype(w_ref.dtype)


def _gdn_prep(q, k, v, bg):
    hd = GDN_HEADS * GDN_DK
    n_chunks = GDN_PREP_TILE // GDN_CHUNK
    tok = lambda cols: pl.BlockSpec((GDN_PREP_TILE, cols), lambda i: (i, 0))
    wide = jax.ShapeDtypeStruct((TOKENS, hd), BF16)
    return pl.pallas_call(
        _gdn_prep_kernel,
        grid=(TOKENS // GDN_PREP_TILE,),
        in_specs=[tok(hd), tok(hd), tok(hd), tok(128)],
        out_specs=(tok(hd), tok(hd), tok(hd), tok(hd), tok(GDN_HEADS * GDN_CHUNK),
                   pl.BlockSpec((n_chunks, 1, hd), lambda i: (i, 0, 0))),
        out_shape=(wide, wide, wide, wide,
                   jax.ShapeDtypeStruct((TOKENS, GDN_HEADS * GDN_CHUNK), BF16),
                   jax.ShapeDtypeStruct((TOKENS // GDN_CHUNK, 1, hd), F32)),
        compiler_params=pltpu.CompilerParams(dimension_semantics=("arbitrary",),
                                             vmem_limit_bytes=VMEM_LIMIT),
        name="gdn_prep",
    )(q, k, v, bg)


GDN_SCAN_TILE = 512


def _gdn_scan_kernel(u_ref, w_ref, qd_ref, kd_ref, attn_ref, el_ref, gnw_ref, o_ref, state_ref):
    c = GDN_CHUNK

    @pl.when(pl.program_id(0) == 0)
    def _():
        state_ref[...] = jnp.zeros(state_ref.shape, F32)

    gnw = gnw_ref[...]
    zblk = jnp.zeros((c, GDN_DK), BF16)
    chains = [(b, hd) for b in range(BATCH) for hd in range(GDN_HEADS)]
    head_cols = lambda hd: slice(hd * GDN_DK, (hd + 1) * GDN_DK)

    def chunk_body(ci, carry):
        rows = pl.ds(pl.multiple_of(ci * c, c), c)
        wqs = {}
        for b, hd in chains:
            cols = head_cols(hd)
            wq = jnp.concatenate([w_ref[b, rows, cols], qd_ref[b, rows, cols]], axis=0)
            wqs[b, hd] = _dot(wq, state_ref[b, hd].astype(BF16))
        for b in range(BATCH):
            for hp in range(GDN_HEADS // 2):
                pair = (2 * hp, 2 * hp + 1)
                v_b = [(u_ref[b, rows, head_cols(hd)].astype(F32) - wqs[b, hd][:c]).astype(BF16)
                       for hd in pair]
                v_bd = jnp.concatenate([jnp.concatenate([v_b[0], zblk], axis=1),
                                        jnp.concatenate([zblk, v_b[1]], axis=1)], axis=0)
                av = _dot(attn_ref[b, rows, hp * 2 * c:(hp + 1) * 2 * c], v_bd)
                for i, hd in enumerate(pair):
                    cols = head_cols(hd)
                    o = wqs[b, hd][c:] + av[:, i * GDN_DK:(i + 1) * GDN_DK]
                    e_last = el_ref[b, ci, :, cols]
                    state_ref[b, hd] = (state_ref[b, hd] * e_last
                                        + _dot_tn(kd_ref[b, rows, cols], v_b[i]))
                    o_ref[b, rows, cols] = (_rms_scale(o) * gnw).astype(o_ref.dtype)
        return carry

    lax.fori_loop(0, GDN_SCAN_TILE // c, chunk_body, 0, unroll=4)


def _gdn_scan(u, w, qd, kd, attn, el, gnw):
    hd = GDN_HEADS * GDN_DK
    aw = GDN_HEADS * GDN_CHUNK
    n_chunks = GDN_SCAN_TILE // GDN_CHUNK
    as3 = lambda a: a.reshape(BATCH, SEQ, a.shape[-1])
    tok = pl.BlockSpec((BATCH, GDN_SCAN_TILE, hd), lambda j: (0, j, 0))
    el4 = el.reshape(BATCH, SEQ // GDN_CHUNK, 1, hd)
    out = pl.pallas_call(
        _gdn_scan_kernel,
        grid=(SEQ // GDN_SCAN_TILE,),
        in_specs=[tok, tok, tok, tok, pl.BlockSpec((BATCH, GDN_SCAN_TILE, aw), lambda j: (0, j, 0)),
                  pl.BlockSpec((BATCH, n_chunks, 1, hd), lambda j: (0, j, 0, 0)),
                  _const_spec((1, GDN_DK))],
        out_specs=tok,
        out_shape=jax.ShapeDtypeStruct((BATCH, SEQ, hd), BF16),
        scratch_shapes=[pltpu.VMEM((BATCH, GDN_HEADS, GDN_DK, GDN_DK), F32)],
        compiler_params=pltpu.CompilerParams(dimension_semantics=("arbitrary",),
                                             vmem_limit_bytes=VMEM_LIMIT),
        name="gdn_scan",
    )(as3(u), as3(w), as3(qd), as3(kd), as3(attn), el4, gnw)
    return out.reshape(TOKENS, hd)


S5_PREP_GROUPS = 8
S5_PAIR_STATE = 2 * S5_STATE


def _s5_prep_kernel(lre_ref, lim_ref, ldt_ref, bre_ref, bim_ref, cre_ref, cim_ref, d_ref,
                    toep_ref, m1r_ref, m1i_ref, m3r_ref, m3i_ref, a16r_ref, a16i_ref):
    n = S5_CHUNK
    row = lax.broadcasted_iota(jnp.int32, (S5_W, S5_W), 0)
    col = lax.broadcasted_iota(jnp.int32, (S5_W, S5_W), 1)
    colblk = lax.broadcasted_iota(jnp.int32, (S5_W, 128), 1) // S5_GROUP
    for gi in range(S5_PREP_GROUPS):
        lr = jnp.minimum(lre_ref[gi], -S5_MIN_NEG)
        li = lim_ref[gi]
        dt = jnp.exp(ldt_ref[gi])
        mag = jnp.exp(lr * dt)
        ar = mag * jnp.cos(li * dt)
        ai = mag * jnp.sin(li * dt)
        den = lr * lr + li * li
        nr, ni = ar - 1.0, ai
        fr = (nr * lr + ni * li) / den
        fi = (ni * lr - nr * li) / den
        br, bi = bre_ref[gi], bim_ref[gi]
        bbr = fr * br - fi * bi
        bbi = fr * bi + fi * br
        pr = [jnp.ones_like(ar)]
        pi = [jnp.zeros_like(ar)]
        for _ in range(n):
            pr.append(pr[-1] * ar - pi[-1] * ai)
            pi.append(pr[-2] * ai + pi[-1] * ar)
        m1r = jnp.concatenate([pr[n - 1 - s] * bbr - pi[n - 1 - s] * bbi for s in range(n)], axis=0)
        m1i = jnp.concatenate([pr[n - 1 - s] * bbi + pi[n - 1 - s] * bbr for s in range(n)], axis=0)
        cr, ci = cre_ref[gi], cim_ref[gi]
        m3r = jnp.concatenate([cr * pr[t + 1] - ci * pi[t + 1] for t in range(n)], axis=0)
        m3i = jnp.concatenate([-(cr * pi[t + 1] + ci * pr[t + 1]) for t in range(n)], axis=0)
        cer = jnp.concatenate([cr] * n, axis=0)
        cei = jnp.concatenate([ci] * n, axis=0)
        rt = _dot3_nt(jnp.concatenate([m1r, m1i], axis=1), jnp.concatenate([cer, -cei], axis=1))
        halves = []
        for lt in range(S5_W // 128):
            rt_lt = rt[:, lt * 128:(lt + 1) * 128]
            part = jnp.zeros((S5_W, 128), F32)
            for tb in range(S5_LANE_BLOCKS):
                sh = (n - 1 - (lt * S5_LANE_BLOCKS + tb)) * S5_GROUP
                if sh:
                    shifted = jnp.concatenate([rt_lt[sh:], jnp.zeros((sh, 128), F32)], axis=0)
                else:
                    shifted = rt_lt
                part = jnp.where(colblk == tb, shifted, part)
            halves.append(part)
        toep = jnp.concatenate(halves, axis=1) + jnp.where(row == col, d_ref[gi], 0.0)
        toep_ref[gi] = toep.astype(toep_ref.dtype)
        m1r_ref[gi] = m1r.astype(m1r_ref.dtype)
        m1i_ref[gi] = m1i.astype(m1i_ref.dtype)
        m3r_ref[gi] = m3r.astype(m3r_ref.dtype)
        m3i_ref[gi] = m3i.astype(m3i_ref.dtype)
        own = (lax.broadcasted_iota(jnp.int32, ar.shape, 1) // S5_STATE) == (gi % 2)
        a16r_ref[gi] = jnp.where(own, pr[n], 0.0)
        a16i_ref[gi] = jnp.where(own, pi[n], 0.0)


def _s5_prep(lre, lim, ldt, bre_t, bim_t, cre, cim, drow):
    gp = S5_PREP_GROUPS
    blk = lambda *tail: pl.BlockSpec((gp,) + tail, lambda i: (i,) + (0,) * len(tail))
    p = S5_PAIR_STATE
    out_shape = (
        jax.ShapeDtypeStruct((S5_GROUPS, S5_W, S5_W), BF16),
        jax.ShapeDtypeStruct((S5_GROUPS, S5_W, p), BF16),
        jax.ShapeDtypeStruct((S5_GROUPS, S5_W, p), BF16),
        jax.ShapeDtypeStruct((S5_GROUPS, S5_W, p), BF16),
        jax.ShapeDtypeStruct((S5_GROUPS, S5_W, p), BF16),
        jax.ShapeDtypeStruct((S5_GROUPS, 1, p), F32),
        jax.ShapeDtypeStruct((S5_GROUPS, 1, p), F32),
    )
    return pl.pallas_call(
        _s5_prep_kernel,
        grid=(S5_GROUPS // gp,),
        in_specs=[blk(1, p), blk(1, p), blk(1, 1), blk(S5_GROUP, p), blk(S5_GROUP, p),
                  blk(S5_GROUP, p), blk(S5_GROUP, p), blk(1, S5_W)],
        out_specs=(blk(S5_W, S5_W), blk(S5_W, p), blk(S5_W, p), blk(S5_W, p), blk(S5_W, p),
                   blk(1, p), blk(1, p)),
        out_shape=out_shape,
        compiler_params=pltpu.CompilerParams(dimension_semantics=("arbitrary",),
                                             vmem_limit_bytes=VMEM_LIMIT),
        name="s5_prep",
    )(lre, lim, ldt, bre_t, bim_t, cre, cim, drow)


S5_PAIRS = 4
S5_PITCH = S5_CPS + 8


def _s5_kernel(u_ref, toep_ref, m1r_ref, m1i_ref, m3r_ref, m3i_ref, a16r_ref, a16i_ref,
               y_ref, zr_ref, zi_ref, xr_ref, xi_ref, pr_ref, pi_ref):
    p = S5_PAIR_STATE
    segs = S5_SEGS
    n_blk = BATCH * segs
    blk_rows = lambda blk: slice(blk * S5_PITCH, blk * S5_PITCH + S5_CPS)
    ar, ai = [], []
    for pp in range(S5_PAIRS):
        g0, g1 = 2 * pp, 2 * pp + 1
        m1 = lambda g: jnp.concatenate([m1r_ref[g], m1i_ref[g]], axis=1)
        z = _dot(u_ref[g0], m1(g0)) + _dot(u_ref[g1], m1(g1))
        zr, zi = z[:, :p], z[:, p:]
        for blk in range(n_blk):
            zr_ref[pp, blk_rows(blk), :] = zr[blk * S5_CPS:(blk + 1) * S5_CPS]
            zi_ref[pp, blk_rows(blk), :] = zi[blk * S5_CPS:(blk + 1) * S5_CPS]
        ar.append(jnp.broadcast_to(a16r_ref[g0] + a16r_ref[g1], (segs, p)))
        ai.append(jnp.broadcast_to(a16i_ref[g0] + a16i_ref[g1], (segs, p)))

    def scan_body(i, carry):
        new = []
        for pp in range(S5_PAIRS):
            pwr, pwi, st = carry[pp]
            pr_ref[pp, pl.ds(i, 1), :] = pwr[0:1]
            pi_ref[pp, pl.ds(i, 1), :] = pwi[0:1]
            new_st = []
            for b in range(BATCH):
                sr, si = st[b]
                rb = pl.ds(b * segs * S5_PITCH + i, segs, stride=S5_PITCH)
                xr_ref[pp, rb, :] = sr
                xi_ref[pp, rb, :] = si
                new_st.append((sr * ar[pp] - si * ai[pp] + zr_ref[pp, rb, :],
                               sr * ai[pp] + si * ar[pp] + zi_ref[pp, rb, :]))
            new.append((pwr * ar[pp] - pwi * ai[pp], pwr * ai[pp] + pwi * ar[pp], tuple(new_st)))
        return tuple(new)

    zero = jnp.zeros((segs, p), F32)
    init = tuple((jnp.ones((segs, p), F32), zero, tuple((zero, zero) for _ in range(BATCH)))
                 for _ in range(S5_PAIRS))
    final = lax.fori_loop(0, S5_CPS, scan_body, init)

    sub = lax.broadcasted_iota(jnp.int32, (segs, p), 0)
    for pp in range(S5_PAIRS):
        pwr, pwi, st = final[pp]
        pw_r = pr_ref[pp]
        pw_i = pi_ref[pp]
        xr_parts, xi_parts = [], []
        for b in range(BATCH):
            fr, fi = st[b]
            gr, gi = zero, zero
            for _ in range(segs - 1):
                nr = pwr * gr - pwi * gi + fr
                ni = pwr * gi + pwi * gr + fi
                gr = jnp.where(sub == 0, 0.0, pltpu.roll(nr, 1, 0))
                gi = jnp.where(sub == 0, 0.0, pltpu.roll(ni, 1, 0))
            for s in range(segs):
                rows = blk_rows(b * segs + s)
                g_r, g_i = gr[s:s + 1], gi[s:s + 1]
                xr_parts.append(xr_ref[pp, rows, :] + pw_r * g_r - pw_i * g_i)
                xi_parts.append(xi_ref[pp, rows, :] + pw_r * g_i + pw_i * g_r)
        x_b = jnp.concatenate([jnp.concatenate(xr_parts, axis=0), jnp.concatenate(xi_parts, axis=0)],
                              axis=1).astype(BF16)
        for g in (2 * pp, 2 * pp + 1):
            m3 = jnp.concatenate([m3r_ref[g], m3i_ref[g]], axis=1)
            y = _dot(u_ref[g], toep_ref[g]) + _dot_nt(x_b, m3)
            y_ref[g] = y.astype(y_ref.dtype)


def _s5(ug, toep, m1r, m1i, m3r, m3i, a16r, a16i):
    gb = 2 * S5_PAIRS
    p = S5_PAIR_STATE
    blk = lambda *tail: pl.BlockSpec((gb,) + tail, lambda i: (i,) + (0,) * len(tail))
    seg_rows = BATCH * S5_SEGS * S5_PITCH
    return pl.pallas_call(
        _s5_kernel,
        grid=(S5_GROUPS // gb,),
        in_specs=[blk(S5_ROWS, S5_W), blk(S5_W, S5_W), blk(S5_W, p), blk(S5_W, p), blk(S5_W, p),
                  blk(S5_W, p), blk(1, p), blk(1, p)],
        out_specs=blk(S5_ROWS, S5_W),
        out_shape=jax.ShapeDtypeStruct((S5_GROUPS, S5_ROWS, S5_W), BF16),
        scratch_shapes=[pltpu.VMEM((S5_PAIRS, seg_rows, p), F32) for _ in range(4)]
        + [pltpu.VMEM((S5_PAIRS, S5_CPS, p), F32) for _ in range(2)],
        compiler_params=pltpu.CompilerParams(dimension_semantics=("arbitrary",),
                                             vmem_limit_bytes=VMEM_LIMIT),
        name="s5",
    )(ug, toep, m1r, m1i, m3r, m3i, a16r, a16i)


def _gelu_tanh(x):
    return 0.5 * x * (1.0 + jnp.tanh(math.sqrt(2.0 / math.pi) * (x + 0.044715 * (x * x * x))))


OUT0_PARTS = 2


def _out0_kernel(x_ref, yg_ref, zb_ref, oa_ref, za_ref, wglu_hbm, wout_hbm, npost_ref, o_ref,
                 ys_ref, wglu_ref, wout_ref, stage_ref, sem):
    @pl.when(pl.program_id(0) == 0)
    def _():
        _load_weight_bf16(wglu_hbm, wglu_ref, stage_ref, sem)
        _load_weight_bf16(wout_hbm, wout_ref, stage_ref, sem)

    hd = GDN_HEADS * GDN_DK
    part_rows = TM // OUT0_PARTS
    chunks = part_rows // S5_CHUNK
    for part in range(OUT0_PARTS):
        r0 = part * part_rows
        rows = slice(r0, r0 + part_rows)
        crows = slice(part * chunks, (part + 1) * chunks)
        for j in range(D_MODEL // 128):
            for half in range(S5_CHUNK // S5_LANE_BLOCKS):
                ws = [yg_ref[j * S5_LANE_BLOCKS + g8, crows, half * 128:(half + 1) * 128].astype(F32)
                      for g8 in range(S5_LANE_BLOCKS)]
                ws = _block_transpose(ws)
                for a in range(S5_LANE_BLOCKS):
                    ys_ref[j, pl.ds(r0 + half * S5_LANE_BLOCKS + a, chunks, stride=S5_CHUNK), :] = ws[a]
        y = _gelu_tanh(jnp.concatenate([ys_ref[j, rows, :] for j in range(D_MODEL // 128)], axis=1))
        y = y * _sigmoid(_dot(y.astype(BF16), wglu_ref[...]))
        yb = (y * zb_ref[rows, :].astype(F32)).astype(BF16)
        ya = oa_ref[rows, :] * za_ref[rows, :]
        mix = _dot(ya, wout_ref[0:hd, :]) + _dot(yb, wout_ref[hd:, :])
        o_ref[rows, :] = x_ref[rows, :] + _rms_scale(mix) * npost_ref[...]


def _out0(x2, yg, zb, oa, za, wglu, wout, npost):
    tok = lambda cols: pl.BlockSpec((TM, cols), lambda i: (i, 0))
    yg_spec = pl.BlockSpec((S5_GROUPS, TM // S5_CHUNK, S5_W), lambda i: (0, i, 0))
    hbm = pl.BlockSpec(memory_space=pl.ANY)
    return pl.pallas_call(
        _out0_kernel,
        grid=(TOKENS // TM,),
        in_specs=[tok(D_MODEL), yg_spec, tok(D_MODEL), tok(D_MODEL), tok(D_MODEL), hbm, hbm,
                  _const_spec((1, D_MODEL))],
        out_specs=tok(D_MODEL),
        out_shape=jax.ShapeDtypeStruct((TOKENS, D_MODEL), F32),
        scratch_shapes=[pltpu.VMEM((D_MODEL // 128, TM, 128), F32),
                        pltpu.VMEM(wglu.shape, BF16), pltpu.VMEM(wout.shape, BF16)]
        + _stage_scratch(D_MODEL),
        compiler_params=pltpu.CompilerParams(dimension_semantics=("arbitrary",),
                                             vmem_limit_bytes=VMEM_LIMIT),
        name="out0",
    )(x2, yg, zb, oa, za, wglu, wout, npost)


def _layer1_kernel(x_ref, npre_ref, win_hbm, convw_ref, wout_hbm, npost_ref, o_ref,
                   ext_ref, win_ref, wout_ref, stage_a, sem_a, stage_b, sem_b):
    tiles_per_seq = SEQ // TM

    @pl.when(pl.program_id(0) == 0)
    def _():
        _load_weight_bf16(win_hbm, win_ref, stage_a, sem_a)
        _load_weight_bf16(wout_hbm, wout_ref, stage_b, sem_b)

    @pl.when(pl.program_id(0) % tiles_per_seq == 0)
    def _():
        ext_ref[0:HALO, :] = jnp.zeros((HALO, SC_WIDTH), F32)

    x = x_ref[...]
    h = (_rms_scale(x) * npre_ref[...]).astype(BF16)
    cw = 512
    acc = jnp.zeros((TM, D_MODEL), F32)
    for c in range(SC_WIDTH // cw):
        cols = slice(c * cw, (c + 1) * cw)
        part = lambda k: _dot(h, win_ref[:, k * SC_WIDTH + c * cw:k * SC_WIDTH + (c + 1) * cw])
        prod = part(1) * part(2)
        ext_ref[HALO:HALO + TM, cols] = prod
        w = convw_ref[:, cols]
        conv = prod * w[2:3]
        for j in range(SC_CONV - 1):
            shift = SC_CONV - 1 - j
            conv = conv + ext_ref[HALO - shift:HALO - shift + TM, cols] * w[j:j + 1]
        y = part(0) * conv * _silu(part(3))
        acc = acc + _dot(y.astype(BF16), wout_ref[cols, :])
    ext_ref[0:HALO, :] = ext_ref[TM:TM + HALO, :]
    o_ref[...] = x + _rms_scale(acc) * npost_ref[...]


def _layer1(x1, npre, win, convw, wout, npost):
    tok = lambda cols: pl.BlockSpec((TM, cols), lambda i: (i, 0))
    hbm = pl.BlockSpec(memory_space=pl.ANY)
    return pl.pallas_call(
        _layer1_kernel,
        grid=(TOKENS // TM,),
        in_specs=[tok(D_MODEL), _const_spec((1, D_MODEL)), hbm, _const_spec(convw.shape), hbm,
                  _const_spec((1, D_MODEL))],
        out_specs=tok(D_MODEL),
        out_shape=jax.ShapeDtypeStruct((TOKENS, D_MODEL), F32),
        scratch_shapes=[pltpu.VMEM((TM + HALO, SC_WIDTH), F32),
                        pltpu.VMEM(win.shape, BF16), pltpu.VMEM(wout.shape, BF16)]
        + _stage_scratch(win.shape[1]) + _stage_scratch(wout.shape[1]),
        compiler_params=pltpu.CompilerParams(dimension_semantics=("arbitrary",),
                                             vmem_limit_bytes=VMEM_LIMIT),
        name="layer1",
    )(x1, npre, win, convw, wout, npost)


def kernel(x, norm_pre, norm_post, w_in_even, conv_qkv, a_log, dt_bias, gdn_norm_w,
           s5_lam_re, s5_lam_im, s5_b_re, s5_b_im, s5_c_re, s5_c_im, s5_log_dt, s5_d,
           w_glu, w_out_even, w_in_odd, conv_short, w_out_odd):
    x2 = x.reshape(TOKENS, D_MODEL)
    hd = GDN_HEADS * GDN_DK

    w_t = jnp.swapaxes(w_in_even[0], 0, 1)
    pad_g = lambda v: jnp.pad(v, (GDN_HEADS, 128 - 2 * GDN_HEADS)).reshape(1, 128)

    q, k, v, za, bg, ug, zb = _proj0(x2, norm_pre[0].reshape(1, D_MODEL), w_t,
                                    conv_qkv[0], pad_g(a_log[0]), pad_g(dt_bias[0]))
    u_c, w_c, q_dec, k_dec, attn, e_last = _gdn_prep(q, k, v, bg)
    oa = _gdn_scan(u_c, w_c, q_dec, k_dec, attn, e_last, gdn_norm_w[0].reshape(1, GDN_DK))

    p = S5_STATE
    drow = jnp.tile(s5_d[0].reshape(S5_GROUPS, 1, S5_GROUP), (1, 1, S5_CHUNK))
    odd = (jnp.arange(S5_GROUPS) % 2 == 1).reshape(S5_GROUPS, 1, 1)

    def by_parity(a):
        z = jnp.zeros_like(a)
        return jnp.concatenate([jnp.where(odd, z, a), jnp.where(odd, a, z)], axis=-1)

    toep, m1r, m1i, m3r, m3i, a16r, a16i = _s5_prep(
        by_parity(s5_lam_re[0].reshape(S5_GROUPS, 1, p)),
        by_parity(s5_lam_im[0].reshape(S5_GROUPS, 1, p)),
        s5_log_dt[0].reshape(S5_GROUPS, 1, 1),
        by_parity(jnp.swapaxes(s5_b_re[0], 1, 2)), by_parity(jnp.swapaxes(s5_b_im[0], 1, 2)),
        by_parity(s5_c_re[0]), by_parity(s5_c_im[0]), drow)
    yg = _s5(ug, toep, m1r, m1i, m3r, m3i, a16r, a16i)

    x1 = _out0(x2, yg, zb, oa, za, w_glu[0], w_out_even[0], norm_post[0].reshape(1, D_MODEL))

    out = _layer1(x1, norm_pre[1].reshape(1, D_MODEL), w_in_odd[0], conv_short[0], w_out_odd[0],
                  norm_post[1].reshape(1, D_MODEL))
    return out.reshape(BATCH, SEQ, D_MODEL)
```

```python
import functools
import math

import jax
import jax.numpy as jnp
from jax import lax
from jax.experimental import pallas as pl
from jax.experimental.pallas import tpu as pltpu

F32 = jnp.float32
BF16 = jnp.bfloat16

D_MODEL = 1024
BATCH = 2
SEQ = 8192
TOKENS = BATCH * SEQ
RMS_EPS = 1e-6

GDN_HEADS = 8
GDN_DK = 128
GDN_CONV = 4
GDN_CHUNK = 64
GDN_QKV = 3 * GDN_HEADS * GDN_DK

S5_GROUP = 16
S5_GROUPS = D_MODEL // S5_GROUP
S5_STATE = 64
S5_MIN_NEG = 1e-4
S5_CHUNK = 16
S5_SEGS = 8
S5_CPS = SEQ // (S5_CHUNK * S5_SEGS)
S5_ROWS = BATCH * SEQ // S5_CHUNK
S5_W = S5_CHUNK * S5_GROUP

SC_WIDTH = 2 * D_MODEL
SC_CONV = 3

HALO = 8
CONV_DEC = 4
VMEM_LIMIT = 56 * 1024 * 1024

TM = 512


def _dot(a, b):
    return jnp.dot(a, b, preferred_element_type=F32)


def _dot_nt(a, b):
    return lax.dot_general(a, b, (((1,), (1,)), ((), ())), preferred_element_type=F32)


def _dot_tn(a, b):
    return lax.dot_general(a, b, (((0,), (0,)), ((), ())), preferred_element_type=F32)


def _split3(x):
    x1 = x.astype(BF16)
    r = x - x1.astype(F32)
    x2 = r.astype(BF16)
    x3 = (r - x2.astype(F32)).astype(BF16)
    return x1, x2, x3


def _dot3_nt(a, b):
    a1, a2, _ = _split3(a)
    b1, b2, _ = _split3(b)
    return _dot_nt(a1, b1) + _dot_nt(a1, b2) + _dot_nt(a2, b1)


STAGE_BYTES = 2 * 1024 * 1024


def _stage_rows(n_cols):
    return STAGE_BYTES // (4 * n_cols)


def _stage_scratch(n_cols):
    return [pltpu.VMEM((2, _stage_rows(n_cols), n_cols), F32), pltpu.SemaphoreType.DMA((2,))]


def _load_weight_bf16(src_hbm, dst_ref, stage_ref, sem, src_row0=0, n_rows=None):
    n_rows = dst_ref.shape[0] if n_rows is None else n_rows
    n_cols = dst_ref.shape[1]
    rows = min(stage_ref.shape[1], n_rows)
    n = n_rows // rows

    def copy(i):
        return pltpu.make_async_copy(
            src_hbm.at[pl.ds(src_row0 + i * rows, rows), pl.ds(0, n_cols)],
            stage_ref.at[i % 2, pl.ds(0, rows)], sem.at[i % 2])

    copy(0).start()
    for i in range(n):
        if i + 1 < n:
            copy(i + 1).start()
        copy(i).wait()
        dst_ref[i * rows:(i + 1) * rows, :] = stage_ref[i % 2, 0:rows].astype(BF16)


def _sigmoid(x):
    return 1.0 / (1.0 + jnp.exp(-x))


def _silu(x):
    return x * _sigmoid(x)


def _rms_scale(x):
    return x * lax.rsqrt(jnp.mean(x * x, axis=-1, keepdims=True) + RMS_EPS)


S5_LANE_BLOCKS = 128 // S5_GROUP


def _block_transpose(ws):
    n = S5_LANE_BLOCKS
    blk = lax.broadcasted_iota(jnp.int32, ws[0].shape, 1) // S5_GROUP
    step = 1
    while step < n:
        upper = (blk & step) != 0
        new = list(ws)
        for a in range(n):
            if a & step:
                continue
            lo, hi = ws[a], ws[a + step]
            new[a] = jnp.where(upper, pltpu.roll(hi, step * S5_GROUP, 1), lo)
            new[a + step] = jnp.where(upper, hi, pltpu.roll(lo, 128 - step * S5_GROUP, 1))
        ws = new
        step *= 2
    return ws


def _proj0_kernel(x_ref, npre_ref, wt_hbm, convw_ref, alog_ref, dtb_ref,
                  q_ref, k_ref, v_ref, za_ref, bg_ref, ug_ref, zb_ref,
                  ext_ref, mix_ref, us_ref, wqz_ref, wuz_ref, wba_ref, stage_ref, sem):
    tiles_per_seq = SEQ // TM

    @pl.when(pl.program_id(0) == 0)
    def _():
        o_b = GDN_QKV + GDN_HEADS * GDN_DK
        o_u = o_b + 2 * GDN_HEADS
        _load_weight_bf16(wt_hbm, wqz_ref, stage_ref, sem)
        _load_weight_bf16(wt_hbm, wuz_ref, stage_ref, sem, src_row0=o_u)
        wba_ref[...] = jnp.zeros(wba_ref.shape, BF16)
        _load_weight_bf16(wt_hbm, wba_ref, stage_ref, sem, src_row0=o_b, n_rows=2 * GDN_HEADS)

    @pl.when(pl.program_id(0) % tiles_per_seq == 0)
    def _():
        ext_ref[:, 0:HALO, :] = jnp.zeros((GDN_QKV // 128, HALO, 128), F32)

    h = (_rms_scale(x_ref[...]) * npre_ref[...]).astype(BF16)

    gw = 256
    chunks = TM // S5_CHUNK
    for c in range(D_MODEL // gw):
        u = _dot_nt(h, wuz_ref[c * gw:(c + 1) * gw, :])
        for jj in range(gw // 128):
            j = c * (gw // 128) + jj
            us_ref[j] = u[:, jj * 128:(jj + 1) * 128]
            for half in range(S5_CHUNK // S5_LANE_BLOCKS):
                ws = [us_ref[j, pl.ds(half * S5_LANE_BLOCKS + a, chunks, stride=S5_CHUNK), :]
                      for a in range(S5_LANE_BLOCKS)]
                ws = _block_transpose(ws)
                for g8 in range(S5_LANE_BLOCKS):
                    ug_ref[j * S5_LANE_BLOCKS + g8, :, half * 128:(half + 1) * 128] = (
                        ws[g8].astype(ug_ref.dtype))
    for c in range(D_MODEL // gw):
        cols = slice(c * gw, (c + 1) * gw)
        wza = wqz_ref[GDN_QKV + c * gw:GDN_QKV + (c + 1) * gw, :]
        wzb = wuz_ref[D_MODEL + c * gw:D_MODEL + (c + 1) * gw, :]
        za_ref[:, cols] = _silu(_dot_nt(h, wza)).astype(za_ref.dtype)
        zb_ref[:, cols] = _silu(_dot_nt(h, wzb)).astype(zb_ref.dtype)

    cw = 512
    slabs_per_chunk = cw // 128
    outs = (q_ref, k_ref, v_ref)
    for c in range(GDN_QKV // cw):
        raw = _dot_nt(h, wqz_ref[c * cw:(c + 1) * cw, :])
        for sl in range(slabs_per_chunk):
            ext_ref[c * slabs_per_chunk + sl, HALO:HALO + TM, :] = raw[:, sl * 128:(sl + 1) * 128]
        which = (c * cw) // (GDN_HEADS * GDN_DK)
        out_ref = outs[which]
        for sl in range(slabs_per_chunk):
            slab = c * slabs_per_chunk + sl
            w = convw_ref[:, slab * 128:(slab + 1) * 128]
            for r in range(CONV_DEC):
                acc = None
                for j in range(GDN_CONV):
                    start = HALO + r - (GDN_CONV - 1) + j
                    term = ext_ref[slab, pl.ds(start, TM // CONV_DEC, stride=CONV_DEC), :] * w[j:j + 1]
                    acc = term if acc is None else acc + term
                sh = _silu(acc)
                if which < 2:
                    scale = lax.rsqrt(jnp.sum(sh * sh, axis=-1, keepdims=True) + RMS_EPS)
                    if which == 0:
                        scale = scale * (GDN_DK ** -0.5)
                    sh = sh * scale
                mix_ref[sl, pl.ds(r, TM // CONV_DEC, stride=CONV_DEC), :] = sh
            col0 = slab * 128 - which * GDN_HEADS * GDN_DK
            out_ref[:, col0:col0 + 128] = mix_ref[sl].astype(out_ref.dtype)
    ext_ref[:, 0:HALO, :] = ext_ref[:, TM:TM + HALO, :]

    ba = _dot_nt(h, wba_ref[...])
    beta = _sigmoid(ba)
    xs = ba + dtb_ref[...]
    softplus = jnp.maximum(xs, 0.0) + jnp.log(1.0 + jnp.exp(-jnp.abs(xs)))
    g = -jnp.exp(alog_ref[...]) * softplus
    lane = lax.broadcasted_iota(jnp.int32, ba.shape, 1)
    bg_ref[...] = jnp.where(lane < GDN_HEADS, beta, g)


def _const_spec(shape):
    nd = len(shape)
    return pl.BlockSpec(shape, lambda *_: (0,) * nd)


def _proj0(x2, npre, w_t, convw, alog, dtb):
    n_tiles = TOKENS // TM
    n_qz = GDN_QKV + GDN_HEADS * GDN_DK
    hbm = pl.BlockSpec(memory_space=pl.ANY)
    tok = lambda cols: pl.BlockSpec((TM, cols), lambda i: (i, 0))
    hd = GDN_HEADS * GDN_DK
    out_shape = (
        jax.ShapeDtypeStruct((TOKENS, hd), BF16),
        jax.ShapeDtypeStruct((TOKENS, hd), BF16),
        jax.ShapeDtypeStruct((TOKENS, hd), BF16),
        jax.ShapeDtypeStruct((TOKENS, hd), BF16),
        jax.ShapeDtypeStruct((TOKENS, 128), F32),
        jax.ShapeDtypeStruct((S5_GROUPS, S5_ROWS, S5_W), BF16),
        jax.ShapeDtypeStruct((TOKENS, D_MODEL), BF16),
    )
    ug_spec = pl.BlockSpec((S5_GROUPS, TM // S5_CHUNK, S5_W), lambda i: (0, i, 0))
    return pl.pallas_call(
        _proj0_kernel,
        grid=(n_tiles,),
        in_specs=[tok(D_MODEL), _const_spec((1, D_MODEL)), hbm,
                  _const_spec(convw.shape), _const_spec((1, 128)), _const_spec((1, 128))],
        out_specs=(tok(hd), tok(hd), tok(hd), tok(hd), tok(128), ug_spec, tok(D_MODEL)),
        out_shape=out_shape,
        scratch_shapes=[pltpu.VMEM((GDN_QKV // 128, TM + HALO, 128), F32),
                        pltpu.VMEM((4, TM, 128), F32),
                        pltpu.VMEM((D_MODEL // 128, TM, 128), F32),
                        pltpu.VMEM((n_qz, D_MODEL), BF16),
                        pltpu.VMEM((2 * D_MODEL, D_MODEL), BF16),
                        pltpu.VMEM((128, D_MODEL), BF16)]
        + _stage_scratch(D_MODEL),
        compiler_params=pltpu.CompilerParams(dimension_semantics=("arbitrary",),
                                             vmem_limit_bytes=VMEM_LIMIT),
        name="proj0",
    )(x2, npre, w_t, convw, alog, dtb)


GDN_PACK = 4
GDN_NGRP = GDN_HEADS // GDN_PACK
GDN_PW = GDN_PACK * GDN_CHUNK
GDN_GW = GDN_PACK * GDN_DK
GDN_PREP_TILE = 512


def _lane_bcast(x, lane, width):
    return jnp.broadcast_to(x[:, lane:lane + 1], (x.shape[0], width))


def _gdn_prep_kernel(q_ref, k_ref, v_ref, bg_ref,
                     u_ref, w_ref, qd_ref, kd_ref, attn_ref, el_ref):
    c = GDN_CHUNK
    pw = GDN_PW
    n_chunks = GDN_PREP_TILE // c

    row_p = lax.broadcasted_iota(jnp.int32, (c, pw), 0)
    lane_p = lax.broadcasted_iota(jnp.int32, (c, pw), 1)
    col_p = lane_p % c
    causal_p = row_p >= col_p
    strict_p = row_p > col_p
    eye_p = jnp.where(row_p == col_p, 1.0, 0.0).astype(F32)
    r2 = lax.broadcasted_iota(jnp.int32, (pw, pw), 0)
    c2 = lax.broadcasted_iota(jnp.int32, (pw, pw), 1)
    bd_mask = (r2 // c) == (c2 // c)
    r3 = lax.broadcasted_iota(jnp.int32, (pw, GDN_GW), 0)
    c3 = lax.broadcasted_iota(jnp.int32, (pw, GDN_GW), 1)
    kbd_mask = (r3 // c) == (c3 // GDN_DK)
    r1 = lax.broadcasted_iota(jnp.int32, (c, c), 0)
    c1 = lax.broadcasted_iota(jnp.int32, (c, c), 1)
    tri = jnp.where(r1 >= c1, 1.0, 0.0).astype(BF16)
    lane128 = lax.broadcasted_iota(jnp.int32, (c, 128), 1)
    low_half = lane128 < c

    bd_one = jnp.where(bd_mask, 1.0, 0.0).astype(BF16)
    kbd_one = jnp.where(kbd_mask, 1.0, 0.0).astype(BF16)

    def block_diag(x_b):
        return jnp.concatenate([x_b] * GDN_PACK, axis=0) * bd_one

    chunks = []
    for ci in range(n_chunks):
        rows = slice(ci * c, (ci + 1) * c)
        bg = bg_ref[rows, :]
        g_pad = jnp.where((lane128 >= GDN_HEADS) & (lane128 < 2 * GDN_HEADS), bg, 0.0)
        chunks.append(dict(ci=ci, rows=rows, bg=bg, g3=_split3(g_pad)))
    for ch in chunks:
        g1, g2, g3 = ch["g3"]
        gc = _dot(tri, g1) + _dot(tri, g2) + _dot(tri, g3)
        ch["gc"] = gc
        ch["gc_t"] = gc.T
        ch["e_gc"] = jnp.exp(gc)
        ch["e_rev"] = jnp.exp(gc[c - 1:c, :] - gc)

    ctxs = []
    for ch in chunks:
        rows, bg, gc = ch["rows"], ch["bg"], ch["gc"]
        for gh in range(GDN_NGRP):
            heads = range(gh * GDN_PACK, (gh + 1) * GDN_PACK)
            cols = slice(gh * GDN_GW, (gh + 1) * GDN_GW)
            k_b = k_ref[rows, cols]
            q_b = q_ref[rows, cols]
            wide = lambda src, off: jnp.concatenate(
                [_lane_bcast(src, off + hd, GDN_DK) for hd in heads], axis=1)
            beta_f = wide(bg, 0)
            egc_f = wide(ch["e_gc"], GDN_HEADS)
            erev_f = wide(ch["e_rev"], GDN_HEADS)
            kf = k_b.astype(F32)
            kb = kf * beta_f
            kbd = jnp.concatenate([k_b] * GDN_PACK, axis=0) * kbd_one
            kkqk = _dot_nt(jnp.concatenate([kb.astype(BF16), q_b], axis=0), kbd)

            bc = [_lane_bcast(gc, GDN_HEADS + hd, 128) for hd in heads]
            gcol = jnp.concatenate([jnp.where(low_half, bc[0], bc[1]),
                                    jnp.where(low_half, bc[2], bc[3])], axis=1)
            gt = ch["gc_t"]
            rows_t = [gt[GDN_HEADS + hd:GDN_HEADS + hd + 1, :] for hd in heads]
            grow = jnp.concatenate([jnp.concatenate(rows_t[0:2], axis=1),
                                    jnp.concatenate(rows_t[2:4], axis=1)], axis=1)
            decay = jnp.exp(jnp.minimum(gcol - grow, 0.0))
            a = jnp.where(strict_p, kkqk[:c] * decay, 0.0)
            attn = jnp.where(causal_p, kkqk[c:] * decay, 0.0)

            vb = (v_ref[rows, cols].astype(F32) * beta_f).astype(BF16)
            kbe = (kb * egc_f).astype(BF16)
            rhs = jnp.concatenate(
                [jnp.concatenate([vb[:, i * GDN_DK:(i + 1) * GDN_DK],
                                  kbe[:, i * GDN_DK:(i + 1) * GDN_DK]], axis=1)
                 for i in range(GDN_PACK)], axis=0)

            qd_ref[rows, cols] = (q_b.astype(F32) * egc_f).astype(qd_ref.dtype)
            kd_ref[rows, cols] = (kf * erev_f).astype(kd_ref.dtype)
            el_ref[ch["ci"], :, cols] = egc_f[c - 1:c, :]
            attn_ref[rows, gh * pw:(gh + 1) * pw] = attn.astype(attn_ref.dtype)
            ctxs.append(dict(rows=rows, gh=gh, t=eye_p - a, p=a, rhs=rhs))

    for cx in ctxs:
        p_b = cx["p"].astype(BF16)
        cx["p"] = _dot(p_b, block_diag(p_b))
    for lvl in range(1, 6):
        for cx in ctxs:
            p_b = cx["p"].astype(BF16)
            t_b = cx["t"].astype(BF16)
            if lvl < 5:
                res = _dot(jnp.concatenate([t_b, p_b], axis=0), block_diag(p_b))
                cx["t"] = cx["t"] + res[:c]
                cx["p"] = res[c:]
            else:
                cx["t"] = cx["t"] + _dot(t_b, block_diag(p_b))

    for cx in ctxs:
        sol = _dot(block_diag(cx["t"].astype(BF16)), cx["rhs"])
        for i in range(GDN_PACK):
            hd = cx["gh"] * GDN_PACK + i
            blk = sol[i * c:(i + 1) * c]
            u_ref[cx["rows"], hd * GDN_DK:(hd + 1) * GDN_DK] = blk[:, :GDN_DK].astype(u_ref.dtype)
            w_ref[cx["rows"], hd * GDN_DK:(hd + 1) * GDN_DK] = blk[:, GDN_DK:].astype(w_ref.dtype)


def _gdn_prep(q, k, v, bg):
    hd = GDN_HEADS * GDN_DK
    n_chunks = GDN_PREP_TILE // GDN_CHUNK
    tok = lambda cols: pl.BlockSpec((GDN_PREP_TILE, cols), lambda i: (i, 0))
    wide = jax.ShapeDtypeStruct((TOKENS, hd), BF16)
    return pl.pallas_call(
        _gdn_prep_kernel,
        grid=(TOKENS // GDN_PREP_TILE,),
        in_specs=[tok(hd), tok(hd), tok(hd), tok(128)],
        out_specs=(tok(hd), tok(hd), tok(hd), tok(hd), tok(GDN_HEADS * GDN_CHUNK),
                   pl.BlockSpec((n_chunks, 1, hd), lambda i: (i, 0, 0))),
        out_shape=(wide, wide, wide, wide,
                   jax.ShapeDtypeStruct((TOKENS, GDN_HEADS * GDN_CHUNK), BF16),
                   jax.ShapeDtypeStruct((TOKENS // GDN_CHUNK, 1, hd), F32)),
        compiler_params=pltpu.CompilerParams(dimension_semantics=("arbitrary",),
                                             vmem_limit_bytes=VMEM_LIMIT),
        name="gdn_prep",
    )(q, k, v, bg)


GDN_SCAN_TILE = 512


def _gdn_scan_kernel(u_ref, w_ref, qd_ref, kd_ref, attn_ref, el_ref, gnw_ref, o_ref, state_ref):
    c = GDN_CHUNK

    @pl.when(pl.program_id(0) == 0)
    def _():
        state_ref[...] = jnp.zeros(state_ref.shape, F32)

    gnw = gnw_ref[...]
    zblk = jnp.zeros((c, GDN_DK), BF16)
    chains = [(b, hd) for b in range(BATCH) for hd in range(GDN_HEADS)]
    head_cols = lambda hd: slice(hd * GDN_DK, (hd + 1) * GDN_DK)

    def chunk_body(ci, carry):
        rows = pl.ds(pl.multiple_of(ci * c, c), c)
        wqs = {}
        for b, hd in chains:
            cols = head_cols(hd)
            wq = jnp.concatenate([w_ref[b, rows, cols], qd_ref[b, rows, cols]], axis=0)
            wqs[b, hd] = _dot(wq, state_ref[b, hd].astype(BF16))
        for b in range(BATCH):
            for hp in range(GDN_HEADS // 2):
                pair = (2 * hp, 2 * hp + 1)
                v_b = [(u_ref[b, rows, head_cols(hd)].astype(F32) - wqs[b, hd][:c]).astype(BF16)
                       for hd in pair]
                v_bd = jnp.concatenate([jnp.concatenate([v_b[0], zblk], axis=1),
                                        jnp.concatenate([zblk, v_b[1]], axis=1)], axis=0)
                av = _dot(attn_ref[b, rows, hp * 2 * c:(hp + 1) * 2 * c], v_bd)
                for i, hd in enumerate(pair):
                    cols = head_cols(hd)
                    o = wqs[b, hd][c:] + av[:, i * GDN_DK:(i + 1) * GDN_DK]
                    e_last = el_ref[b, ci, :, cols]
                    state_ref[b, hd] = (state_ref[b, hd] * e_last
                                        + _dot_tn(kd_ref[b, rows, cols], v_b[i]))
                    o_ref[b, rows, cols] = (_rms_scale(o) * gnw).astype(o_ref.dtype)
        return carry

    lax.fori_loop(0, GDN_SCAN_TILE // c, chunk_body, 0, unroll=4)


def _gdn_scan(u, w, qd, kd, attn, el, gnw):
    hd = GDN_HEADS * GDN_DK
    aw = GDN_HEADS * GDN_CHUNK
    n_chunks = GDN_SCAN_TILE // GDN_CHUNK
    as3 = lambda a: a.reshape(BATCH, SEQ, a.shape[-1])
    tok = pl.BlockSpec((BATCH, GDN_SCAN_TILE, hd), lambda j: (0, j, 0))
    el4 = el.reshape(BATCH, SEQ // GDN_CHUNK, 1, hd)
    out = pl.pallas_call(
        _gdn_scan_kernel,
        grid=(SEQ // GDN_SCAN_TILE,),
        in_specs=[tok, tok, tok, tok, pl.BlockSpec((BATCH, GDN_SCAN_TILE, aw), lambda j: (0, j, 0)),
                  pl.BlockSpec((BATCH, n_chunks, 1, hd), lambda j: (0, j, 0, 0)),
                  _const_spec((1, GDN_DK))],
        out_specs=tok,
        out_shape=jax.ShapeDtypeStruct((BATCH, SEQ, hd), BF16),
        scratch_shapes=[pltpu.VMEM((BATCH, GDN_HEADS, GDN_DK, GDN_DK), F32)],
        compiler_params=pltpu.CompilerParams(dimension_semantics=("arbitrary",),
                                             vmem_limit_bytes=VMEM_LIMIT),
        name="gdn_scan",
    )(as3(u), as3(w), as3(qd), as3(kd), as3(attn), el4, gnw)
    return out.reshape(TOKENS, hd)


S5_PREP_GROUPS = 8
S5_PAIR_STATE = 2 * S5_STATE


def _s5_prep_kernel(lre_ref, lim_ref, ldt_ref, bre_ref, bim_ref, cre_ref, cim_ref, d_ref,
                    toep_ref, m1r_ref, m1i_ref, m3r_ref, m3i_ref, a16r_ref, a16i_ref):
    n = S5_CHUNK
    row = lax.broadcasted_iota(jnp.int32, (S5_W, S5_W), 0)
    col = lax.broadcasted_iota(jnp.int32, (S5_W, S5_W), 1)
    colblk = lax.broadcasted_iota(jnp.int32, (S5_W, 128), 1) // S5_GROUP
    for gi in range(S5_PREP_GROUPS):
        lr = jnp.minimum(lre_ref[gi], -S5_MIN_NEG)
        li = lim_ref[gi]
        dt = jnp.exp(ldt_ref[gi])
        mag = jnp.exp(lr * dt)
        ar = mag * jnp.cos(li * dt)
        ai = mag * jnp.sin(li * dt)
        den = lr * lr + li * li
        nr, ni = ar - 1.0, ai
        fr = (nr * lr + ni * li) / den
        fi = (ni * lr - nr * li) / den
        br, bi = bre_ref[gi], bim_ref[gi]
        bbr = fr * br - fi * bi
        bbi = fr * bi + fi * br
        pr = [jnp.ones_like(ar)]
        pi = [jnp.zeros_like(ar)]
        for _ in range(n):
            pr.append(pr[-1] * ar - pi[-1] * ai)
            pi.append(pr[-2] * ai + pi[-1] * ar)
        m1r = jnp.concatenate([pr[n - 1 - s] * bbr - pi[n - 1 - s] * bbi for s in range(n)], axis=0)
        m1i = jnp.concatenate([pr[n - 1 - s] * bbi + pi[n - 1 - s] * bbr for s in range(n)], axis=0)
        cr, ci = cre_ref[gi], cim_ref[gi]
        m3r = jnp.concatenate([cr * pr[t + 1] - ci * pi[t + 1] for t in range(n)], axis=0)
        m3i = jnp.concatenate([-(cr * pi[t + 1] + ci * pr[t + 1]) for t in range(n)], axis=0)
        cer = jnp.concatenate([cr] * n, axis=0)
        cei = jnp.concatenate([ci] * n, axis=0)
        rt = _dot3_nt(m1r, cer) - _dot3_nt(m1i, cei)
        halves = []
        for lt in range(S5_W // 128):
            rt_lt = rt[:, lt * 128:(lt + 1) * 128]
            part = jnp.zeros((S5_W, 128), F32)
            for tb in range(S5_LANE_BLOCKS):
                sh = (n - 1 - (lt * S5_LANE_BLOCKS + tb)) * S5_GROUP
                if sh:
                    shifted = jnp.concatenate([rt_lt[sh:], jnp.zeros((sh, 128), F32)], axis=0)
                else:
                    shifted = rt_lt
                part = jnp.where(colblk == tb, shifted, part)
            halves.append(part)
        toep = jnp.concatenate(halves, axis=1) + jnp.where(row == col, d_ref[gi], 0.0)
        toep_ref[gi] = toep.astype(toep_ref.dtype)
        m1r_ref[gi] = m1r.astype(m1r_ref.dtype)
        m1i_ref[gi] = m1i.astype(m1i_ref.dtype)
        m3r_ref[gi] = m3r.astype(m3r_ref.dtype)
        m3i_ref[gi] = m3i.astype(m3i_ref.dtype)
        own = (lax.broadcasted_iota(jnp.int32, ar.shape, 1) // S5_STATE) == (gi % 2)
        a16r_ref[gi] = jnp.where(own, pr[n], 0.0)
        a16i_ref[gi] = jnp.where(own, pi[n], 0.0)


def _s5_prep(lre, lim, ldt, bre_t, bim_t, cre, cim, drow):
    gp = S5_PREP_GROUPS
    blk = lambda *tail: pl.BlockSpec((gp,) + tail, lambda i: (i,) + (0,) * len(tail))
    p = S5_PAIR_STATE
    out_shape = (
        jax.ShapeDtypeStruct((S5_GROUPS, S5_W, S5_W), BF16),
        jax.ShapeDtypeStruct((S5_GROUPS, S5_W, p), BF16),
        jax.ShapeDtypeStruct((S5_GROUPS, S5_W, p), BF16),
        jax.ShapeDtypeStruct((S5_GROUPS, S5_W, p), BF16),
        jax.ShapeDtypeStruct((S5_GROUPS, S5_W, p), BF16),
        jax.ShapeDtypeStruct((S5_GROUPS, 1, p), F32),
        jax.ShapeDtypeStruct((S5_GROUPS, 1, p), F32),
    )
    return pl.pallas_call(
        _s5_prep_kernel,
        grid=(S5_GROUPS // gp,),
        in_specs=[blk(1, p), blk(1, p), blk(1, 1), blk(S5_GROUP, p), blk(S5_GROUP, p),
                  blk(S5_GROUP, p), blk(S5_GROUP, p), blk(1, S5_W)],
        out_specs=(blk(S5_W, S5_W), blk(S5_W, p), blk(S5_W, p), blk(S5_W, p), blk(S5_W, p),
                   blk(1, p), blk(1, p)),
        out_shape=out_shape,
        compiler_params=pltpu.CompilerParams(dimension_semantics=("arbitrary",),
                                             vmem_limit_bytes=VMEM_LIMIT),
        name="s5_prep",
    )(lre, lim, ldt, bre_t, bim_t, cre, cim, drow)


S5_PAIRS = 4
S5_PITCH = S5_CPS + 8


def _s5_kernel(u_ref, toep_ref, m1r_ref, m1i_ref, m3r_ref, m3i_ref, a16r_ref, a16i_ref,
               y_ref, zr_ref, zi_ref, xr_ref, xi_ref, pr_ref, pi_ref):
    p = S5_PAIR_STATE
    segs = S5_SEGS
    n_blk = BATCH * segs
    blk_rows = lambda blk: slice(blk * S5_PITCH, blk * S5_PITCH + S5_CPS)
    ar, ai = [], []
    for pp in range(S5_PAIRS):
        g0, g1 = 2 * pp, 2 * pp + 1
        m1 = lambda g: jnp.concatenate([m1r_ref[g], m1i_ref[g]], axis=1)
        z = _dot(u_ref[g0], m1(g0)) + _dot(u_ref[g1], m1(g1))
        zr, zi = z[:, :p], z[:, p:]
        for blk in range(n_blk):
            zr_ref[pp, blk_rows(blk), :] = zr[blk * S5_CPS:(blk + 1) * S5_CPS]
            zi_ref[pp, blk_rows(blk), :] = zi[blk * S5_CPS:(blk + 1) * S5_CPS]
        ar.append(jnp.broadcast_to(a16r_ref[g0] + a16r_ref[g1], (segs, p)))
        ai.append(jnp.broadcast_to(a16i_ref[g0] + a16i_ref[g1], (segs, p)))

    def scan_body(i, carry):
        new = []
        for pp in range(S5_PAIRS):
            pwr, pwi, st = carry[pp]
            pr_ref[pp, pl.ds(i, 1), :] = pwr[0:1]
            pi_ref[pp, pl.ds(i, 1), :] = pwi[0:1]
            new_st = []
            for b in range(BATCH):
                sr, si = st[b]
                rb = pl.ds(b * segs * S5_PITCH + i, segs, stride=S5_PITCH)
                xr_ref[pp, rb, :] = sr
                xi_ref[pp, rb, :] = si
                new_st.append((sr * ar[pp] - si * ai[pp] + zr_ref[pp, rb, :],
                               sr * ai[pp] + si * ar[pp] + zi_ref[pp, rb, :]))
            new.append((pwr * ar[pp] - pwi * ai[pp], pwr * ai[pp] + pwi * ar[pp], tuple(new_st)))
        return tuple(new)

    zero = jnp.zeros((segs, p), F32)
    init = tuple((jnp.ones((segs, p), F32), zero, tuple((zero, zero) for _ in range(BATCH)))
                 for _ in range(S5_PAIRS))
    final = lax.fori_loop(0, S5_CPS, scan_body, init)

    sub = lax.broadcasted_iota(jnp.int32, (segs, p), 0)
    for pp in range(S5_PAIRS):
        pwr, pwi, st = final[pp]
        pw_r = pr_ref[pp]
        pw_i = pi_ref[pp]
        xr_parts, xi_parts = [], []
        for b in range(BATCH):
            fr, fi = st[b]
            gr, gi = zero, zero
            for _ in range(segs - 1):
                nr = pwr * gr - pwi * gi + fr
                ni = pwr * gi + pwi * gr + fi
                gr = jnp.where(sub == 0, 0.0, pltpu.roll(nr, 1, 0))
                gi = jnp.where(sub == 0, 0.0, pltpu.roll(ni, 1, 0))
            for s in range(segs):
                rows = blk_rows(b * segs + s)
                g_r, g_i = gr[s:s + 1], gi[s:s + 1]
                xr_parts.append(xr_ref[pp, rows, :] + pw_r * g_r - pw_i * g_i)
                xi_parts.append(xi_ref[pp, rows, :] + pw_r * g_i + pw_i * g_r)
        x_b = jnp.concatenate([jnp.concatenate(xr_parts, axis=0), jnp.concatenate(xi_parts, axis=0)],
                              axis=1).astype(BF16)
        for g in (2 * pp, 2 * pp + 1):
            m3 = jnp.concatenate([m3r_ref[g], m3i_ref[g]], axis=1)
            y = _dot(u_ref[g], toep_ref[g]) + _dot_nt(x_b, m3)
            y_ref[g] = y.astype(y_ref.dtype)


def _s5(ug, toep, m1r, m1i, m3r, m3i, a16r, a16i):
    gb = 2 * S5_PAIRS
    p = S5_PAIR_STATE
    blk = lambda *tail: pl.BlockSpec((gb,) + tail, lambda i: (i,) + (0,) * len(tail))
    seg_rows = BATCH * S5_SEGS * S5_PITCH
    return pl.pallas_call(
        _s5_kernel,
        grid=(S5_GROUPS // gb,),
        in_specs=[blk(S5_ROWS, S5_W), blk(S5_W, S5_W), blk(S5_W, p), blk(S5_W, p), blk(S5_W, p),
                  blk(S5_W, p), blk(1, p), blk(1, p)],
        out_specs=blk(S5_ROWS, S5_W),
        out_shape=jax.ShapeDtypeStruct((S5_GROUPS, S5_ROWS, S5_W), BF16),
        scratch_shapes=[pltpu.VMEM((S5_PAIRS, seg_rows, p), F32) for _ in range(4)]
        + [pltpu.VMEM((S5_PAIRS, S5_CPS, p), F32) for _ in range(2)],
        compiler_params=pltpu.CompilerParams(dimension_semantics=("arbitrary",),
                                             vmem_limit_bytes=VMEM_LIMIT),
        name="s5",
    )(ug, toep, m1r, m1i, m3r, m3i, a16r, a16i)


def _gelu_tanh(x):
    return 0.5 * x * (1.0 + jnp.tanh(math.sqrt(2.0 / math.pi) * (x + 0.044715 * (x * x * x))))


OUT0_PARTS = 2


def _out0_kernel(x_ref, yg_ref, zb_ref, oa_ref, za_ref, wglu_hbm, wout_hbm, npost_ref, o_ref,
                 ys_ref, wglu_ref, wout_ref, stage_ref, sem):
    @pl.when(pl.program_id(0) == 0)
    def _():
        _load_weight_bf16(wglu_hbm, wglu_ref, stage_ref, sem)
        _load_weight_bf16(wout_hbm, wout_ref, stage_ref, sem)

    hd = GDN_HEADS * GDN_DK
    part_rows = TM // OUT0_PARTS
    chunks = part_rows // S5_CHUNK
    for part in range(OUT0_PARTS):
        r0 = part * part_rows
        rows = slice(r0, r0 + part_rows)
        crows = slice(part * chunks, (part + 1) * chunks)
        for j in range(D_MODEL // 128):
            for half in range(S5_CHUNK // S5_LANE_BLOCKS):
                ws = [yg_ref[j * S5_LANE_BLOCKS + g8, crows, half * 128:(half + 1) * 128].astype(F32)
                      for g8 in range(S5_LANE_BLOCKS)]
                ws = _block_transpose(ws)
                for a in range(S5_LANE_BLOCKS):
                    ys_ref[j, pl.ds(r0 + half * S5_LANE_BLOCKS + a, chunks, stride=S5_CHUNK), :] = ws[a]
        y = _gelu_tanh(jnp.concatenate([ys_ref[j, rows, :] for j in range(D_MODEL // 128)], axis=1))
        y = y * _sigmoid(_dot(y.astype(BF16), wglu_ref[...]))
        yb = (y * zb_ref[rows, :].astype(F32)).astype(BF16)
        ya = oa_ref[rows, :] * za_ref[rows, :]
        mix = _dot(ya, wout_ref[0:hd, :]) + _dot(yb, wout_ref[hd:, :])
        o_ref[rows, :] = x_ref[rows, :] + _rms_scale(mix) * npost_ref[...]


def _out0(x2, yg, zb, oa, za, wglu, wout, npost):
    tok = lambda cols: pl.BlockSpec((TM, cols), lambda i: (i, 0))
    yg_spec = pl.BlockSpec((S5_GROUPS, TM // S5_CHUNK, S5_W), lambda i: (0, i, 0))
    hbm = pl.BlockSpec(memory_space=pl.ANY)
    return pl.pallas_call(
        _out0_kernel,
        grid=(TOKENS // TM,),
        in_specs=[tok(D_MODEL), yg_spec, tok(D_MODEL), tok(D_MODEL), tok(D_MODEL), hbm, hbm,
                  _const_spec((1, D_MODEL))],
        out_specs=tok(D_MODEL),
        out_shape=jax.ShapeDtypeStruct((TOKENS, D_MODEL), F32),
        scratch_shapes=[pltpu.VMEM((D_MODEL // 128, TM, 128), F32),
                        pltpu.VMEM(wglu.shape, BF16), pltpu.VMEM(wout.shape, BF16)]
        + _stage_scratch(D_MODEL),
        compiler_params=pltpu.CompilerParams(dimension_semantics=("arbitrary",),
                                             vmem_limit_bytes=VMEM_LIMIT),
        name="out0",
    )(x2, yg, zb, oa, za, wglu, wout, npost)


def _layer1_kernel(x_ref, npre_ref, win_hbm, convw_ref, wout_hbm, npost_ref, o_ref,
                   ext_ref, win_ref, wout_ref, stage_a, sem_a, stage_b, sem_b):
    tiles_per_seq = SEQ // TM

    @pl.when(pl.program_id(0) == 0)
    def _():
        _load_weight_bf16(win_hbm, win_ref, stage_a, sem_a)
        _load_weight_bf16(wout_hbm, wout_ref, stage_b, sem_b)

    @pl.when(pl.program_id(0) % tiles_per_seq == 0)
    def _():
        ext_ref[0:HALO, :] = jnp.zeros((HALO, SC_WIDTH), F32)

    x = x_ref[...]
    h = (_rms_scale(x) * npre_ref[...]).astype(BF16)
    cw = 256
    acc = jnp.zeros((TM, D_MODEL), F32)
    for c in range(SC_WIDTH // cw):
        cols = slice(c * cw, (c + 1) * cw)
        part = lambda k: _dot(h, win_ref[:, k * SC_WIDTH + c * cw:k * SC_WIDTH + (c + 1) * cw])
        prod = part(1) * part(2)
        ext_ref[HALO:HALO + TM, cols] = prod
        w = convw_ref[:, cols]
        conv = prod * w[2:3]
        for j in range(SC_CONV - 1):
            shift = SC_CONV - 1 - j
            conv = conv + ext_ref[HALO - shift:HALO - shift + TM, cols] * w[j:j + 1]
        y = part(0) * conv * _silu(part(3))
        acc = acc + _dot(y.astype(BF16), wout_ref[cols, :])
    ext_ref[0:HALO, :] = ext_ref[TM:TM + HALO, :]
    o_ref[...] = x + _rms_scale(acc) * npost_ref[...]


def _layer1(x1, npre, win, convw, wout, npost):
    tok = lambda cols: pl.BlockSpec((TM, cols), lambda i: (i, 0))
    hbm = pl.BlockSpec(memory_space=pl.ANY)
    return pl.pallas_call(
        _layer1_kernel,
        grid=(TOKENS // TM,),
        in_specs=[tok(D_MODEL), _const_spec((1, D_MODEL)), hbm, _const_spec(convw.shape), hbm,
                  _const_spec((1, D_MODEL))],
        out_specs=tok(D_MODEL),
        out_shape=jax.ShapeDtypeStruct((TOKENS, D_MODEL), F32),
        scratch_shapes=[pltpu.VMEM((TM + HALO, SC_WIDTH), F32),
                        pltpu.VMEM(win.shape, BF16), pltpu.VMEM(wout.shape, BF16)]
        + _stage_scratch(win.shape[1]) + _stage_scratch(wout.shape[1]),
        compiler_params=pltpu.CompilerParams(dimension_semantics=("arbitrary",),
                                             vmem_limit_bytes=VMEM_LIMIT),
        name="layer1",
    )(x1, npre, win, convw, wout, npost)


def kernel(x, norm_pre, norm_post, w_in_even, conv_qkv, a_log, dt_bias, gdn_norm_w,
           s5_lam_re, s5_lam_im, s5_b_re, s5_b_im, s5_c_re, s5_c_im, s5_log_dt, s5_d,
           w_glu, w_out_even, w_in_odd, conv_short, w_out_odd):
    x2 = x.reshape(TOKENS, D_MODEL)
    hd = GDN_HEADS * GDN_DK

    w_t = jnp.swapaxes(w_in_even[0], 0, 1)
    pad_g = lambda v: jnp.pad(v, (GDN_HEADS, 128 - 2 * GDN_HEADS)).reshape(1, 128)

    q, k, v, za, bg, ug, zb = _proj0(x2, norm_pre[0].reshape(1, D_MODEL), w_t,
                                    conv_qkv[0], pad_g(a_log[0]), pad_g(dt_bias[0]))
    u_c, w_c, q_dec, k_dec, attn, e_last = _gdn_prep(q, k, v, bg)
    oa = _gdn_scan(u_c, w_c, q_dec, k_dec, attn, e_last, gdn_norm_w[0].reshape(1, GDN_DK))

    p = S5_STATE
    drow = jnp.tile(s5_d[0].reshape(S5_GROUPS, 1, S5_GROUP), (1, 1, S5_CHUNK))
    odd = (jnp.arange(S5_GROUPS) % 2 == 1).reshape(S5_GROUPS, 1, 1)

    def by_parity(a):
        z = jnp.zeros_like(a)
        return jnp.concatenate([jnp.where(odd, z, a), jnp.where(odd, a, z)], axis=-1)

    toep, m1r, m1i, m3r, m3i, a16r, a16i = _s5_prep(
        by_parity(s5_lam_re[0].reshape(S5_GROUPS, 1, p)),
        by_parity(s5_lam_im[0].reshape(S5_GROUPS, 1, p)),
        s5_log_dt[0].reshape(S5_GROUPS, 1, 1),
        by_parity(jnp.swapaxes(s5_b_re[0], 1, 2)), by_parity(jnp.swapaxes(s5_b_im[0], 1, 2)),
        by_parity(s5_c_re[0]), by_parity(s5_c_im[0]), drow)
    yg = _s5(ug, toep, m1r, m1i, m3r, m3i, a16r, a16i)

    x1 = _out0(x2, yg, zb, oa, za, w_glu[0], w_out_even[0], norm_post[0].reshape(1, D_MODEL))

    out = _layer1(x1, norm_pre[1].reshape(1, D_MODEL), w_in_odd[0], conv_short[0], w_out_odd[0],
                  norm_post[1].reshape(1, D_MODEL))
    return out.reshape(BATCH, SEQ, D_MODEL)
```
